```python
import jax, jax.numpy as jnp
from jax import lax
import numpy as np


D_MODEL = 2048
BATCH = 2
SEQ = 4096
DEPTH = 1
DEC_BATCH = 16
DEC_SEQ = 32
PAST_LEN = 2048

CHUNK = 64
D_MIX = D_MODEL
POOL_WIDTH = D_MIX // 2
POOL_WINDOWS = (2, 4, 8, 16)
POOL_GROUPS = 4
POOL_CH = POOL_WIDTH // POOL_GROUPS
POOL_HIST = max(POOL_WINDOWS) - 1
RET_WIDTH = D_MIX - POOL_WIDTH
RET_HEADS = 8
RET_HEAD_DIM = RET_WIDTH // RET_HEADS
ROPE_BASE = 10000.0
IN_WIDTH = POOL_WIDTH + 4 * RET_WIDTH
N_MEM = 256
MEM_HEADS = 4
MEM_HEAD_DIM = D_MODEL // MEM_HEADS
N_GROUPS = 4
EXPERTS_PER_GROUP = 8
TOP_K_IN_GROUP = 2
EXPERT_HIDDEN = D_MODEL // 4
LN_EPS = 1e-5
GN_EPS = 1e-6
DEEPNORM_ALPHA = (2.0 * DEPTH) ** 0.25
DEEPNORM_BETA = (8.0 * DEPTH) ** -0.25

kernel_name = 'hymba_pool_retention_hmoe_stream'


def layer_norm(x, g, b):
    xf = x.astype(jnp.float32)
    mu = jnp.mean(xf, axis=-1, keepdims=True)
    var = jnp.mean(jnp.square(xf - mu), axis=-1, keepdims=True)
    return ((xf - mu) * lax.rsqrt(var + LN_EPS) * g + b).astype(x.dtype)


def rotary(x, pos):
    half = x.shape[-1] // 2
    freqs = ROPE_BASE ** (-jnp.arange(half, dtype=jnp.float32) / half)
    ang = pos[:, None] * freqs[None, :]
    cos = jnp.cos(ang)[None, :, None, :]
    sin = jnp.sin(ang)[None, :, None, :]
    xf = x.astype(jnp.float32)
    x1, x2 = xf[..., :half], xf[..., half:]
    return jnp.concatenate([x1 * cos - x2 * sin, x1 * sin + x2 * cos], axis=-1)


def multiscale_pool(u_ext, pos0, w_pool, b_pool, pool_scale):
    B, L, C = u_ext.shape
    T = L - POOL_HIST
    uf = u_ext.astype(jnp.float32)
    cs = jnp.concatenate([jnp.zeros((B, 1, C), jnp.float32), jnp.cumsum(uf, axis=1)], axis=1)
    u_new = uf[:, POOL_HIST:]
    pos = pos0 + jnp.arange(T, dtype=jnp.float32)
    outs = []
    for gi, w in enumerate(POOL_WINDOWS):
        sl = slice(gi * POOL_CH, (gi + 1) * POOL_CH)
        lo = POOL_HIST + 1 - w
        win_sum = cs[:, POOL_HIST + 1:POOL_HIST + 1 + T, sl] - cs[:, lo:lo + T, sl]
        cnt = jnp.minimum(float(w), pos + 1.0)[None, :, None]
        d = win_sum / cnt - u_new[..., sl]
        outs.append(jnp.einsum('btc,ce->bte', d, w_pool[gi].astype(jnp.float32)) + b_pool[gi])
    return (jnp.concatenate(outs, axis=-1) * pool_scale).astype(u_ext.dtype)


def retention(q, k, v, s0, chunk):
    B, T, H, dk = q.shape
    dv = v.shape[-1]
    n = T // chunk
    log_gamma = jnp.log(1.0 - 2.0 ** (-5.0 - jnp.arange(H, dtype=jnp.float32)))
    qc = q.reshape(B, n, chunk, H, dk)
    kc = k.reshape(B, n, chunk, H, dk)
    vc = v.astype(jnp.float32).reshape(B, n, chunk, H, dv)
    idx = jnp.arange(chunk, dtype=jnp.float32)
    diff = idx[:, None] - idx[None, :]
    decay = jnp.where(diff >= 0, jnp.exp(log_gamma[:, None, None] * jnp.maximum(diff, 0.0)), 0.0)
    scores = jnp.einsum('bnihd,bnjhd->bnhij', qc, kc) * decay
    o_intra = jnp.einsum('bnhij,bnjhv->bnihv', scores, vc)
    k_decay = jnp.exp(log_gamma[None, :] * (chunk - 1.0 - idx)[:, None])
    upd = jnp.einsum('bnjhd,jh,bnjhv->nbhdv', kc, k_decay, vc)
    g_chunk = jnp.exp(log_gamma * chunk)[None, :, None, None]

    def step(s, u):
        return g_chunk * s + u, s

    s_final, s_prev = lax.scan(step, s0.astype(jnp.float32), upd)
    q_decay = jnp.exp(log_gamma[None, :] * (idx + 1.0)[:, None])
    o_cross = jnp.einsum('bnihd,ih,nbhdv->bnihv', qc, q_decay, s_prev)
    return (o_intra + o_cross).reshape(B, T, H, dv), s_final


def parallel_mixer(x, pool_hist, s0, pos0, chunk, w_in, w_pool, b_pool, pool_scale, ret_gn_g, ret_gn_b, w_out):
    B, T, _ = x.shape
    h = jnp.einsum('btd,de->bte', x, w_in)
    u = h[..., :POOL_WIDTH]
    q, k, v, g = jnp.split(h[..., POOL_WIDTH:], 4, axis=-1)
    u_ext = jnp.concatenate([pool_hist.astype(u.dtype), u], axis=1)
    pool_out = multiscale_pool(u_ext, pos0, w_pool, b_pool, pool_scale)
    new_hist = u_ext[:, -POOL_HIST:]
    pos = pos0 + jnp.arange(T, dtype=jnp.float32)
    qh = rotary(q.reshape(B, T, RET_HEADS, RET_HEAD_DIM), pos)
    kh = rotary(k.reshape(B, T, RET_HEADS, RET_HEAD_DIM), pos) * (RET_HEAD_DIM ** -0.5)
    vh = v.reshape(B, T, RET_HEADS, RET_HEAD_DIM)
    o, s_new = retention(qh, kh, vh, s0, chunk)
    mu = jnp.mean(o, axis=-1, keepdims=True)
    var = jnp.mean(jnp.square(o - mu), axis=-1, keepdims=True)
    on = ((o - mu) * lax.rsqrt(var + GN_EPS)).reshape(B, T, RET_WIDTH) * ret_gn_g + ret_gn_b
    ret_out = (on * jax.nn.silu(g.astype(jnp.float32))).astype(x.dtype)
    y = jnp.einsum('bte,ed->btd', jnp.concatenate([pool_out, ret_out], axis=-1), w_out)
    return y, new_hist, s_new


def memory_kv(mem, w_mk, w_mv):
    B, M, _ = mem.shape
    mk = jnp.einsum('bmd,de->bme', mem, w_mk).reshape(B, M, MEM_HEADS, MEM_HEAD_DIM)
    mv = jnp.einsum('bmd,de->bme', mem, w_mv).reshape(B, M, MEM_HEADS, MEM_HEAD_DIM)
    return mk, mv


def memory_attention(x, mk, mv, w_mq, w_mo):
    B, T, _ = x.shape
    q = jnp.einsum('btd,de->bte', x, w_mq).reshape(B, T, MEM_HEADS, MEM_HEAD_DIM)
    s = jnp.einsum('bthd,bmhd->bhtm', q, mk, preferred_element_type=jnp.float32) * (MEM_HEAD_DIM ** -0.5)
    p = jax.nn.softmax(s, axis=-1).astype(mv.dtype)
    o = jnp.einsum('bhtm,bmhd->bthd', p, mv).reshape(B, T, D_MODEL)
    return jnp.einsum('bte,ed->btd', o, w_mo)


def hier_moe(x, w_rg, w_re, w1, w3, w2):
    B, T, D = x.shape
    xt = x.reshape(B * T, D)
    g_prob = jax.nn.softmax(jnp.einsum('nd,dg->ng', xt, w_rg, preferred_element_type=jnp.float32), axis=-1)
    g_w, g_idx = lax.top_k(g_prob, 1)
    e_logits = jnp.einsum('nd,dge->nge', xt, w_re, preferred_element_type=jnp.float32)
    e_sel = jnp.take_along_axis(e_logits, g_idx[:, :, None], axis=1)[:, 0]
    e_w, e_idx = lax.top_k(jax.nn.softmax(e_sel, axis=-1), TOP_K_IN_GROUP)
    e_w = e_w / jnp.sum(e_w, axis=-1, keepdims=True)
    gate_e = jnp.sum(jax.nn.one_hot(e_idx, EXPERTS_PER_GROUP, dtype=jnp.float32) * e_w[..., None], axis=1) * g_w
    combine = jax.nn.one_hot(g_idx[:, 0], N_GROUPS, dtype=jnp.float32)[:, :, None] * gate_e[:, None, :]
    y = jnp.zeros_like(xt)
    for gi in range(N_GROUPS):
        a = jnp.einsum('nd,edf->nef', xt, w1[gi])
        b = jnp.einsum('nd,edf->nef', xt, w3[gi])
        hdn = jax.nn.silu(a) * b * combine[:, gi, :, None].astype(x.dtype)
        y = y + jnp.einsum('nef,efd->nd', hdn, w2[gi])
    return y.reshape(B, T, D)


def encoder_layer(x, pool_hist, s0, mk, mv, pos0, chunk, lw):
    (w_in, w_pool, b_pool, pool_scale, ret_gn_g, ret_gn_b, w_out, ln1_g, ln1_b,
     w_mq, w_mo, ln2_g, ln2_b, w_rg, w_re, w1, w3, w2, ln3_g, ln3_b) = lw
    y, new_hist, s_new = parallel_mixer(x, pool_hist, s0, pos0, chunk, w_in, w_pool, b_pool,
                                        pool_scale, ret_gn_g, ret_gn_b, w_out)
    x = layer_norm(DEEPNORM_ALPHA * x + y, ln1_g, ln1_b)
    x = layer_norm(DEEPNORM_ALPHA * x + memory_attention(x, mk, mv, w_mq, w_mo), ln2_g, ln2_b)
    x = layer_norm(DEEPNORM_ALPHA * x + hier_moe(x, w_rg, w_re, w1, w3, w2), ln3_g, ln3_b)
    return x, new_hist, s_new


def setup_inputs(seed: int = 0) -> dict:
    key = jax.random.key(seed)
    ks = jax.random.split(key, 40)
    D = D_MODEL

    def nrm(k, shape, scale):
        return jax.random.normal(k, shape, jnp.float32) * scale

    v_lo = POOL_WIDTH + 2 * RET_WIDTH
    col_scale = jnp.ones((IN_WIDTH,), jnp.float32).at[v_lo:v_lo + RET_WIDTH].set(DEEPNORM_BETA)
    w_in = nrm(ks[0], (DEPTH, D, IN_WIDTH), D ** -0.5) * col_scale
    return {
        'x_prompt': nrm(ks[1], (BATCH, SEQ, D), 1.0),
        'x_sample': nrm(ks[2], (DEC_BATCH, DEC_SEQ, D), 1.0),
        'mem_prompt': nrm(ks[3], (BATCH, N_MEM, D), 1.0),
        'state_pool': nrm(ks[4], (DEPTH, DEC_BATCH, POOL_HIST, POOL_WIDTH), 1.0),
        'state_ret': nrm(ks[5], (DEPTH, DEC_BATCH, RET_HEADS, RET_HEAD_DIM, RET_HEAD_DIM), 0.5),
        'cache_mem_k': nrm(ks[6], (DEPTH, DEC_BATCH, N_MEM, MEM_HEADS, MEM_HEAD_DIM), 1.0),
        'cache_mem_v': nrm(ks[7], (DEPTH, DEC_BATCH, N_MEM, MEM_HEADS, MEM_HEAD_DIM), DEEPNORM_BETA),
        'w_in': w_in,
        'w_pool': nrm(ks[8], (DEPTH, POOL_GROUPS, POOL_CH, POOL_CH), POOL_CH ** -0.5),
        'b_pool': nrm(ks[9], (DEPTH, POOL_GROUPS, POOL_CH), 0.02),
        'pool_scale': 1.0 + nrm(ks[10], (DEPTH, POOL_WIDTH), 0.1),
        'ret_gn_g': 1.0 + nrm(ks[11], (DEPTH, RET_WIDTH), 0.1),
        'ret_gn_b': nrm(ks[12], (DEPTH, RET_WIDTH), 0.02),
        'w_out': nrm(ks[13], (DEPTH, D_MIX, D), DEEPNORM_BETA * D_MIX ** -0.5),
        'ln1_g': 1.0 + nrm(ks[14], (DEPTH, D), 0.1),
        'ln1_b': nrm(ks[15], (DEPTH, D), 0.02),
        'w_mq': nrm(ks[16], (DEPTH, D, D), D ** -0.5),
        'w_mk': nrm(ks[17], (DEPTH, D, D), D ** -0.5),
        'w_mv': nrm(ks[18], (DEPTH, D, D), DEEPNORM_BETA * D ** -0.5),
        'w_mo': nrm(ks[19], (DEPTH, D, D), DEEPNORM_BETA * D ** -0.5),
        'ln2_g': 1.0 + nrm(ks[20], (DEPTH, D), 0.1),
        'ln2_b': nrm(ks[21], (DEPTH, D), 0.02),
        'w_rg': nrm(ks[22], (DEPTH, D, N_GROUPS), D ** -0.5),
        'w_re': nrm(ks[23], (DEPTH, D, N_GROUPS, EXPERTS_PER_GROUP), D ** -0.5),
        'w1': nrm(ks[24], (DEPTH, N_GROUPS, EXPERTS_PER_GROUP, D, EXPERT_HIDDEN), D ** -0.5),
        'w3': nrm(ks[25], (DEPTH, N_GROUPS, EXPERTS_PER_GROUP, D, EXPERT_HIDDEN), D ** -0.5),
        'w2': nrm(ks[26], (DEPTH, N_GROUPS, EXPERTS_PER_GROUP, EXPERT_HIDDEN, D), DEEPNORM_BETA * EXPERT_HIDDEN ** -0.5),
        'ln3_g': 1.0 + nrm(ks[27], (DEPTH, D), 0.1),
        'ln3_b': nrm(ks[28], (DEPTH, D), 0.02),
    }


def reference(x_prompt, x_sample, mem_prompt, state_pool, state_ret, cache_mem_k, cache_mem_v,
              w_in, w_pool, b_pool, pool_scale, ret_gn_g, ret_gn_b, w_out, ln1_g, ln1_b,
              w_mq, w_mk, w_mv, w_mo, ln2_g, ln2_b, w_rg, w_re, w1, w3, w2, ln3_g, ln3_b):
    xp = x_prompt
    xs = x_sample
    bp = xp.shape[0]
    hist_p = jnp.zeros((bp, POOL_HIST, POOL_WIDTH), xp.dtype)
    s_p = jnp.zeros((bp, RET_HEADS, RET_HEAD_DIM, RET_HEAD_DIM), jnp.float32)
    pool_p, ret_p, mk_p_all, mv_p_all, pool_s, ret_s = [], [], [], [], [], []
    for l in range(DEPTH):
        lw = (w_in[l], w_pool[l], b_pool[l], pool_scale[l], ret_gn_g[l], ret_gn_b[l], w_out[l],
              ln1_g[l], ln1_b[l], w_mq[l], w_mo[l], ln2_g[l], ln2_b[l], w_rg[l], w_re[l],
              w1[l], w3[l], w2[l], ln3_g[l], ln3_b[l])
        mk_p, mv_p = memory_kv(mem_prompt, w_mk[l], w_mv[l])
        xp, hp, sp = encoder_layer(xp, hist_p, s_p, mk_p, mv_p, 0, CHUNK, lw)
        xs, hs, ss = encoder_layer(xs, state_pool[l], state_ret[l], cache_mem_k[l], cache_mem_v[l],
                                   PAST_LEN, xs.shape[1], lw)
        pool_p.append(hp)
        ret_p.append(sp)
        mk_p_all.append(mk_p)
        mv_p_all.append(mv_p)
        pool_s.append(hs)
        ret_s.append(ss)
    return (xp, xs, jnp.stack(pool_p), jnp.stack(ret_p), jnp.stack(mk_p_all), jnp.stack(mv_p_all),
            jnp.stack(pool_s), jnp.stack(ret_s))
```

```python
import functools
import math

import jax
import jax.numpy as jnp
from jax import lax
from jax.experimental import pallas as pl
from jax.experimental.pallas import tpu as pltpu

F32 = jnp.float32
BF16 = jnp.bfloat16

D_MODEL = 2048
POOL_WIDTH = 1024
POOL_WINDOWS = (2, 4, 8, 16)
POOL_CH = 256
POOL_HIST = 15
RET_WIDTH = 1024
RET_HEADS = 8
RET_HEAD_DIM = 128
IN_WIDTH = POOL_WIDTH + 4 * RET_WIDTH
ROPE_BASE = 10000.0
N_MEM = 256
MEM_HEADS = 4
MEM_HEAD_DIM = 512
N_GROUPS = 4
EXPERTS_PER_GROUP = 8
N_EXPERTS = N_GROUPS * EXPERTS_PER_GROUP
EXPERT_HIDDEN = 512
LN_EPS = 1e-5
GN_EPS = 1e-6
DEPTH = 1
ALPHA = (2.0 * DEPTH) ** 0.25
PAST_LEN = 2048
PROMPT_CHUNK = 64

LANES = 128
HIST_PAD = 16
ROW_TILE = 512
MOE_TILE = 256
COMBINE_TILE = 256
VMEM_LIMIT = 58 * 1024 * 1024

_NEG_INF = float("-inf")


def _const_spec(shape):
    zeros = (0,) * len(shape)
    return pl.BlockSpec(shape, lambda *_: zeros, pipeline_mode=pl.Buffered(1))


def _when(cond, fn):
    if cond is True:
        fn()
    else:
        pl.when(cond)(fn)


def _layer_norm(z, g, b):
    mu = jnp.mean(z, axis=-1, keepdims=True)
    zc = z - mu
    var = jnp.mean(zc * zc, axis=-1, keepdims=True)
    return zc * lax.rsqrt(var + LN_EPS) * g + b


def _silu(a):
    return a * (1.0 / (1.0 + jnp.exp(-a)))


def _mixer_kernel(x_ref, w_in_ref, cos_ref, sin_ref, dec_ref, kd_ref, qd_ref, wp_ref, bp_ref, ps_ref,
                  gng_ref, gnb_ref, s0_ref, h0_ref, cat_ref, snew_ref, hnew_ref, h_ref, s_ref, u_ref,
                  *, seg_len, chunk, n_seg, tiles_per_stream, pos0, g_chunk):
    g = pl.program_id(0)
    j = pl.program_id(1)

    @pl.when(j == 0)
    def _project():
        h_ref[...] = jnp.dot(x_ref[...].astype(BF16), w_in_ref[...], preferred_element_type=F32)

    if n_seg == 1:
        r0 = 0
        first_of_stream = (g % tiles_per_stream) == 0
        last_of_stream = (g % tiles_per_stream) == tiles_per_stream - 1
        t_start = (g % tiles_per_stream) * seg_len
    else:
        r0 = pl.multiple_of(j * seg_len, seg_len)
        first_of_stream = True
        last_of_stream = True
        t_start = 0

    @functools.partial(_when, first_of_stream)
    def _init_state():
        s_ref[...] = s0_ref[0]
        u_ref[0:1, :] = jnp.zeros((1, POOL_WIDTH), F32)
        u_ref[1:HIST_PAD, :] = h0_ref[0]

    u_new = h_ref[pl.ds(r0, seg_len), 0:POOL_WIDTH]
    u_ref[HIST_PAD:HIST_PAD + seg_len, :] = u_new
    pos = (pos0 + t_start + lax.broadcasted_iota(jnp.int32, (seg_len, 1), 0)).astype(F32)
    for gi, w in enumerate(POOL_WINDOWS):
        cols = slice(gi * POOL_CH, (gi + 1) * POOL_CH)
        win = u_ref[HIST_PAD:HIST_PAD + seg_len, cols]
        for back in range(1, w):
            win = win + u_ref[HIST_PAD - back:HIST_PAD - back + seg_len, cols]
        cnt = jnp.minimum(float(w), pos + 1.0)
        d = win * (1.0 / cnt) - u_ref[HIST_PAD:HIST_PAD + seg_len, cols]
        pooled = jnp.dot(d.astype(BF16), wp_ref[gi], preferred_element_type=F32) + bp_ref[:, cols]
        cat_ref[:, cols] = (pooled * ps_ref[:, cols]).astype(BF16)

    @functools.partial(_when, last_of_stream)
    def _emit_hist():
        hnew_ref[0] = u_ref[seg_len + 1:seg_len + HIST_PAD, :]

    u_ref[0:HIST_PAD, :] = u_ref[seg_len:seg_len + HIST_PAD, :]

    scale = RET_HEAD_DIM ** -0.5
    for c in range(seg_len // chunk):
        rows = pl.ds(r0 + c * chunk, chunk)
        trows = slice(c * chunk, (c + 1) * chunk)
        cos_t = cos_ref[trows, :]
        sin_t = sin_ref[trows, :]
        for hd in range(RET_HEADS):
            lo = hd * RET_HEAD_DIM
            hcols = slice(lo, lo + RET_HEAD_DIM)
            q = h_ref[rows, POOL_WIDTH + lo:POOL_WIDTH + lo + RET_HEAD_DIM]
            k = h_ref[rows, POOL_WIDTH + RET_WIDTH + lo:POOL_WIDTH + RET_WIDTH + lo + RET_HEAD_DIM]
            v = h_ref[rows, POOL_WIDTH + 2 * RET_WIDTH + lo:POOL_WIDTH + 2 * RET_WIDTH + lo + RET_HEAD_DIM]
            gate = h_ref[rows, POOL_WIDTH + 3 * RET_WIDTH + lo:POOL_WIDTH + 3 * RET_WIDTH + lo + RET_HEAD_DIM]
            qr = q * cos_t + pltpu.roll(q, RET_HEAD_DIM // 2, 1) * sin_t
            kr = (k * cos_t + pltpu.roll(k, RET_HEAD_DIM // 2, 1) * sin_t) * scale
            vb = v.astype(BF16)
            scores = lax.dot_general(qr.astype(BF16), kr.astype(BF16), (((1,), (1,)), ((), ())),
                                     preferred_element_type=F32) * dec_ref[hd]
            o = jnp.dot(scores.astype(BF16), vb, preferred_element_type=F32)
            s_prev = s_ref[hd]
            o = o + jnp.dot((qr * qd_ref[hd]).astype(BF16), s_prev.astype(BF16), preferred_element_type=F32)
            upd = lax.dot_general((kr * kd_ref[hd]).astype(BF16), vb, (((0,), (0,)), ((), ())),
                                  preferred_element_type=F32)
            s_ref[hd] = g_chunk[hd] * s_prev + upd
            mu = jnp.mean(o, axis=-1, keepdims=True)
            oc = o - mu
            var = jnp.mean(oc * oc, axis=-1, keepdims=True)
            on = oc * lax.rsqrt(var + GN_EPS) * gng_ref[:, hcols] + gnb_ref[:, hcols]
            cat_ref[trows, POOL_WIDTH + lo:POOL_WIDTH + lo + RET_HEAD_DIM] = (on * _silu(gate)).astype(BF16)

    @functools.partial(_when, last_of_stream)
    def _emit_state():
        snew_ref[0] = s_ref[...]


def _retention_tables(chunk, t_len, pos0):
    log_gamma = jnp.log(1.0 - 2.0 ** (-5.0 - jnp.arange(RET_HEADS, dtype=F32)))
    idx = jnp.arange(chunk, dtype=F32)
    diff = idx[:, None] - idx[None, :]
    dec = jnp.where(diff >= 0, jnp.exp(log_gamma[:, None, None] * jnp.maximum(diff, 0.0)), 0.0)
    kd = jnp.exp(log_gamma[:, None] * (chunk - 1.0 - idx)[None, :])
    qd = jnp.exp(log_gamma[:, None] * (idx + 1.0)[None, :])
    kd = jnp.broadcast_to(kd[:, :, None], (RET_HEADS, chunk, RET_HEAD_DIM))
    qd = jnp.broadcast_to(qd[:, :, None], (RET_HEADS, chunk, RET_HEAD_DIM))
    half = RET_HEAD_DIM // 2
    freqs = ROPE_BASE ** (-jnp.arange(half, dtype=F32) / half)
    pos = pos0 + jnp.arange(t_len, dtype=F32)
    ang = pos[:, None] * freqs[None, :]
    cos = jnp.cos(ang)
    sin = jnp.sin(ang)
    cos_t = jnp.concatenate([cos, cos], axis=-1)
    sin_t = jnp.concatenate([-sin, sin], axis=-1)
    g_chunk = tuple(math.exp(math.log(1.0 - 2.0 ** (-5.0 - h)) * chunk) for h in range(RET_HEADS))
    return dec, kd, qd, cos_t, sin_t, g_chunk


def _mixer(x2d, w_in_b, wp_b, bp, ps, gng, gnb, s0, h0, *, n_streams, t_len, tile_rows, seg_len, chunk, pos0):
    rows = n_streams * t_len
    n_tiles = rows // tile_rows
    n_seg = tile_rows // seg_len
    tiles_per_stream = max(t_len // tile_rows, 1)
    assert n_seg == 1 or (seg_len == t_len and n_tiles == 1)
    assert seg_len % chunk == 0 and t_len >= POOL_HIST
    dec, kd, qd, cos_t, sin_t, g_chunk = _retention_tables(chunk, t_len, pos0)

    if n_seg == 1:
        stream_of = lambda g, j: g // tiles_per_stream
        time_of = lambda g, j: g % tiles_per_stream
    else:
        stream_of = lambda g, j: j
        time_of = lambda g, j: 0

    kern = functools.partial(_mixer_kernel, seg_len=seg_len, chunk=chunk, n_seg=n_seg,
                             tiles_per_stream=tiles_per_stream, pos0=pos0, g_chunk=g_chunk)
    return pl.pallas_call(
        kern,
        grid=(n_tiles, n_seg),
        in_specs=[
            pl.BlockSpec((tile_rows, D_MODEL), lambda g, j: (g, 0)),
            _const_spec((D_MODEL, IN_WIDTH)),
            pl.BlockSpec((seg_len, RET_HEAD_DIM), lambda g, j: (time_of(g, j), 0)),
            pl.BlockSpec((seg_len, RET_HEAD_DIM), lambda g, j: (time_of(g, j), 0)),
            _const_spec((RET_HEADS, chunk, chunk)),
            _const_spec((RET_HEADS, chunk, RET_HEAD_DIM)),
            _const_spec((RET_HEADS, chunk, RET_HEAD_DIM)),
            _const_spec((len(POOL_WINDOWS), POOL_CH, POOL_CH)),
            _const_spec((1, POOL_WIDTH)),
            _const_spec((1, POOL_WIDTH)),
            _const_spec((1, RET_WIDTH)),
            _const_spec((1, RET_WIDTH)),
            pl.BlockSpec((1, RET_HEADS, RET_HEAD_DIM, RET_HEAD_DIM), lambda g, j: (stream_of(g, j), 0, 0, 0)),
            pl.BlockSpec((1, POOL_HIST, POOL_WIDTH), lambda g, j: (stream_of(g, j), 0, 0)),
        ],
        out_specs=[
            pl.BlockSpec((seg_len, D_MODEL), lambda g, j: (g * n_seg + j, 0)),
            pl.BlockSpec((1, RET_HEADS, RET_HEAD_DIM, RET_HEAD_DIM), lambda g, j: (stream_of(g, j), 0, 0, 0)),
            pl.BlockSpec((1, POOL_HIST, POOL_WIDTH), lambda g, j: (stream_of(g, j), 0, 0)),
        ],
        out_shape=[
            jax.ShapeDtypeStruct((rows, D_MODEL), BF16),
            jax.ShapeDtypeStruct((n_streams, RET_HEADS, RET_HEAD_DIM, RET_HEAD_DIM), F32),
            jax.ShapeDtypeStruct((n_streams, POOL_HIST, POOL_WIDTH), F32),
        ],
        scratch_shapes=[
            pltpu.VMEM((tile_rows, IN_WIDTH), F32),
            pltpu.VMEM((RET_HEADS, RET_HEAD_DIM, RET_HEAD_DIM), F32),
            pltpu.VMEM((HIST_PAD + seg_len, POOL_WIDTH), F32),
        ],
        compiler_params=pltpu.CompilerParams(dimension_semantics=("arbitrary", "arbitrary"),
                                             vmem_limit_bytes=VMEM_LIMIT),
        name="mixer",
    )(x2d, w_in_b, cos_t, sin_t, dec, kd, qd, wp_b, bp, ps, gng, gnb, s0, h0)


def _row_sources(arrays, n_tiles_first):
    if len(arrays) == 1:
        return [pl.BlockSpec((ROW_TILE, arrays[0].shape[1]), lambda i: (i, 0))]
    first, second = arrays
    return [
        pl.BlockSpec((ROW_TILE, first.shape[1]), lambda i: (jnp.minimum(i, n_tiles_first - 1), 0)),
        pl.BlockSpec((ROW_TILE, second.shape[1]), lambda i: (jnp.maximum(i - n_tiles_first, 0), 0)),
    ]


def _mm_ln_kernel(*refs, n_a, n_res, n_tiles_first):
    a_refs = refs[:n_a]
    w_ref = refs[n_a]
    res_refs = refs[n_a + 1:n_a + 1 + n_res]
    g_ref, b_ref, o_ref = refs[n_a + 1 + n_res:]

    def body(a_ref, res_ref):
        acc = jnp.dot(a_ref[...].astype(BF16), w_ref[...], preferred_element_type=F32)
        o_ref[...] = _layer_norm(ALPHA * res_ref[...] + acc, g_ref[...], b_ref[...])

    i = pl.program_id(0)
    pl.when(i < n_tiles_first)(lambda: body(a_refs[0], res_refs[0]))
    pl.when(i >= n_tiles_first)(lambda: body(a_refs[-1], res_refs[-1]))


def _mm_ln(a_arrays, w_b, res_arrays, g, b, *, n_rows, n_tiles_first, name):
    n_tiles = n_rows // ROW_TILE
    kern = functools.partial(_mm_ln_kernel, n_a=len(a_arrays), n_res=len(res_arrays),
                             n_tiles_first=n_tiles_first)
    return pl.pallas_call(
        kern,
        grid=(n_tiles,),
        in_specs=(_row_sources(a_arrays, n_tiles_first) + [_const_spec(w_b.shape)]
                  + _row_sources(res_arrays, n_tiles_first)
                  + [_const_spec((1, D_MODEL)), _const_spec((1, D_MODEL))]),
        out_specs=pl.BlockSpec((ROW_TILE, D_MODEL), lambda i: (i, 0)),
        out_shape=jax.ShapeDtypeStruct((n_rows, D_MODEL), F32),
        compiler_params=pltpu.CompilerParams(dimension_semantics=("arbitrary",),
                                             vmem_limit_bytes=VMEM_LIMIT),
        name=name,
    )(*a_arrays, w_b, *res_arrays, g, b)


def _matmul_kernel(a_ref, w_ref, o_ref):
    o_ref[...] = jnp.dot(a_ref[...].astype(BF16), w_ref[...], preferred_element_type=F32).astype(o_ref.dtype)


def _matmul(a, w_b, out_dtype, name):
    n_rows = a.shape[0]
    return pl.pallas_call(
        _matmul_kernel,
        grid=(n_rows // ROW_TILE,),
        in_specs=[pl.BlockSpec((ROW_TILE, a.shape[1]), lambda i: (i, 0)), _const_spec(w_b.shape)],
        out_specs=pl.BlockSpec((ROW_TILE, w_b.shape[1]), lambda i: (i, 0)),
        out_shape=jax.ShapeDtypeStruct((n_rows, w_b.shape[1]), out_dtype),
        compiler_params=pltpu.CompilerParams(dimension_semantics=("arbitrary",),
                                             vmem_limit_bytes=VMEM_LIMIT),
        name=name,
    )(a, w_b)


def _attention_kernel(q_ref, k_ref, v_ref, o_ref):
    scale = MEM_HEAD_DIM ** -0.5
    for h in range(MEM_HEADS):
        cols = slice(h * MEM_HEAD_DIM, (h + 1) * MEM_HEAD_DIM)
        s = lax.dot_general(q_ref[:, cols], k_ref[0, :, cols].astype(BF16), (((1,), (1,)), ((), ())),
                            preferred_element_type=F32) * scale
        m = jnp.max(s, axis=-1, keepdims=True)
        p = jnp.exp(s - m)
        p = p * (1.0 / jnp.sum(p, axis=-1, keepdims=True))
        o_ref[:, cols] = jnp.dot(p.astype(BF16), v_ref[0, :, cols].astype(BF16),
                                 preferred_element_type=F32).astype(o_ref.dtype)


def _attention(q_all, mem_k, mem_v, *, n_streams, t_len, q_rows, row_offset, name):
    tiles_per_stream = t_len // q_rows
    base = row_offset // q_rows
    return pl.pallas_call(
        _attention_kernel,
        grid=(n_streams, tiles_per_stream),
        in_specs=[
            pl.BlockSpec((q_rows, D_MODEL), lambda b, t: (base + b * tiles_per_stream + t, 0)),
            pl.BlockSpec((1, N_MEM, D_MODEL), lambda b, t: (b, 0, 0)),
            pl.BlockSpec((1, N_MEM, D_MODEL), lambda b, t: (b, 0, 0)),
        ],
        out_specs=pl.BlockSpec((q_rows, D_MODEL), lambda b, t: (b * tiles_per_stream + t, 0)),
        out_shape=jax.ShapeDtypeStruct((n_streams * t_len, D_MODEL), BF16),
        compiler_params=pltpu.CompilerParams(dimension_semantics=("arbitrary", "arbitrary"),
                                             vmem_limit_bytes=VMEM_LIMIT),
        name=name,
    )(q_all, mem_k, mem_v)


_GROUP_LANE0 = 0
_EXPERT_LANE0 = N_GROUPS
(_META_E1, _META_E2, _META_POS1, _META_POS2, _META_GATE1, _META_GATE2) = range(6)


def _router_kernel(x_ref, wr_ref, meta_ref, counts_ref, carry_ref):
    i = pl.program_id(0)

    @pl.when(i == 0)
    def _():
        carry_ref[...] = jnp.zeros_like(carry_ref)

    logits = jnp.dot(x_ref[...], wr_ref[...], preferred_element_type=F32, precision=lax.Precision.HIGHEST)
    lane = lax.broadcasted_iota(jnp.int32, logits.shape, 1)

    def first_argmax(vals):
        m = jnp.max(vals, axis=-1, keepdims=True)
        idx = jnp.min(jnp.where(vals == m, lane, LANES), axis=-1, keepdims=True)
        return m, idx

    gl = jnp.where(lane < N_GROUPS, logits, _NEG_INF)
    gm, g_idx = first_argmax(gl)
    g_w = 1.0 / jnp.sum(jnp.exp(gl - gm), axis=-1, keepdims=True)

    in_group = ((lane >= _EXPERT_LANE0) & (lane < _EXPERT_LANE0 + N_EXPERTS)
                & (((lane - _EXPERT_LANE0) >> 3) == g_idx))
    el = jnp.where(in_group, logits, _NEG_INF)
    m1, i1 = first_argmax(el)
    z = jnp.sum(jnp.exp(el - m1), axis=-1, keepdims=True)
    m2, i2 = first_argmax(jnp.where(lane == i1, _NEG_INF, el))
    p1 = 1.0 / z
    p2 = jnp.exp(m2 - m1) / z
    den = p1 + p2
    gate1 = p1 / den * g_w
    gate2 = p2 / den * g_w

    hit1 = lane == i1
    hit2 = lane == i2
    onehot = (hit1 | hit2).astype(BF16)
    rows = logits.shape[0]
    earlier = (lax.broadcasted_iota(jnp.int32, (rows, rows), 1)
               < lax.broadcasted_iota(jnp.int32, (rows, rows), 0)).astype(BF16)
    rank = jnp.dot(earlier, onehot, preferred_element_type=F32) + carry_ref[...]
    pos1 = jnp.sum(jnp.where(hit1, rank, 0.0), axis=-1, keepdims=True)
    pos2 = jnp.sum(jnp.where(hit2, rank, 0.0), axis=-1, keepdims=True)
    carry_ref[...] = carry_ref[...] + jnp.sum(onehot.astype(F32), axis=0, keepdims=True)
    counts_ref[...] = carry_ref[...]

    meta = jnp.zeros_like(logits)
    for col, val in ((_META_E1, (i1 - _EXPERT_LANE0).astype(F32)), (_META_E2, (i2 - _EXPERT_LANE0).astype(F32)),
                     (_META_POS1, pos1), (_META_POS2, pos2), (_META_GATE1, gate1), (_META_GATE2, gate2)):
        meta = jnp.where(lane == col, val, meta)
    meta_ref[...] = meta


def _router(x_all, w_router):
    n_rows = x_all.shape[0]
    return pl.pallas_call(
        _router_kernel,
        grid=(n_rows // ROW_TILE,),
        in_specs=[pl.BlockSpec((ROW_TILE, D_MODEL), lambda i: (i, 0)), _const_spec((D_MODEL, LANES))],
        out_specs=[pl.BlockSpec((ROW_TILE, LANES), lambda i: (i, 0)), pl.BlockSpec((1, LANES), lambda i: (0, 0))],
        out_shape=[jax.ShapeDtypeStruct((n_rows, LANES), F32), jax.ShapeDtypeStruct((1, LANES), F32)],
        scratch_shapes=[pltpu.VMEM((1, LANES), F32)],
        compiler_params=pltpu.CompilerParams(dimension_semantics=("arbitrary",),
                                             vmem_limit_bytes=VMEM_LIMIT),
        name="router",
    )(x_all, w_router)


def _row_copy(src_hbm, row, dst_buf, slot, r, sem):
    return pltpu.make_async_copy(src_hbm.at[pl.ds(row, 1), :], dst_buf.at[slot, pl.ds(r, 1), :], sem.at[slot])


def _experts_kernel(tile_expert_ref, n_active_ref, src_ref, x_hbm, w1_ref, w3_ref, w2_ref, y_ref, xbuf, sem):
    del tile_expert_ref
    i = pl.program_id(0)
    n_active = n_active_ref[0]

    def start_gather(tile, slot):
        def body(r, carry):
            _row_copy(x_hbm, src_ref[tile * MOE_TILE + r], xbuf, slot, r, sem).start()
            return carry
        lax.fori_loop(0, MOE_TILE, body, 0, unroll=8)

    @pl.when(i == 0)
    def _():
        start_gather(0, 0)

    @pl.when(i + 1 < n_active)
    def _():
        start_gather(i + 1, (i + 1) % 2)

    @pl.when(i < n_active)
    def _():
        slot = i % 2

        def wait_row(r, carry):
            _row_copy(x_hbm, 0, xbuf, slot, r, sem).wait()
            return carry
        lax.fori_loop(0, MOE_TILE, wait_row, 0, unroll=8)

        xb = xbuf[slot].astype(BF16)
        a = jnp.dot(xb, w1_ref[0].astype(BF16), preferred_element_type=F32)
        b = jnp.dot(xb, w3_ref[0].astype(BF16), preferred_element_type=F32)
        hdn = (_silu(a) * b).astype(BF16)
        y_ref[...] = jnp.dot(hdn, w2_ref[0].astype(BF16), preferred_element_type=F32)

    @pl.when(i >= n_active)
    def _():
        y_ref[...] = jnp.zeros_like(y_ref)


def _experts(tile_expert, n_active, src, x_all, w1, w3, w2, n_tiles):
    return pl.pallas_call(
        _experts_kernel,
        grid_spec=pltpu.PrefetchScalarGridSpec(
            num_scalar_prefetch=3,
            grid=(n_tiles,),
            in_specs=[
                pl.BlockSpec(memory_space=pl.ANY),
                pl.BlockSpec((1, D_MODEL, EXPERT_HIDDEN), lambda i, te, na, sr: (te[i], 0, 0)),
                pl.BlockSpec((1, D_MODEL, EXPERT_HIDDEN), lambda i, te, na, sr: (te[i], 0, 0)),
                pl.BlockSpec((1, EXPERT_HIDDEN, D_MODEL), lambda i, te, na, sr: (te[i], 0, 0)),
            ],
            out_specs=pl.BlockSpec((MOE_TILE, D_MODEL), lambda i, te, na, sr: (i, 0)),
            scratch_shapes=[pltpu.VMEM((2, MOE_TILE, D_MODEL), F32), pltpu.SemaphoreType.DMA((2,))],
        ),
        out_shape=jax.ShapeDtypeStruct((n_tiles * MOE_TILE, D_MODEL), F32),
        compiler_params=pltpu.CompilerParams(dimension_semantics=("arbitrary",),
                                             vmem_limit_bytes=VMEM_LIMIT),
        name="experts",
    )(tile_expert, n_active, src, x_all, w1, w3, w2)


def _combine_kernel(slot_ref, y_hbm, gates_ref, x_ref, g_ref, b_ref, op_ref, os_ref, ybuf, sem,
                    *, n_tiles, n_tiles_first):
    i = pl.program_id(0)

    def copies(tile, buf, r):
        tok = tile * COMBINE_TILE + r
        return (
            pltpu.make_async_copy(y_hbm.at[pl.ds(slot_ref[2 * tok], 1), :],
                                  ybuf.at[buf, 0, pl.ds(r, 1), :], sem.at[buf]),
            pltpu.make_async_copy(y_hbm.at[pl.ds(slot_ref[2 * tok + 1], 1), :],
                                  ybuf.at[buf, 1, pl.ds(r, 1), :], sem.at[buf]),
        )

    def start_gather(tile, buf):
        def body(r, carry):
            c0, c1 = copies(tile, buf, r)
            c0.start()
            c1.start()
            return carry
        lax.fori_loop(0, COMBINE_TILE, body, 0, unroll=8)

    @pl.when(i == 0)
    def _():
        start_gather(0, 0)

    @pl.when(i + 1 < n_tiles)
    def _():
        start_gather(i + 1, (i + 1) % 2)

    buf = i % 2

    def wait_row(r, carry):
        c0, c1 = copies(i, buf, r)
        c0.wait()
        c1.wait()
        return carry
    lax.fori_loop(0, COMBINE_TILE, wait_row, 0, unroll=8)

    y = gates_ref[:, 0:1] * ybuf[buf, 0] + gates_ref[:, 1:2] * ybuf[buf, 1]
    out = _layer_norm(ALPHA * x_ref[...] + y, g_ref[...], b_ref[...])

    @pl.when(i < n_tiles_first)
    def _():
        op_ref[...] = out

    @pl.when(i >= n_tiles_first)
    def _():
        os_ref[...] = out


def _combine(slots, y_sorted, gates, x_all, g, b, n_rows_first):
    n_rows = x_all.shape[0]
    n_tiles = n_rows // COMBINE_TILE
    n_first = n_rows_first // COMBINE_TILE
    kern = functools.partial(_combine_kernel, n_tiles=n_tiles, n_tiles_first=n_first)
    return pl.pallas_call(
        kern,
        grid_spec=pltpu.PrefetchScalarGridSpec(
            num_scalar_prefetch=1,
            grid=(n_tiles,),
            in_specs=[
                pl.BlockSpec(memory_space=pl.ANY),
                pl.BlockSpec((COMBINE_TILE, 2), lambda i, sl: (i, 0)),
                pl.BlockSpec((COMBINE_TILE, D_MODEL), lambda i, sl: (i, 0)),
                pl.BlockSpec((1, D_MODEL), lambda i, sl: (0, 0)),
                pl.BlockSpec((1, D_MODEL), lambda i, sl: (0, 0)),
            ],
            out_specs=[
                pl.BlockSpec((COMBINE_TILE, D_MODEL), lambda i, sl: (jnp.minimum(i, n_first - 1), 0)),
                pl.BlockSpec((COMBINE_TILE, D_MODEL), lambda i, sl: (jnp.maximum(i - n_first, 0), 0)),
            ],
            scratch_shapes=[pltpu.VMEM((2, 2, COMBINE_TILE, D_MODEL), F32), pltpu.SemaphoreType.DMA((2,))],
        ),
        out_shape=[jax.ShapeDtypeStruct((n_rows_first, D_MODEL), F32),
                   jax.ShapeDtypeStruct((n_rows - n_rows_first, D_MODEL), F32)],
        compiler_params=pltpu.CompilerParams(dimension_semantics=("arbitrary",),
                                             vmem_limit_bytes=VMEM_LIMIT),
        name="combine",
    )(slots, y_sorted, gates, x_all, g, b)


def _dispatch_plan(meta, counts, n_tiles):
    n_tok = meta.shape[0]
    experts = meta[:, _META_E1:_META_E2 + 1].astype(jnp.int32)
    pos = meta[:, _META_POS1:_META_POS2 + 1].astype(jnp.int32)
    gates = meta[:, _META_GATE1:_META_GATE2 + 1]
    cnt = counts[0, _EXPERT_LANE0:_EXPERT_LANE0 + N_EXPERTS].astype(jnp.int32)
    tiles_per_expert = (cnt + MOE_TILE - 1) // MOE_TILE
    tile_end = jnp.cumsum(tiles_per_expert)
    row_start = (tile_end - tiles_per_expert) * MOE_TILE
    n_active = tile_end[-1]
    slots = (row_start[experts] + pos).reshape(-1)
    tile_ids = jnp.minimum(jnp.arange(n_tiles, dtype=jnp.int32), n_active - 1)
    tile_expert = jnp.sum((tile_end[None, :] <= tile_ids[:, None]).astype(jnp.int32), axis=1)
    tile_expert = jnp.minimum(tile_expert, N_EXPERTS - 1)
    tokens = jnp.repeat(jnp.arange(n_tok, dtype=jnp.int32), 2)
    src = jnp.zeros((n_tiles * MOE_TILE,), jnp.int32).at[slots].set(tokens)
    return tile_expert, n_active.reshape(1).astype(jnp.int32), src, slots.astype(jnp.int32), gates


def kernel(x_prompt, x_sample, mem_prompt, state_pool, state_ret, cache_mem_k, cache_mem_v, w_in, w_pool, b_pool,
           pool_scale, ret_gn_g, ret_gn_b, w_out, ln1_g, ln1_b, w_mq, w_mk, w_mv, w_mo, ln2_g, ln2_b, w_rg, w_re,
           w1, w3, w2, ln3_g, ln3_b):
    assert w_in.shape[0] == DEPTH == 1
    bp_n, tp, _ = x_prompt.shape
    bs_n, ts, _ = x_sample.shape
    rows_p, rows_s = bp_n * tp, bs_n * ts
    assert rows_s == ROW_TILE and rows_p % ROW_TILE == 0
    n_rows = rows_p + rows_s
    tiles_p = rows_p // ROW_TILE

    w_in_b = w_in[0].astype(BF16)
    w_out_b = w_out[0].astype(BF16)
    w_mq_b = w_mq[0].astype(BF16)
    w_mk_b = w_mk[0].astype(BF16)
    w_mv_b = w_mv[0].astype(BF16)
    w_mo_b = w_mo[0].astype(BF16)
    wp_b = w_pool[0].astype(BF16)
    bp = b_pool[0].reshape(1, POOL_WIDTH)
    ps = pool_scale[0].reshape(1, POOL_WIDTH)
    gng = ret_gn_g[0].reshape(1, RET_WIDTH)
    gnb = ret_gn_b[0].reshape(1, RET_WIDTH)
    row = lambda p: p[0].reshape(1, D_MODEL)

    xp2d = x_prompt.reshape(rows_p, D_MODEL)
    xs2d = x_sample.reshape(rows_s, D_MODEL)

    mem2d = mem_prompt.reshape(bp_n * N_MEM, D_MODEL)
    mk_p = _matmul(mem2d, w_mk_b, F32, "mem_k").reshape(bp_n, N_MEM, D_MODEL)
    mv_p = _matmul(mem2d, w_mv_b, F32, "mem_v").reshape(bp_n, N_MEM, D_MODEL)

    zeros_s = jnp.zeros((bp_n, RET_HEADS, RET_HEAD_DIM, RET_HEAD_DIM), F32)
    zeros_h = jnp.zeros((bp_n, POOL_HIST, POOL_WIDTH), F32)
    cat_p, ret_p, pool_p = _mixer(xp2d, w_in_b, wp_b, bp, ps, gng, gnb, zeros_s, zeros_h,
                                  n_streams=bp_n, t_len=tp, tile_rows=256, seg_len=256, chunk=256, pos0=0)
    cat_s, ret_s, pool_s = _mixer(xs2d, w_in_b, wp_b, bp, ps, gng, gnb, state_ret[0], state_pool[0],
                                  n_streams=bs_n, t_len=ts, tile_rows=rows_s, seg_len=ts, chunk=ts, pos0=PAST_LEN)
    x1 = _mm_ln([cat_p, cat_s], w_out_b, [xp2d, xs2d], row(ln1_g), row(ln1_b),
                n_rows=n_rows, n_tiles_first=tiles_p, name="out_ln1")

    q_all = _matmul(x1, w_mq_b, BF16, "mem_q")
    o_p = _attention(q_all, mk_p, mv_p, n_streams=bp_n, t_len=tp, q_rows=ROW_TILE, row_offset=0, name="attn_prompt")
    o_s = _attention(q_all, cache_mem_k[0].reshape(bs_n, N_MEM, D_MODEL), cache_mem_v[0].reshape(bs_n, N_MEM, D_MODEL),
                     n_streams=bs_n, t_len=ts, q_rows=ts, row_offset=rows_p, name="attn_sample")
    x2 = _mm_ln([o_p, o_s], w_mo_b, [x1], row(ln2_g), row(ln2_b),
                n_rows=n_rows, n_tiles_first=tiles_p, name="mo_ln2")

    w_router = jnp.concatenate([w_rg[0], w_re[0].reshape(D_MODEL, N_EXPERTS),
                                jnp.zeros((D_MODEL, LANES - N_GROUPS - N_EXPERTS), F32)], axis=1)
    meta, counts = _router(x2, w_router)
    n_tiles = (2 * n_rows) // MOE_TILE + N_EXPERTS
    tile_expert, n_active, src, slots, gates = _dispatch_plan(meta, counts, n_tiles)
    y_sorted = _experts(tile_expert, n_active, src, x2, w1[0].reshape(N_EXPERTS, D_MODEL, EXPERT_HIDDEN),
                        w3[0].reshape(N_EXPERTS, D_MODEL, EXPERT_HIDDEN),
                        w2[0].reshape(N_EXPERTS, EXPERT_HIDDEN, D_MODEL), n_tiles)
    y_p, y_s = _combine(slots, y_sorted, gates, x2, row(ln3_g), row(ln3_b), rows_p)

    kv_shape = (DEPTH, bp_n, N_MEM, MEM_HEADS, MEM_HEAD_DIM)
    return (y_p.reshape(bp_n, tp, D_MODEL), y_s.reshape(bs_n, ts, D_MODEL), pool_p[None], ret_p[None],
            mk_p.reshape(kv_shape), mv_p.reshape(kv_shape), pool_s[None], ret_s[None])
```

```python
import functools
import math

import jax
import jax.numpy as jnp
from jax import lax
from jax.experimental import pallas as pl
from jax.experimental.pallas import tpu as pltpu

F32 = jnp.float32
BF16 = jnp.bfloat16

D_MODEL = 2048
POOL_WIDTH = 1024
POOL_WINDOWS = (2, 4, 8, 16)
POOL_CH = 256
POOL_HIST = 15
RET_WIDTH = 1024
RET_HEADS = 8
RET_HEAD_DIM = 128
IN_WIDTH = POOL_WIDTH + 4 * RET_WIDTH
ROPE_BASE = 10000.0
N_MEM = 256
MEM_HEADS = 4
MEM_HEAD_DIM = 512
N_GROUPS = 4
EXPERTS_PER_GROUP = 8
N_EXPERTS = N_GROUPS * EXPERTS_PER_GROUP
EXPERT_HIDDEN = 512
LN_EPS = 1e-5
GN_EPS = 1e-6
DEPTH = 1
ALPHA = (2.0 * DEPTH) ** 0.25
PAST_LEN = 2048
PROMPT_CHUNK = 64

LANES = 128
HIST_PAD = 16
ROW_TILE = 512
MOE_TILE = 256
COMBINE_TILE = 256
VMEM_LIMIT = 58 * 1024 * 1024

_NEG_INF = float("-inf")


def _const_spec(shape):
    zeros = (0,) * len(shape)
    return pl.BlockSpec(shape, lambda *_: zeros, pipeline_mode=pl.Buffered(1))


def _when(cond, fn):
    if cond is True:
        fn()
    else:
        pl.when(cond)(fn)


def _layer_norm(z, g, b):
    mu = jnp.mean(z, axis=-1, keepdims=True)
    zc = z - mu
    var = jnp.mean(zc * zc, axis=-1, keepdims=True)
    return zc * lax.rsqrt(var + LN_EPS) * g + b


def _silu(a):
    return a * (1.0 / (1.0 + jnp.exp(-a)))


def _mixer_kernel(x_ref, w_in_ref, cos_ref, sin_ref, dec_ref, kd_ref, qd_ref, wp_ref, bp_ref, ps_ref,
                  gng_ref, gnb_ref, s0_ref, h0_ref, cat_ref, snew_ref, hnew_ref, h_ref, s_ref, u_ref,
                  *, seg_len, chunk, n_seg, tiles_per_stream, pos0, g_chunk):
    g = pl.program_id(0)
    j = pl.program_id(1)

    @pl.when(j == 0)
    def _project():
        h_ref[...] = jnp.dot(x_ref[...].astype(BF16), w_in_ref[...], preferred_element_type=F32)

    if n_seg == 1:
        r0 = 0
        first_of_stream = (g % tiles_per_stream) == 0
        last_of_stream = (g % tiles_per_stream) == tiles_per_stream - 1
        t_start = (g % tiles_per_stream) * seg_len
    else:
        r0 = pl.multiple_of(j * seg_len, seg_len)
        first_of_stream = True
        last_of_stream = True
        t_start = 0

    @functools.partial(_when, first_of_stream)
    def _init_state():
        s_ref[...] = s0_ref[0]
        u_ref[0:1, :] = jnp.zeros((1, POOL_WIDTH), F32)
        u_ref[1:HIST_PAD, :] = h0_ref[0]

    u_new = h_ref[pl.ds(r0, seg_len), 0:POOL_WIDTH]
    u_ref[HIST_PAD:HIST_PAD + seg_len, :] = u_new
    pos = (pos0 + t_start + lax.broadcasted_iota(jnp.int32, (seg_len, 1), 0)).astype(F32)
    for gi, w in enumerate(POOL_WINDOWS):
        cols = slice(gi * POOL_CH, (gi + 1) * POOL_CH)
        win = u_ref[HIST_PAD:HIST_PAD + seg_len, cols]
        for back in range(1, w):
            win = win + u_ref[HIST_PAD - back:HIST_PAD - back + seg_len, cols]
        cnt = jnp.minimum(float(w), pos + 1.0)
        d = win * (1.0 / cnt) - u_ref[HIST_PAD:HIST_PAD + seg_len, cols]
        pooled = jnp.dot(d.astype(BF16), wp_ref[gi], preferred_element_type=F32) + bp_ref[:, cols]
        cat_ref[:, cols] = (pooled * ps_ref[:, cols]).astype(BF16)

    @functools.partial(_when, last_of_stream)
    def _emit_hist():
        hnew_ref[0] = u_ref[seg_len + 1:seg_len + HIST_PAD, :]

    u_ref[0:HIST_PAD, :] = u_ref[seg_len:seg_len + HIST_PAD, :]

    scale = RET_HEAD_DIM ** -0.5
    for c in range(seg_len // chunk):
        rows = pl.ds(r0 + c * chunk, chunk)
        trows = slice(c * chunk, (c + 1) * chunk)
        cos_t = cos_ref[trows, :]
        sin_t = sin_ref[trows, :]
        for hd in range(RET_HEADS):
            lo = hd * RET_HEAD_DIM
            hcols = slice(lo, lo + RET_HEAD_DIM)
            q = h_ref[rows, POOL_WIDTH + lo:POOL_WIDTH + lo + RET_HEAD_DIM]
            k = h_ref[rows, POOL_WIDTH + RET_WIDTH + lo:POOL_WIDTH + RET_WIDTH + lo + RET_HEAD_DIM]
            v = h_ref[rows, POOL_WIDTH + 2 * RET_WIDTH + lo:POOL_WIDTH + 2 * RET_WIDTH + lo + RET_HEAD_DIM]
            gate = h_ref[rows, POOL_WIDTH + 3 * RET_WIDTH + lo:POOL_WIDTH + 3 * RET_WIDTH + lo + RET_HEAD_DIM]
            qr = q * cos_t + pltpu.roll(q, RET_HEAD_DIM // 2, 1) * sin_t
            kr = (k * cos_t + pltpu.roll(k, RET_HEAD_DIM // 2, 1) * sin_t) * scale
            vb = v.astype(BF16)
            scores = lax.dot_general(qr.astype(BF16), kr.astype(BF16), (((1,), (1,)), ((), ())),
                                     preferred_element_type=F32) * dec_ref[hd]
            o = jnp.dot(scores.astype(BF16), vb, preferred_element_type=F32)
            s_prev = s_ref[hd]
            o = o + jnp.dot((qr * qd_ref[hd]).astype(BF16), s_prev.astype(BF16), preferred_element_type=F32)
            upd = lax.dot_general((kr * kd_ref[hd]).astype(BF16), vb, (((0,), (0,)), ((), ())),
                                  preferred_element_type=F32)
            s_ref[hd] = g_chunk[hd] * s_prev + upd
            mu = jnp.mean(o, axis=-1, keepdims=True)
            oc = o - mu
            var = jnp.mean(oc * oc, axis=-1, keepdims=True)
            on = oc * lax.rsqrt(var + GN_EPS) * gng_ref[:, hcols] + gnb_ref[:, hcols]
            cat_ref[trows, POOL_WIDTH + lo:POOL_WIDTH + lo + RET_HEAD_DIM] = (on * _silu(gate)).astype(BF16)

    @functools.partial(_when, last_of_stream)
    def _emit_state():
        snew_ref[0] = s_ref[...]


def _retention_tables(chunk, t_len, pos0):
    log_gamma = jnp.log(1.0 - 2.0 ** (-5.0 - jnp.arange(RET_HEADS, dtype=F32)))
    idx = jnp.arange(chunk, dtype=F32)
    diff = idx[:, None] - idx[None, :]
    dec = jnp.where(diff >= 0, jnp.exp(log_gamma[:, None, None] * jnp.maximum(diff, 0.0)), 0.0)
    kd = jnp.exp(log_gamma[:, None] * (chunk - 1.0 - idx)[None, :])
    qd = jnp.exp(log_gamma[:, None] * (idx + 1.0)[None, :])
    kd = jnp.broadcast_to(kd[:, :, None], (RET_HEADS, chunk, RET_HEAD_DIM))
    qd = jnp.broadcast_to(qd[:, :, None], (RET_HEADS, chunk, RET_HEAD_DIM))
    half = RET_HEAD_DIM // 2
    freqs = ROPE_BASE ** (-jnp.arange(half, dtype=F32) / half)
    pos = pos0 + jnp.arange(t_len, dtype=F32)
    ang = pos[:, None] * freqs[None, :]
    cos = jnp.cos(ang)
    sin = jnp.sin(ang)
    cos_t = jnp.concatenate([cos, cos], axis=-1)
    sin_t = jnp.concatenate([-sin, sin], axis=-1)
    g_chunk = tuple(math.exp(math.log(1.0 - 2.0 ** (-5.0 - h)) * chunk) for h in range(RET_HEADS))
    return dec, kd, qd, cos_t, sin_t, g_chunk


def _mixer(x2d, w_in_b, wp_b, bp, ps, gng, gnb, s0, h0, *, n_streams, t_len, tile_rows, seg_len, chunk, pos0):
    rows = n_streams * t_len
    n_tiles = rows // tile_rows
    n_seg = tile_rows // seg_len
    tiles_per_stream = max(t_len // tile_rows, 1)
    assert n_seg == 1 or (seg_len == t_len and n_tiles == 1)
    assert seg_len % chunk == 0 and t_len >= POOL_HIST
    dec, kd, qd, cos_t, sin_t, g_chunk = _retention_tables(chunk, t_len, pos0)

    if n_seg == 1:
        stream_of = lambda g, j: g // tiles_per_stream
        time_of = lambda g, j: g % tiles_per_stream
    else:
        stream_of = lambda g, j: j
        time_of = lambda g, j: 0

    kern = functools.partial(_mixer_kernel, seg_len=seg_len, chunk=chunk, n_seg=n_seg,
                             tiles_per_stream=tiles_per_stream, pos0=pos0, g_chunk=g_chunk)
    return pl.pallas_call(
        kern,
        grid=(n_tiles, n_seg),
        in_specs=[
            pl.BlockSpec((tile_rows, D_MODEL), lambda g, j: (g, 0)),
            _const_spec((D_MODEL, IN_WIDTH)),
            pl.BlockSpec((seg_len, RET_HEAD_DIM), lambda g, j: (time_of(g, j), 0)),
            pl.BlockSpec((seg_len, RET_HEAD_DIM), lambda g, j: (time_of(g, j), 0)),
            _const_spec((RET_HEADS, chunk, chunk)),
            _const_spec((RET_HEADS, chunk, RET_HEAD_DIM)),
            _const_spec((RET_HEADS, chunk, RET_HEAD_DIM)),
            _const_spec((len(POOL_WINDOWS), POOL_CH, POOL_CH)),
            _const_spec((1, POOL_WIDTH)),
            _const_spec((1, POOL_WIDTH)),
            _const_spec((1, RET_WIDTH)),
            _const_spec((1, RET_WIDTH)),
            pl.BlockSpec((1, RET_HEADS, RET_HEAD_DIM, RET_HEAD_DIM), lambda g, j: (stream_of(g, j), 0, 0, 0)),
            pl.BlockSpec((1, POOL_HIST, POOL_WIDTH), lambda g, j: (stream_of(g, j), 0, 0)),
        ],
        out_specs=[
            pl.BlockSpec((seg_len, D_MODEL), lambda g, j: (g * n_seg + j, 0)),
            pl.BlockSpec((1, RET_HEADS, RET_HEAD_DIM, RET_HEAD_DIM), lambda g, j: (stream_of(g, j), 0, 0, 0)),
            pl.BlockSpec((1, POOL_HIST, POOL_WIDTH), lambda g, j: (stream_of(g, j), 0, 0)),
        ],
        out_shape=[
            jax.ShapeDtypeStruct((rows, D_MODEL), BF16),
            jax.ShapeDtypeStruct((n_streams, RET_HEADS, RET_HEAD_DIM, RET_HEAD_DIM), F32),
            jax.ShapeDtypeStruct((n_streams, POOL_HIST, POOL_WIDTH), F32),
        ],
        scratch_shapes=[
            pltpu.VMEM((tile_rows, IN_WIDTH), F32),
            pltpu.VMEM((RET_HEADS, RET_HEAD_DIM, RET_HEAD_DIM), F32),
            pltpu.VMEM((HIST_PAD + seg_len, POOL_WIDTH), F32),
        ],
        compiler_params=pltpu.CompilerParams(dimension_semantics=("arbitrary", "arbitrary"),
                                             vmem_limit_bytes=VMEM_LIMIT),
        name="mixer",
    )(x2d, w_in_b, cos_t, sin_t, dec, kd, qd, wp_b, bp, ps, gng, gnb, s0, h0)


def _row_sources(arrays, n_tiles_first):
    if len(arrays) == 1:
        return [pl.BlockSpec((ROW_TILE, arrays[0].shape[1]), lambda i: (i, 0))]
    first, second = arrays
    return [
        pl.BlockSpec((ROW_TILE, first.shape[1]), lambda i: (jnp.minimum(i, n_tiles_first - 1), 0)),
        pl.BlockSpec((ROW_TILE, second.shape[1]), lambda i: (jnp.maximum(i - n_tiles_first, 0), 0)),
    ]


def _mm_ln_kernel(*refs, n_a, n_res, n_tiles_first):
    a_refs = refs[:n_a]
    w_ref = refs[n_a]
    res_refs = refs[n_a + 1:n_a + 1 + n_res]
    g_ref, b_ref, o_ref = refs[n_a + 1 + n_res:]

    def body(a_ref, res_ref):
        acc = jnp.dot(a_ref[...].astype(BF16), w_ref[...], preferred_element_type=F32)
        o_ref[...] = _layer_norm(ALPHA * res_ref[...] + acc, g_ref[...], b_ref[...])

    i = pl.program_id(0)
    pl.when(i < n_tiles_first)(lambda: body(a_refs[0], res_refs[0]))
    pl.when(i >= n_tiles_first)(lambda: body(a_refs[-1], res_refs[-1]))


def _mm_ln(a_arrays, w_b, res_arrays, g, b, *, n_rows, n_tiles_first, name):
    n_tiles = n_rows // ROW_TILE
    kern = functools.partial(_mm_ln_kernel, n_a=len(a_arrays), n_res=len(res_arrays),
                             n_tiles_first=n_tiles_first)
    return pl.pallas_call(
        kern,
        grid=(n_tiles,),
        in_specs=(_row_sources(a_arrays, n_tiles_first) + [_const_spec(w_b.shape)]
                  + _row_sources(res_arrays, n_tiles_first)
                  + [_const_spec((1, D_MODEL)), _const_spec((1, D_MODEL))]),
        out_specs=pl.BlockSpec((ROW_TILE, D_MODEL), lambda i: (i, 0)),
        out_shape=jax.ShapeDtypeStruct((n_rows, D_MODEL), F32),
        compiler_params=pltpu.CompilerParams(dimension_semantics=("arbitrary",),
                                             vmem_limit_bytes=VMEM_LIMIT),
        name=name,
    )(*a_arrays, w_b, *res_arrays, g, b)


def _matmul_kernel(a_ref, w_ref, o_ref):
    o_ref[...] = jnp.dot(a_ref[...].astype(BF16), w_ref[...], preferred_element_type=F32).astype(o_ref.dtype)


def _matmul(a, w_b, out_dtype, name):
    n_rows = a.shape[0]
    return pl.pallas_call(
        _matmul_kernel,
        grid=(n_rows // ROW_TILE,),
        in_specs=[pl.BlockSpec((ROW_TILE, a.shape[1]), lambda i: (i, 0)), _const_spec(w_b.shape)],
        out_specs=pl.BlockSpec((ROW_TILE, w_b.shape[1]), lambda i: (i, 0)),
        out_shape=jax.ShapeDtypeStruct((n_rows, w_b.shape[1]), out_dtype),
        compiler_params=pltpu.CompilerParams(dimension_semantics=("arbitrary",),
                                             vmem_limit_bytes=VMEM_LIMIT),
        name=name,
    )(a, w_b)


def _attention_kernel(q_ref, k_ref, v_ref, o_ref):
    scale = MEM_HEAD_DIM ** -0.5
    for h in range(MEM_HEADS):
        cols = slice(h * MEM_HEAD_DIM, (h + 1) * MEM_HEAD_DIM)
        s = lax.dot_general(q_ref[:, cols], k_ref[0, :, cols].astype(BF16), (((1,), (1,)), ((), ())),
                            preferred_element_type=F32) * scale
        m = jnp.max(s, axis=-1, keepdims=True)
        p = jnp.exp(s - m)
        p = p * (1.0 / jnp.sum(p, axis=-1, keepdims=True))
        o_ref[:, cols] = jnp.dot(p.astype(BF16), v_ref[0, :, cols].astype(BF16),
                                 preferred_element_type=F32).astype(o_ref.dtype)


def _attention(q_all, mem_k, mem_v, *, n_streams, t_len, q_rows, row_offset, name):
    tiles_per_stream = t_len // q_rows
    base = row_offset // q_rows
    return pl.pallas_call(
        _attention_kernel,
        grid=(n_streams, tiles_per_stream),
        in_specs=[
            pl.BlockSpec((q_rows, D_MODEL), lambda b, t: (base + b * tiles_per_stream + t, 0)),
            pl.BlockSpec((1, N_MEM, D_MODEL), lambda b, t: (b, 0, 0)),
            pl.BlockSpec((1, N_MEM, D_MODEL), lambda b, t: (b, 0, 0)),
        ],
        out_specs=pl.BlockSpec((q_rows, D_MODEL), lambda b, t: (b * tiles_per_stream + t, 0)),
        out_shape=jax.ShapeDtypeStruct((n_streams * t_len, D_MODEL), BF16),
        compiler_params=pltpu.CompilerParams(dimension_semantics=("arbitrary", "arbitrary"),
                                             vmem_limit_bytes=VMEM_LIMIT),
        name=name,
    )(q_all, mem_k, mem_v)


_GROUP_LANE0 = 0
_EXPERT_LANE0 = N_GROUPS
(_META_LANE1, _META_LANE2, _META_POS1, _META_POS2, _META_GATE1, _META_GATE2) = range(6)
(_OUT_SLOT1, _OUT_SLOT2, _OUT_GATE1, _OUT_GATE2) = range(4)
_MOE_TILE_LOG2 = MOE_TILE.bit_length() - 1
assert 1 << _MOE_TILE_LOG2 == MOE_TILE


def _router_kernel(x_ref, wr_ref, out_ref, counts_ref, carry_ref, meta_ref, start_ref):
    phase = pl.program_id(0)
    i = pl.program_id(1)
    rows = x_ref.shape[0]
    tile_rows = pl.ds(pl.multiple_of(i * rows, rows), rows)
    lane = lax.broadcasted_iota(jnp.int32, (rows, LANES), 1)

    @pl.when((phase == 0) & (i == 0))
    def _():
        carry_ref[...] = jnp.zeros_like(carry_ref)

    @pl.when(phase == 0)
    def _route():
        logits = jnp.dot(x_ref[...], wr_ref[...], preferred_element_type=F32, precision=lax.Precision.HIGHEST)

        def first_argmax(vals):
            m = jnp.max(vals, axis=-1, keepdims=True)
            idx = jnp.min(jnp.where(vals == m, lane, LANES), axis=-1, keepdims=True)
            return m, idx

        gl = jnp.where(lane < N_GROUPS, logits, _NEG_INF)
        gm, g_idx = first_argmax(gl)
        g_w = 1.0 / jnp.sum(jnp.exp(gl - gm), axis=-1, keepdims=True)

        in_group = ((lane >= _EXPERT_LANE0) & (lane < _EXPERT_LANE0 + N_EXPERTS)
                    & (((lane - _EXPERT_LANE0) >> 3) == g_idx))
        el = jnp.where(in_group, logits, _NEG_INF)
        m1, i1 = first_argmax(el)
        z = jnp.sum(jnp.exp(el - m1), axis=-1, keepdims=True)
        m2, i2 = first_argmax(jnp.where(lane == i1, _NEG_INF, el))
        p1 = 1.0 / z
        p2 = jnp.exp(m2 - m1) / z
        den = p1 + p2
        gate1 = p1 / den * g_w
        gate2 = p2 / den * g_w

        hit1 = lane == i1
        hit2 = lane == i2
        onehot = (hit1 | hit2).astype(BF16)
        earlier = (lax.broadcasted_iota(jnp.int32, (rows, rows), 1)
                   < lax.broadcasted_iota(jnp.int32, (rows, rows), 0)).astype(BF16)
        rank = jnp.dot(earlier, onehot, preferred_element_type=F32) + carry_ref[...]
        pos1 = jnp.sum(jnp.where(hit1, rank, 0.0), axis=-1, keepdims=True)
        pos2 = jnp.sum(jnp.where(hit2, rank, 0.0), axis=-1, keepdims=True)
        carry_ref[...] = carry_ref[...] + jnp.sum(onehot.astype(F32), axis=0, keepdims=True)

        meta = jnp.zeros((rows, LANES), F32)
        for col, val in ((_META_LANE1, i1.astype(F32)), (_META_LANE2, i2.astype(F32)), (_META_POS1, pos1),
                         (_META_POS2, pos2), (_META_GATE1, gate1), (_META_GATE2, gate2)):
            meta = jnp.where(lane == col, val, meta)
        meta_ref[tile_rows, :] = meta

    @pl.when((phase == 1) & (i == 0))
    def _segment_starts():
        counts_ref[...] = carry_ref[...]
        tiles = ((carry_ref[...].astype(jnp.int32) + (MOE_TILE - 1)) >> _MOE_TILE_LOG2).astype(F32).astype(BF16)
        before = (lax.broadcasted_iota(jnp.int32, (LANES, LANES), 0)
                  < lax.broadcasted_iota(jnp.int32, (LANES, LANES), 1)).astype(BF16)
        tiles8 = jnp.broadcast_to(tiles, (8, LANES))
        start_ref[...] = jnp.dot(tiles8, before, preferred_element_type=F32)[0:1, :] * float(MOE_TILE)

    @pl.when(phase == 1)
    def _slots():
        meta = meta_ref[tile_rows, :]
        starts = start_ref[...]

        def col(c):
            return jnp.sum(jnp.where(lane == c, meta, 0.0), axis=-1, keepdims=True)

        def start_of(expert_lane):
            return jnp.sum(jnp.where(lane == expert_lane.astype(jnp.int32), starts, 0.0), axis=-1, keepdims=True)

        slot1 = start_of(col(_META_LANE1)) + col(_META_POS1)
        slot2 = start_of(col(_META_LANE2)) + col(_META_POS2)
        out = jnp.zeros((rows, LANES), F32)
        for c, val in ((_OUT_SLOT1, slot1), (_OUT_SLOT2, slot2), (_OUT_GATE1, col(_META_GATE1)),
                       (_OUT_GATE2, col(_META_GATE2))):
            out = jnp.where(lane == c, val, out)
        out_ref[...] = out


def _router(x_all, w_router):
    n_rows = x_all.shape[0]
    n_tiles = n_rows // ROW_TILE
    return pl.pallas_call(
        _router_kernel,
        grid=(2, n_tiles),
        in_specs=[pl.BlockSpec((ROW_TILE, D_MODEL), lambda p, i: (i * (1 - p) + (n_tiles - 1) * p, 0)),
                  _const_spec((D_MODEL, LANES))],
        out_specs=[pl.BlockSpec((ROW_TILE, LANES), lambda p, i: (i * p, 0)),
                   pl.BlockSpec((1, LANES), lambda p, i: (0, 0))],
        out_shape=[jax.ShapeDtypeStruct((n_rows, LANES), F32), jax.ShapeDtypeStruct((1, LANES), F32)],
        scratch_shapes=[pltpu.VMEM((1, LANES), F32), pltpu.VMEM((n_rows, LANES), F32), pltpu.VMEM((1, LANES), F32)],
        compiler_params=pltpu.CompilerParams(dimension_semantics=("arbitrary", "arbitrary"),
                                             vmem_limit_bytes=VMEM_LIMIT),
        name="router",
    )(x_all, w_router)


def _row_copy(src_hbm, row, dst_buf, slot, r, sem):
    return pltpu.make_async_copy(src_hbm.at[pl.ds(row, 1), :], dst_buf.at[slot, pl.ds(r, 1), :], sem.at[slot])


def _tile_copy(src_hbm, dst_buf, slot, sem):
    return pltpu.make_async_copy(src_hbm.at[pl.ds(0, dst_buf.shape[1]), :], dst_buf.at[slot], sem.at[slot])


def _experts_kernel(tile_expert_ref, n_active_ref, slots_ref, x_hbm, w1_ref, w3_ref, w2_ref, y_ref,
                    xbuf, src_ref, sem):
    del tile_expert_ref
    i = pl.program_id(0)
    n_active = n_active_ref[0]
    slot = i % 2

    def gather(tile, buf):
        base = tile * MOE_TILE
        for r in range(MOE_TILE):
            _row_copy(x_hbm, src_ref[base + r], xbuf, buf, r, sem).start()

    def compute():
        xb = xbuf[slot].astype(BF16)
        a = jnp.dot(xb, w1_ref[0, 0, 0].astype(BF16), preferred_element_type=F32)
        b = jnp.dot(xb, w3_ref[0, 0, 0].astype(BF16), preferred_element_type=F32)
        hdn = (_silu(a) * b).astype(BF16)
        y_ref[...] = jnp.dot(hdn, w2_ref[0, 0, 0].astype(BF16), preferred_element_type=F32)

    @pl.when(i == 0)
    def _plan():
        def clear(p, carry):
            src_ref[p] = 0
            return carry
        lax.fori_loop(0, src_ref.shape[0], clear, 0, unroll=16)

        def put(n, carry):
            src_ref[slots_ref[n]] = n >> 1
            return carry
        lax.fori_loop(0, slots_ref.shape[0], put, 0, unroll=16)
        gather(0, 0)

    @pl.when(i + 1 < n_active)
    def _steady():
        _tile_copy(x_hbm, xbuf, slot, sem).wait()
        gather(i + 1, 1 - slot)
        compute()

    @pl.when(i + 1 == n_active)
    def _last():
        _tile_copy(x_hbm, xbuf, slot, sem).wait()
        compute()

    @pl.when(i >= n_active)
    def _():
        y_ref[...] = jnp.zeros_like(y_ref)


def _experts(tile_expert, n_active, slots, x_all, w1, w3, w2, n_tiles):
    def expert_block(i, te, na, sl):
        return (0, te[i] // EXPERTS_PER_GROUP, te[i] % EXPERTS_PER_GROUP, 0, 0)

    return pl.pallas_call(
        _experts_kernel,
        grid_spec=pltpu.PrefetchScalarGridSpec(
            num_scalar_prefetch=3,
            grid=(n_tiles,),
            in_specs=[
                pl.BlockSpec(memory_space=pl.ANY),
                pl.BlockSpec((1, 1, 1, D_MODEL, EXPERT_HIDDEN), expert_block),
                pl.BlockSpec((1, 1, 1, D_MODEL, EXPERT_HIDDEN), expert_block),
                pl.BlockSpec((1, 1, 1, EXPERT_HIDDEN, D_MODEL), expert_block),
            ],
            out_specs=pl.BlockSpec((MOE_TILE, D_MODEL), lambda i, te, na, sl: (i, 0)),
            scratch_shapes=[pltpu.VMEM((2, MOE_TILE, D_MODEL), F32), pltpu.SMEM((n_tiles * MOE_TILE,), jnp.int32),
                            pltpu.SemaphoreType.DMA((2,))],
        ),
        out_shape=jax.ShapeDtypeStruct((n_tiles * MOE_TILE, D_MODEL), F32),
        compiler_params=pltpu.CompilerParams(dimension_semantics=("arbitrary",),
                                             vmem_limit_bytes=VMEM_LIMIT),
        name="experts",
    )(tile_expert, n_active, slots, x_all, w1, w3, w2)


def _combine_kernel(slot_ref, y_hbm, gates_ref, x_ref, g_ref, b_ref, op_ref, os_ref, ybuf, sem,
                    *, n_tiles, n_tiles_first):
    i = pl.program_id(0)

    def copies(tile, buf, r):
        tok = tile * COMBINE_TILE + r
        return (
            pltpu.make_async_copy(y_hbm.at[pl.ds(slot_ref[2 * tok], 1), :],
                                  ybuf.at[buf, 0, pl.ds(r, 1), :], sem.at[buf]),
            pltpu.make_async_copy(y_hbm.at[pl.ds(slot_ref[2 * tok + 1], 1), :],
                                  ybuf.at[buf, 1, pl.ds(r, 1), :], sem.at[buf]),
        )

    def gather(tile, buf):
        for r in range(COMBINE_TILE):
            c0, c1 = copies(tile, buf, r)
            c0.start()
            c1.start()

    buf = i % 2

    def wait_all():
        for k in range(2):
            pltpu.make_async_copy(y_hbm.at[pl.ds(0, COMBINE_TILE), :], ybuf.at[buf, k], sem.at[buf]).wait()

    def compute():
        y = gates_ref[:, 0:1] * ybuf[buf, 0] + gates_ref[:, 1:2] * ybuf[buf, 1]
        out = _layer_norm(ALPHA * x_ref[...] + y, g_ref[...], b_ref[...])

        @pl.when(i < n_tiles_first)
        def _():
            op_ref[...] = out

        @pl.when(i >= n_tiles_first)
        def _():
            os_ref[...] = out

    @pl.when(i == 0)
    def _():
        gather(0, 0)

    @pl.when(i + 1 < n_tiles)
    def _steady():
        wait_all()
        gather(i + 1, 1 - buf)
        compute()

    @pl.when(i + 1 == n_tiles)
    def _last():
        wait_all()
        compute()


def _combine(slots, y_sorted, gates, x_all, g, b, n_rows_first):
    n_rows = x_all.shape[0]
    n_tiles = n_rows // COMBINE_TILE
    n_first = n_rows_first // COMBINE_TILE
    kern = functools.partial(_combine_kernel, n_tiles=n_tiles, n_tiles_first=n_first)
    return pl.pallas_call(
        kern,
        grid_spec=pltpu.PrefetchScalarGridSpec(
            num_scalar_prefetch=1,
            grid=(n_tiles,),
            in_specs=[
                pl.BlockSpec(memory_space=pl.ANY),
                pl.BlockSpec((COMBINE_TILE, 2), lambda i, sl: (i, 0)),
                pl.BlockSpec((COMBINE_TILE, D_MODEL), lambda i, sl: (i, 0)),
                pl.BlockSpec((1, D_MODEL), lambda i, sl: (0, 0)),
                pl.BlockSpec((1, D_MODEL), lambda i, sl: (0, 0)),
            ],
            out_specs=[
                pl.BlockSpec((COMBINE_TILE, D_MODEL), lambda i, sl: (jnp.minimum(i, n_first - 1), 0)),
                pl.BlockSpec((COMBINE_TILE, D_MODEL), lambda i, sl: (jnp.maximum(i - n_first, 0), 0)),
            ],
            scratch_shapes=[pltpu.VMEM((2, 2, COMBINE_TILE, D_MODEL), F32), pltpu.SemaphoreType.DMA((2,))],
        ),
        out_shape=[jax.ShapeDtypeStruct((n_rows_first, D_MODEL), F32),
                   jax.ShapeDtypeStruct((n_rows - n_rows_first, D_MODEL), F32)],
        compiler_params=pltpu.CompilerParams(dimension_semantics=("arbitrary",),
                                             vmem_limit_bytes=VMEM_LIMIT),
        name="combine",
    )(slots, y_sorted, gates, x_all, g, b)


def _dispatch_plan(routed, counts, n_tiles):
    slots = routed[:, _OUT_SLOT1:_OUT_SLOT2 + 1].astype(jnp.int32).reshape(-1)
    gates = routed[:, _OUT_GATE1:_OUT_GATE2 + 1]
    cnt = counts[0, _EXPERT_LANE0:_EXPERT_LANE0 + N_EXPERTS].astype(jnp.int32)
    tile_end = jnp.cumsum((cnt + MOE_TILE - 1) // MOE_TILE)
    n_active = tile_end[-1]
    tile_ids = jnp.minimum(jnp.arange(n_tiles, dtype=jnp.int32), n_active - 1)
    tile_expert = jnp.sum((tile_end[None, :] <= tile_ids[:, None]).astype(jnp.int32), axis=1)
    tile_expert = jnp.minimum(tile_expert, N_EXPERTS - 1)
    return tile_expert, n_active.reshape(1).astype(jnp.int32), slots, gates


def kernel(x_prompt, x_sample, mem_prompt, state_pool, state_ret, cache_mem_k, cache_mem_v, w_in, w_pool, b_pool,
           pool_scale, ret_gn_g, ret_gn_b, w_out, ln1_g, ln1_b, w_mq, w_mk, w_mv, w_mo, ln2_g, ln2_b, w_rg, w_re,
           w1, w3, w2, ln3_g, ln3_b):
    assert w_in.shape[0] == DEPTH == 1
    bp_n, tp, _ = x_prompt.shape
    bs_n, ts, _ = x_sample.shape
    rows_p, rows_s = bp_n * tp, bs_n * ts
    assert rows_s == ROW_TILE and rows_p % ROW_TILE == 0
    n_rows = rows_p + rows_s
    tiles_p = rows_p // ROW_TILE

    w_in_b = w_in[0].astype(BF16)
    w_out_b = w_out[0].astype(BF16)
    w_mq_b = w_mq[0].astype(BF16)
    w_mk_b = w_mk[0].astype(BF16)
    w_mv_b = w_mv[0].astype(BF16)
    w_mo_b = w_mo[0].astype(BF16)
    wp_b = w_pool[0].astype(BF16)
    bp = b_pool[0].reshape(1, POOL_WIDTH)
    ps = pool_scale[0].reshape(1, POOL_WIDTH)
    gng = ret_gn_g[0].reshape(1, RET_WIDTH)
    gnb = ret_gn_b[0].reshape(1, RET_WIDTH)
    row = lambda p: p[0].reshape(1, D_MODEL)

    xp2d = x_prompt.reshape(rows_p, D_MODEL)
    xs2d = x_sample.reshape(rows_s, D_MODEL)

    mem2d = mem_prompt.reshape(bp_n * N_MEM, D_MODEL)
    mk_p = _matmul(mem2d, w_mk_b, F32, "mem_k").reshape(bp_n, N_MEM, D_MODEL)
    mv_p = _matmul(mem2d, w_mv_b, F32, "mem_v").reshape(bp_n, N_MEM, D_MODEL)

    zeros_s = jnp.zeros((bp_n, RET_HEADS, RET_HEAD_DIM, RET_HEAD_DIM), F32)
    zeros_h = jnp.zeros((bp_n, POOL_HIST, POOL_WIDTH), F32)
    cat_p, ret_p, pool_p = _mixer(xp2d, w_in_b, wp_b, bp, ps, gng, gnb, zeros_s, zeros_h,
                                  n_streams=bp_n, t_len=tp, tile_rows=256, seg_len=256, chunk=256, pos0=0)
    cat_s, ret_s, pool_s = _mixer(xs2d, w_in_b, wp_b, bp, ps, gng, gnb, state_ret[0], state_pool[0],
                                  n_streams=bs_n, t_len=ts, tile_rows=rows_s, seg_len=ts, chunk=ts, pos0=PAST_LEN)
    x1 = _mm_ln([cat_p, cat_s], w_out_b, [xp2d, xs2d], row(ln1_g), row(ln1_b),
                n_rows=n_rows, n_tiles_first=tiles_p, name="out_ln1")

    q_all = _matmul(x1, w_mq_b, BF16, "mem_q")
    o_p = _attention(q_all, mk_p, mv_p, n_streams=bp_n, t_len=tp, q_rows=ROW_TILE, row_offset=0, name="attn_prompt")
    o_s = _attention(q_all, cache_mem_k[0].reshape(bs_n, N_MEM, D_MODEL), cache_mem_v[0].reshape(bs_n, N_MEM, D_MODEL),
                     n_streams=bs_n, t_len=ts, q_rows=ts, row_offset=rows_p, name="attn_sample")
    x2 = _mm_ln([o_p, o_s], w_mo_b, [x1], row(ln2_g), row(ln2_b),
                n_rows=n_rows, n_tiles_first=tiles_p, name="mo_ln2")

    w_router = jnp.concatenate([w_rg[0], w_re[0].reshape(D_MODEL, N_EXPERTS),
                                jnp.zeros((D_MODEL, LANES - N_GROUPS - N_EXPERTS), F32)], axis=1)
    routed, counts = _router(x2, w_router)
    n_tiles = (2 * n_rows) // MOE_TILE + N_EXPERTS
    tile_expert, n_active, slots, gates = _dispatch_plan(routed, counts, n_tiles)
    y_sorted = _experts(tile_expert, n_active, slots, x2, w1, w3, w2, n_tiles)
    y_p, y_s = _combine(slots, y_sorted, gates, x2, row(ln3_g), row(ln3_b), rows_p)

    kv_shape = (DEPTH, bp_n, N_MEM, MEM_HEADS, MEM_HEAD_DIM)
    return (y_p.reshape(bp_n, tp, D_MODEL), y_s.reshape(bs_n, ts, D_MODEL), pool_p[None], ret_p[None],
            mk_p.reshape(kv_shape), mv_p.reshape(kv_shape), pool_s[None], ret_s[None])
```

```python
import functools
import math

import jax
import jax.numpy as jnp
from jax import lax
from jax.experimental import pallas as pl
from jax.experimental.pallas import tpu as pltpu

F32 = jnp.float32
BF16 = jnp.bfloat16

D_MODEL = 2048
POOL_WIDTH = 1024
POOL_WINDOWS = (2, 4, 8, 16)
POOL_CH = 256
POOL_HIST = 15
RET_WIDTH = 1024
RET_HEADS = 8
RET_HEAD_DIM = 128
IN_WIDTH = POOL_WIDTH + 4 * RET_WIDTH
ROPE_BASE = 10000.0
N_MEM = 256
MEM_HEADS = 4
MEM_HEAD_DIM = 512
N_GROUPS = 4
EXPERTS_PER_GROUP = 8
N_EXPERTS = N_GROUPS * EXPERTS_PER_GROUP
EXPERT_HIDDEN = 512
LN_EPS = 1e-5
GN_EPS = 1e-6
DEPTH = 1
ALPHA = (2.0 * DEPTH) ** 0.25
PAST_LEN = 2048
PROMPT_CHUNK = 64

LANES = 128
HIST_PAD = 16
ROW_TILE = 512
MOE_TILE = 256
COMBINE_TILE = 256
VMEM_LIMIT = 58 * 1024 * 1024

_NEG_INF = float("-inf")


def _const_spec(shape):
    zeros = (0,) * len(shape)
    return pl.BlockSpec(shape, lambda *_: zeros, pipeline_mode=pl.Buffered(1))


def _when(cond, fn):
    if cond is True:
        fn()
    else:
        pl.when(cond)(fn)


def _layer_norm(z, g, b):
    mu = jnp.mean(z, axis=-1, keepdims=True)
    zc = z - mu
    var = jnp.mean(zc * zc, axis=-1, keepdims=True)
    return zc * lax.rsqrt(var + LN_EPS) * g + b


def _silu(a):
    return a * (1.0 / (1.0 + jnp.exp(-a)))


TOKEN_TILE_ROWS = D_MODEL // LANES


def _to_token_tiles(x):
    rows = x.shape[0]
    return x.astype(BF16).reshape(rows, TOKEN_TILE_ROWS, LANES).reshape(rows * TOKEN_TILE_ROWS, LANES)


def _from_token_tiles(tiles):
    rows = tiles.shape[0] // TOKEN_TILE_ROWS
    return tiles.reshape(rows, TOKEN_TILE_ROWS, LANES).reshape(rows, D_MODEL)


def _mixer_kernel(x_ref, w_in_ref, cos_ref, sin_ref, dec_ref, kd_ref, qd_ref, wp_ref, bp_ref, ps_ref,
                  gng_ref, gnb_ref, s0_ref, h0_ref, cat_ref, snew_ref, hnew_ref, h_ref, s_ref, u_ref,
                  *, seg_len, chunk, n_seg, tiles_per_stream, pos0, g_chunk):
    g = pl.program_id(0)
    j = pl.program_id(1)

    @pl.when(j == 0)
    def _project():
        h_ref[...] = jnp.dot(x_ref[...].astype(BF16), w_in_ref[...], preferred_element_type=F32)

    if n_seg == 1:
        r0 = 0
        first_of_stream = (g % tiles_per_stream) == 0
        last_of_stream = (g % tiles_per_stream) == tiles_per_stream - 1
        t_start = (g % tiles_per_stream) * seg_len
    else:
        r0 = pl.multiple_of(j * seg_len, seg_len)
        first_of_stream = True
        last_of_stream = True
        t_start = 0

    @functools.partial(_when, first_of_stream)
    def _init_state():
        s_ref[...] = s0_ref[0]
        u_ref[0:1, :] = jnp.zeros((1, POOL_WIDTH), F32)
        u_ref[1:HIST_PAD, :] = h0_ref[0]

    u_new = h_ref[pl.ds(r0, seg_len), 0:POOL_WIDTH]
    u_ref[HIST_PAD:HIST_PAD + seg_len, :] = u_new
    pos = (pos0 + t_start + lax.broadcasted_iota(jnp.int32, (seg_len, 1), 0)).astype(F32)
    for gi, w in enumerate(POOL_WINDOWS):
        cols = slice(gi * POOL_CH, (gi + 1) * POOL_CH)
        win = u_ref[HIST_PAD:HIST_PAD + seg_len, cols]
        for back in range(1, w):
            win = win + u_ref[HIST_PAD - back:HIST_PAD - back + seg_len, cols]
        cnt = jnp.minimum(float(w), pos + 1.0)
        d = win * (1.0 / cnt) - u_ref[HIST_PAD:HIST_PAD + seg_len, cols]
        pooled = jnp.dot(d.astype(BF16), wp_ref[gi], preferred_element_type=F32) + bp_ref[:, cols]
        cat_ref[:, cols] = (pooled * ps_ref[:, cols]).astype(BF16)

    @functools.partial(_when, last_of_stream)
    def _emit_hist():
        hnew_ref[0] = u_ref[seg_len + 1:seg_len + HIST_PAD, :]

    u_ref[0:HIST_PAD, :] = u_ref[seg_len:seg_len + HIST_PAD, :]

    scale = RET_HEAD_DIM ** -0.5
    for c in range(seg_len // chunk):
        rows = pl.ds(r0 + c * chunk, chunk)
        trows = slice(c * chunk, (c + 1) * chunk)
        cos_t = cos_ref[trows, :]
        sin_t = sin_ref[trows, :]
        for hd in range(RET_HEADS):
            lo = hd * RET_HEAD_DIM
            hcols = slice(lo, lo + RET_HEAD_DIM)
            q = h_ref[rows, POOL_WIDTH + lo:POOL_WIDTH + lo + RET_HEAD_DIM]
            k = h_ref[rows, POOL_WIDTH + RET_WIDTH + lo:POOL_WIDTH + RET_WIDTH + lo + RET_HEAD_DIM]
            v = h_ref[rows, POOL_WIDTH + 2 * RET_WIDTH + lo:POOL_WIDTH + 2 * RET_WIDTH + lo + RET_HEAD_DIM]
            gate = h_ref[rows, POOL_WIDTH + 3 * RET_WIDTH + lo:POOL_WIDTH + 3 * RET_WIDTH + lo + RET_HEAD_DIM]
            qr = q * cos_t + pltpu.roll(q, RET_HEAD_DIM // 2, 1) * sin_t
            kr = (k * cos_t + pltpu.roll(k, RET_HEAD_DIM // 2, 1) * sin_t) * scale
            vb = v.astype(BF16)
            scores = lax.dot_general(qr.astype(BF16), kr.astype(BF16), (((1,), (1,)), ((), ())),
                                     preferred_element_type=F32) * dec_ref[hd]
            o = jnp.dot(scores.astype(BF16), vb, preferred_element_type=F32)
            s_prev = s_ref[hd]
            o = o + jnp.dot((qr * qd_ref[hd]).astype(BF16), s_prev.astype(BF16), preferred_element_type=F32)
            upd = lax.dot_general((kr * kd_ref[hd]).astype(BF16), vb, (((0,), (0,)), ((), ())),
                                  preferred_element_type=F32)
            s_ref[hd] = g_chunk[hd] * s_prev + upd
            mu = jnp.mean(o, axis=-1, keepdims=True)
            oc = o - mu
            var = jnp.mean(oc * oc, axis=-1, keepdims=True)
            on = oc * lax.rsqrt(var + GN_EPS) * gng_ref[:, hcols] + gnb_ref[:, hcols]
            cat_ref[trows, POOL_WIDTH + lo:POOL_WIDTH + lo + RET_HEAD_DIM] = (on * _silu(gate)).astype(BF16)

    @functools.partial(_when, last_of_stream)
    def _emit_state():
        snew_ref[0] = s_ref[...]


def _retention_tables(chunk, t_len, pos0):
    log_gamma = jnp.log(1.0 - 2.0 ** (-5.0 - jnp.arange(RET_HEADS, dtype=F32)))
    idx = jnp.arange(chunk, dtype=F32)
    diff = idx[:, None] - idx[None, :]
    dec = jnp.where(diff >= 0, jnp.exp(log_gamma[:, None, None] * jnp.maximum(diff, 0.0)), 0.0)
    kd = jnp.exp(log_gamma[:, None] * (chunk - 1.0 - idx)[None, :])
    qd = jnp.exp(log_gamma[:, None] * (idx + 1.0)[None, :])
    kd = jnp.broadcast_to(kd[:, :, None], (RET_HEADS, chunk, RET_HEAD_DIM))
    qd = jnp.broadcast_to(qd[:, :, None], (RET_HEADS, chunk, RET_HEAD_DIM))
    half = RET_HEAD_DIM // 2
    freqs = ROPE_BASE ** (-jnp.arange(half, dtype=F32) / half)
    pos = pos0 + jnp.arange(t_len, dtype=F32)
    ang = pos[:, None] * freqs[None, :]
    cos = jnp.cos(ang)
    sin = jnp.sin(ang)
    cos_t = jnp.concatenate([cos, cos], axis=-1)
    sin_t = jnp.concatenate([-sin, sin], axis=-1)
    g_chunk = tuple(math.exp(math.log(1.0 - 2.0 ** (-5.0 - h)) * chunk) for h in range(RET_HEADS))
    return dec, kd, qd, cos_t, sin_t, g_chunk


def _mixer(x2d, w_in_b, wp_b, bp, ps, gng, gnb, s0, h0, *, n_streams, t_len, tile_rows, seg_len, chunk, pos0):
    rows = n_streams * t_len
    n_tiles = rows // tile_rows
    n_seg = tile_rows // seg_len
    tiles_per_stream = max(t_len // tile_rows, 1)
    assert n_seg == 1 or (seg_len == t_len and n_tiles == 1)
    assert seg_len % chunk == 0 and t_len >= POOL_HIST
    dec, kd, qd, cos_t, sin_t, g_chunk = _retention_tables(chunk, t_len, pos0)

    if n_seg == 1:
        stream_of = lambda g, j: g // tiles_per_stream
        time_of = lambda g, j: g % tiles_per_stream
    else:
        stream_of = lambda g, j: j
        time_of = lambda g, j: 0

    kern = functools.partial(_mixer_kernel, seg_len=seg_len, chunk=chunk, n_seg=n_seg,
                             tiles_per_stream=tiles_per_stream, pos0=pos0, g_chunk=g_chunk)
    return pl.pallas_call(
        kern,
        grid=(n_tiles, n_seg),
        in_specs=[
            pl.BlockSpec((tile_rows, D_MODEL), lambda g, j: (g, 0)),
            _const_spec((D_MODEL, IN_WIDTH)),
            pl.BlockSpec((seg_len, RET_HEAD_DIM), lambda g, j: (time_of(g, j), 0)),
            pl.BlockSpec((seg_len, RET_HEAD_DIM), lambda g, j: (time_of(g, j), 0)),
            _const_spec((RET_HEADS, chunk, chunk)),
            _const_spec((RET_HEADS, chunk, RET_HEAD_DIM)),
            _const_spec((RET_HEADS, chunk, RET_HEAD_DIM)),
            _const_spec((len(POOL_WINDOWS), POOL_CH, POOL_CH)),
            _const_spec((1, POOL_WIDTH)),
            _const_spec((1, POOL_WIDTH)),
            _const_spec((1, RET_WIDTH)),
            _const_spec((1, RET_WIDTH)),
            pl.BlockSpec((1, RET_HEADS, RET_HEAD_DIM, RET_HEAD_DIM), lambda g, j: (stream_of(g, j), 0, 0, 0)),
            pl.BlockSpec((1, POOL_HIST, POOL_WIDTH), lambda g, j: (stream_of(g, j), 0, 0)),
        ],
        out_specs=[
            pl.BlockSpec((seg_len, D_MODEL), lambda g, j: (g * n_seg + j, 0)),
            pl.BlockSpec((1, RET_HEADS, RET_HEAD_DIM, RET_HEAD_DIM), lambda g, j: (stream_of(g, j), 0, 0, 0)),
            pl.BlockSpec((1, POOL_HIST, POOL_WIDTH), lambda g, j: (stream_of(g, j), 0, 0)),
        ],
        out_shape=[
            jax.ShapeDtypeStruct((rows, D_MODEL), BF16),
            jax.ShapeDtypeStruct((n_streams, RET_HEADS, RET_HEAD_DIM, RET_HEAD_DIM), F32),
            jax.ShapeDtypeStruct((n_streams, POOL_HIST, POOL_WIDTH), F32),
        ],
        scratch_shapes=[
            pltpu.VMEM((tile_rows, IN_WIDTH), F32),
            pltpu.VMEM((RET_HEADS, RET_HEAD_DIM, RET_HEAD_DIM), F32),
            pltpu.VMEM((HIST_PAD + seg_len, POOL_WIDTH), F32),
        ],
        compiler_params=pltpu.CompilerParams(dimension_semantics=("arbitrary", "arbitrary"),
                                             vmem_limit_bytes=VMEM_LIMIT),
        name="mixer",
    )(x2d, w_in_b, cos_t, sin_t, dec, kd, qd, wp_b, bp, ps, gng, gnb, s0, h0)


def _row_sources(arrays, n_tiles_first):
    if len(arrays) == 1:
        return [pl.BlockSpec((ROW_TILE, arrays[0].shape[1]), lambda i: (i, 0))]
    first, second = arrays
    return [
        pl.BlockSpec((ROW_TILE, first.shape[1]), lambda i: (jnp.minimum(i, n_tiles_first - 1), 0)),
        pl.BlockSpec((ROW_TILE, second.shape[1]), lambda i: (jnp.maximum(i - n_tiles_first, 0), 0)),
    ]


def _mm_ln_kernel(*refs, n_a, n_res, n_tiles_first, emit_packed):
    a_refs = refs[:n_a]
    w_ref = refs[n_a]
    res_refs = refs[n_a + 1:n_a + 1 + n_res]
    g_ref, b_ref, o_ref = refs[n_a + 1 + n_res:n_a + 4 + n_res]

    def body(a_ref, res_ref):
        acc = jnp.dot(a_ref[...].astype(BF16), w_ref[...], preferred_element_type=F32)
        out = _layer_norm(ALPHA * res_ref[...] + acc, g_ref[...], b_ref[...])
        o_ref[...] = out
        if emit_packed:
            refs[-1][...] = _to_token_tiles(out)

    i = pl.program_id(0)
    pl.when(i < n_tiles_first)(lambda: body(a_refs[0], res_refs[0]))
    pl.when(i >= n_tiles_first)(lambda: body(a_refs[-1], res_refs[-1]))


def _mm_ln(a_arrays, w_b, res_arrays, g, b, *, n_rows, n_tiles_first, name, emit_packed=False):
    n_tiles = n_rows // ROW_TILE
    kern = functools.partial(_mm_ln_kernel, n_a=len(a_arrays), n_res=len(res_arrays),
                             n_tiles_first=n_tiles_first, emit_packed=emit_packed)
    out_specs = [pl.BlockSpec((ROW_TILE, D_MODEL), lambda i: (i, 0))]
    out_shape = [jax.ShapeDtypeStruct((n_rows, D_MODEL), F32)]
    if emit_packed:
        out_specs.append(pl.BlockSpec((ROW_TILE * TOKEN_TILE_ROWS, LANES), lambda i: (i, 0)))
        out_shape.append(jax.ShapeDtypeStruct((n_rows * TOKEN_TILE_ROWS, LANES), BF16))
    return pl.pallas_call(
        kern,
        grid=(n_tiles,),
        in_specs=(_row_sources(a_arrays, n_tiles_first) + [_const_spec(w_b.shape)]
                  + _row_sources(res_arrays, n_tiles_first)
                  + [_const_spec((1, D_MODEL)), _const_spec((1, D_MODEL))]),
        out_specs=out_specs,
        out_shape=out_shape,
        compiler_params=pltpu.CompilerParams(dimension_semantics=("arbitrary",),
                                             vmem_limit_bytes=VMEM_LIMIT),
        name=name,
    )(*a_arrays, w_b, *res_arrays, g, b)


def _matmul_kernel(a_ref, w_ref, o_ref):
    o_ref[...] = jnp.dot(a_ref[...].astype(BF16), w_ref[...], preferred_element_type=F32).astype(o_ref.dtype)


def _matmul(a, w_b, out_dtype, name):
    n_rows = a.shape[0]
    return pl.pallas_call(
        _matmul_kernel,
        grid=(n_rows // ROW_TILE,),
        in_specs=[pl.BlockSpec((ROW_TILE, a.shape[1]), lambda i: (i, 0)), _const_spec(w_b.shape)],
        out_specs=pl.BlockSpec((ROW_TILE, w_b.shape[1]), lambda i: (i, 0)),
        out_shape=jax.ShapeDtypeStruct((n_rows, w_b.shape[1]), out_dtype),
        compiler_params=pltpu.CompilerParams(dimension_semantics=("arbitrary",),
                                             vmem_limit_bytes=VMEM_LIMIT),
        name=name,
    )(a, w_b)


def _attention_kernel(q_ref, k_ref, v_ref, o_ref):
    scale = MEM_HEAD_DIM ** -0.5
    for h in range(MEM_HEADS):
        cols = slice(h * MEM_HEAD_DIM, (h + 1) * MEM_HEAD_DIM)
        s = lax.dot_general(q_ref[:, cols], k_ref[0, :, cols].astype(BF16), (((1,), (1,)), ((), ())),
                            preferred_element_type=F32) * scale
        m = jnp.max(s, axis=-1, keepdims=True)
        p = jnp.exp(s - m)
        p = p * (1.0 / jnp.sum(p, axis=-1, keepdims=True))
        o_ref[:, cols] = jnp.dot(p.astype(BF16), v_ref[0, :, cols].astype(BF16),
                                 preferred_element_type=F32).astype(o_ref.dtype)


def _attention(q_all, mem_k, mem_v, *, n_streams, t_len, q_rows, row_offset, name):
    tiles_per_stream = t_len // q_rows
    base = row_offset // q_rows
    return pl.pallas_call(
        _attention_kernel,
        grid=(n_streams, tiles_per_stream),
        in_specs=[
            pl.BlockSpec((q_rows, D_MODEL), lambda b, t: (base + b * tiles_per_stream + t, 0)),
            pl.BlockSpec((1, N_MEM, D_MODEL), lambda b, t: (b, 0, 0)),
            pl.BlockSpec((1, N_MEM, D_MODEL), lambda b, t: (b, 0, 0)),
        ],
        out_specs=pl.BlockSpec((q_rows, D_MODEL), lambda b, t: (b * tiles_per_stream + t, 0)),
        out_shape=jax.ShapeDtypeStruct((n_streams * t_len, D_MODEL), BF16),
        compiler_params=pltpu.CompilerParams(dimension_semantics=("arbitrary", "arbitrary"),
                                             vmem_limit_bytes=VMEM_LIMIT),
        name=name,
    )(q_all, mem_k, mem_v)


_GROUP_LANE0 = 0
_EXPERT_LANE0 = N_GROUPS
(_META_LANE1, _META_LANE2, _META_POS1, _META_POS2, _META_GATE1, _META_GATE2) = range(6)
(_OUT_SLOT1, _OUT_SLOT2, _OUT_GATE1, _OUT_GATE2) = range(4)
_MOE_TILE_LOG2 = MOE_TILE.bit_length() - 1
assert 1 << _MOE_TILE_LOG2 == MOE_TILE


def _router_kernel(x_ref, wr_ref, out_ref, counts_ref, carry_ref, meta_ref, start_ref):
    phase = pl.program_id(0)
    i = pl.program_id(1)
    rows = x_ref.shape[0]
    tile_rows = pl.ds(pl.multiple_of(i * rows, rows), rows)
    lane = lax.broadcasted_iota(jnp.int32, (rows, LANES), 1)

    @pl.when((phase == 0) & (i == 0))
    def _():
        carry_ref[...] = jnp.zeros_like(carry_ref)

    @pl.when(phase == 0)
    def _route():
        logits = jnp.dot(x_ref[...], wr_ref[...], preferred_element_type=F32, precision=lax.Precision.HIGHEST)

        def first_argmax(vals):
            m = jnp.max(vals, axis=-1, keepdims=True)
            idx = jnp.min(jnp.where(vals == m, lane, LANES), axis=-1, keepdims=True)
            return m, idx

        gl = jnp.where(lane < N_GROUPS, logits, _NEG_INF)
        gm, g_idx = first_argmax(gl)
        g_w = 1.0 / jnp.sum(jnp.exp(gl - gm), axis=-1, keepdims=True)

        in_group = ((lane >= _EXPERT_LANE0) & (lane < _EXPERT_LANE0 + N_EXPERTS)
                    & (((lane - _EXPERT_LANE0) >> 3) == g_idx))
        el = jnp.where(in_group, logits, _NEG_INF)
        m1, i1 = first_argmax(el)
        z = jnp.sum(jnp.exp(el - m1), axis=-1, keepdims=True)
        m2, i2 = first_argmax(jnp.where(lane == i1, _NEG_INF, el))
        p1 = 1.0 / z
        p2 = jnp.exp(m2 - m1) / z
        den = p1 + p2
        gate1 = p1 / den * g_w
        gate2 = p2 / den * g_w

        hit1 = lane == i1
        hit2 = lane == i2
        onehot = (hit1 | hit2).astype(BF16)
        earlier = (lax.broadcasted_iota(jnp.int32, (rows, rows), 1)
                   < lax.broadcasted_iota(jnp.int32, (rows, rows), 0)).astype(BF16)
        rank = jnp.dot(earlier, onehot, preferred_element_type=F32) + carry_ref[...]
        pos1 = jnp.sum(jnp.where(hit1, rank, 0.0), axis=-1, keepdims=True)
        pos2 = jnp.sum(jnp.where(hit2, rank, 0.0), axis=-1, keepdims=True)
        carry_ref[...] = carry_ref[...] + jnp.sum(onehot.astype(F32), axis=0, keepdims=True)

        meta = jnp.zeros((rows, LANES), F32)
        for col, val in ((_META_LANE1, i1.astype(F32)), (_META_LANE2, i2.astype(F32)), (_META_POS1, pos1),
                         (_META_POS2, pos2), (_META_GATE1, gate1), (_META_GATE2, gate2)):
            meta = jnp.where(lane == col, val, meta)
        meta_ref[tile_rows, :] = meta

    @pl.when((phase == 1) & (i == 0))
    def _segment_starts():
        counts_ref[...] = carry_ref[...]
        tiles = ((carry_ref[...].astype(jnp.int32) + (MOE_TILE - 1)) >> _MOE_TILE_LOG2).astype(F32).astype(BF16)
        before = (lax.broadcasted_iota(jnp.int32, (LANES, LANES), 0)
                  < lax.broadcasted_iota(jnp.int32, (LANES, LANES), 1)).astype(BF16)
        tiles8 = jnp.broadcast_to(tiles, (8, LANES))
        start_ref[...] = jnp.dot(tiles8, before, preferred_element_type=F32)[0:1, :] * float(MOE_TILE)

    @pl.when(phase == 1)
    def _slots():
        meta = meta_ref[tile_rows, :]
        starts = start_ref[...]

        def col(c):
            return jnp.sum(jnp.where(lane == c, meta, 0.0), axis=-1, keepdims=True)

        def start_of(expert_lane):
            return jnp.sum(jnp.where(lane == expert_lane.astype(jnp.int32), starts, 0.0), axis=-1, keepdims=True)

        slot1 = start_of(col(_META_LANE1)) + col(_META_POS1)
        slot2 = start_of(col(_META_LANE2)) + col(_META_POS2)
        out = jnp.zeros((rows, LANES), F32)
        for c, val in ((_OUT_SLOT1, slot1), (_OUT_SLOT2, slot2), (_OUT_GATE1, col(_META_GATE1)),
                       (_OUT_GATE2, col(_META_GATE2))):
            out = jnp.where(lane == c, val, out)
        out_ref[...] = out


def _router(x_all, w_router):
    n_rows = x_all.shape[0]
    n_tiles = n_rows // ROW_TILE
    return pl.pallas_call(
        _router_kernel,
        grid=(2, n_tiles),
        in_specs=[pl.BlockSpec((ROW_TILE, D_MODEL), lambda p, i: (i * (1 - p) + (n_tiles - 1) * p, 0)),
                  _const_spec((D_MODEL, LANES))],
        out_specs=[pl.BlockSpec((ROW_TILE, LANES), lambda p, i: (i * p, 0)),
                   pl.BlockSpec((1, LANES), lambda p, i: (0, 0))],
        out_shape=[jax.ShapeDtypeStruct((n_rows, LANES), F32), jax.ShapeDtypeStruct((1, LANES), F32)],
        scratch_shapes=[pltpu.VMEM((1, LANES), F32), pltpu.VMEM((n_rows, LANES), F32), pltpu.VMEM((1, LANES), F32)],
        compiler_params=pltpu.CompilerParams(dimension_semantics=("arbitrary", "arbitrary"),
                                             vmem_limit_bytes=VMEM_LIMIT),
        name="router",
    )(x_all, w_router)


def _experts_kernel(tile_start_ref, n_active_ref, slots_ref, xg_hbm, w1_ref, w3_ref, w2_ref, y_hbm,
                    xbuf, ybuf, w1b, w3b, w2b, src_ref, gsem, ysem, *, n_tiles_max):
    e = pl.program_id(0)
    n_active = n_active_ref[0]
    tile_words = MOE_TILE * TOKEN_TILE_ROWS

    def gather(tile, buf):
        base = tile * MOE_TILE
        for r in range(MOE_TILE):
            tok = src_ref[base + r]
            pltpu.make_async_copy(xg_hbm.at[pl.ds(pl.multiple_of(tok * TOKEN_TILE_ROWS, TOKEN_TILE_ROWS),
                                                  TOKEN_TILE_ROWS), :],
                                  xbuf.at[buf, pl.ds(r * TOKEN_TILE_ROWS, TOKEN_TILE_ROWS), :], gsem.at[buf]).start()

    def gather_wait(buf):
        pltpu.make_async_copy(xg_hbm.at[pl.ds(0, tile_words), :], xbuf.at[buf], gsem.at[buf]).wait()

    def y_copy(tile, buf):
        rows = pl.ds(pl.multiple_of(tile * tile_words, tile_words), tile_words)
        return pltpu.make_async_copy(ybuf.at[buf], y_hbm.at[rows, :], ysem.at[buf])

    @pl.when(e == 0)
    def _plan():
        def clear(p, carry):
            src_ref[p] = 0
            return carry
        lax.fori_loop(0, src_ref.shape[0], clear, 0, unroll=16)

        def put(n, carry):
            src_ref[slots_ref[n]] = n >> 1
            return carry
        lax.fori_loop(0, slots_ref.shape[0], put, 0, unroll=16)
        gather(0, 0)

    w1b[...] = w1_ref[0, 0, 0].astype(BF16)
    w3b[...] = w3_ref[0, 0, 0].astype(BF16)
    w2b[...] = w2_ref[0, 0, 0].astype(BF16)
    first_tile = tile_start_ref[e]

    def tile_body(t, carry):
        g = first_tile + t
        buf = g % 2
        gather_wait(buf)

        @pl.when(g >= 2)
        def _():
            y_copy(g, buf).wait()

        gather(g + 1, 1 - buf)
        x = _from_token_tiles(xbuf[buf])
        a = jnp.dot(x, w1b[...], preferred_element_type=F32)
        b = jnp.dot(x, w3b[...], preferred_element_type=F32)
        hdn = (_silu(a) * b).astype(BF16)
        y = jnp.dot(hdn, w2b[...], preferred_element_type=F32)
        ybuf[buf] = _to_token_tiles(y)
        y_copy(g, buf).start()
        return carry

    lax.fori_loop(0, tile_start_ref[e + 1] - first_tile, tile_body, 0)

    @pl.when(e == N_EXPERTS - 1)
    def _drain():
        gather_wait(n_active % 2)

        @pl.when(n_active >= 2)
        def _():
            y_copy(0, n_active % 2).wait()

        @pl.when(n_active >= 1)
        def _():
            y_copy(0, (n_active + 1) % 2).wait()

        ybuf[0] = jnp.zeros(ybuf.shape[1:], ybuf.dtype)

        def fill(g, carry):
            cp = y_copy(g, 0)
            cp.start()
            cp.wait()
            return carry
        lax.fori_loop(n_active, n_tiles_max, fill, 0)


def _experts(tile_start, n_active, slots, xg, w1, w3, w2, n_tiles_max):
    def expert_block(e, ts, na, sl):
        return (0, e // EXPERTS_PER_GROUP, e % EXPERTS_PER_GROUP, 0, 0)

    tile_words = MOE_TILE * TOKEN_TILE_ROWS
    kern = functools.partial(_experts_kernel, n_tiles_max=n_tiles_max)
    return pl.pallas_call(
        kern,
        grid_spec=pltpu.PrefetchScalarGridSpec(
            num_scalar_prefetch=3,
            grid=(N_EXPERTS,),
            in_specs=[
                pl.BlockSpec(memory_space=pl.ANY),
                pl.BlockSpec((1, 1, 1, D_MODEL, EXPERT_HIDDEN), expert_block),
                pl.BlockSpec((1, 1, 1, D_MODEL, EXPERT_HIDDEN), expert_block),
                pl.BlockSpec((1, 1, 1, EXPERT_HIDDEN, D_MODEL), expert_block),
            ],
            out_specs=pl.BlockSpec(memory_space=pl.ANY),
            scratch_shapes=[
                pltpu.VMEM((2, tile_words, LANES), BF16),
                pltpu.VMEM((2, tile_words, LANES), BF16),
                pltpu.VMEM((D_MODEL, EXPERT_HIDDEN), BF16),
                pltpu.VMEM((D_MODEL, EXPERT_HIDDEN), BF16),
                pltpu.VMEM((EXPERT_HIDDEN, D_MODEL), BF16),
                pltpu.SMEM(((n_tiles_max + 1) * MOE_TILE,), jnp.int32),
                pltpu.SemaphoreType.DMA((2,)),
                pltpu.SemaphoreType.DMA((2,)),
            ],
        ),
        out_shape=jax.ShapeDtypeStruct((n_tiles_max * tile_words, LANES), BF16),
        compiler_params=pltpu.CompilerParams(dimension_semantics=("arbitrary",),
                                             vmem_limit_bytes=VMEM_LIMIT),
        name="experts",
    )(tile_start, n_active, slots, xg, w1, w3, w2)


def _combine_kernel(slot_ref, y_hbm, gates_ref, x_ref, g_ref, b_ref, op_ref, os_ref, ybuf, sem,
                    *, n_tiles, n_tiles_first):
    i = pl.program_id(0)

    tile_words = COMBINE_TILE * TOKEN_TILE_ROWS

    def gather(tile, buf):
        for r in range(COMBINE_TILE):
            tok = tile * COMBINE_TILE + r
            for k in range(2):
                row0 = pl.multiple_of(slot_ref[2 * tok + k] * TOKEN_TILE_ROWS, TOKEN_TILE_ROWS)
                pltpu.make_async_copy(y_hbm.at[pl.ds(row0, TOKEN_TILE_ROWS), :],
                                      ybuf.at[buf, k, pl.ds(r * TOKEN_TILE_ROWS, TOKEN_TILE_ROWS), :],
                                      sem.at[buf]).start()

    buf = i % 2

    def wait_all():
        for k in range(2):
            pltpu.make_async_copy(y_hbm.at[pl.ds(0, tile_words), :], ybuf.at[buf, k], sem.at[buf]).wait()

    def compute():
        y = (gates_ref[:, 0:1] * _from_token_tiles(ybuf[buf, 0]).astype(F32)
             + gates_ref[:, 1:2] * _from_token_tiles(ybuf[buf, 1]).astype(F32))
        out = _layer_norm(ALPHA * x_ref[...] + y, g_ref[...], b_ref[...])

        @pl.when(i < n_tiles_first)
        def _():
            op_ref[...] = out

        @pl.when(i >= n_tiles_first)
        def _():
            os_ref[...] = out

    @pl.when(i == 0)
    def _():
        gather(0, 0)

    @pl.when(i + 1 < n_tiles)
    def _steady():
        wait_all()
        gather(i + 1, 1 - buf)
        compute()

    @pl.when(i + 1 == n_tiles)
    def _last():
        wait_all()
        compute()


def _combine(slots, y_sorted, gates, x_all, g, b, n_rows_first):
    n_rows = x_all.shape[0]
    n_tiles = n_rows // COMBINE_TILE
    n_first = n_rows_first // COMBINE_TILE
    kern = functools.partial(_combine_kernel, n_tiles=n_tiles, n_tiles_first=n_first)
    return pl.pallas_call(
        kern,
        grid_spec=pltpu.PrefetchScalarGridSpec(
            num_scalar_prefetch=1,
            grid=(n_tiles,),
            in_specs=[
                pl.BlockSpec(memory_space=pl.ANY),
                pl.BlockSpec((COMBINE_TILE, 2), lambda i, sl: (i, 0)),
                pl.BlockSpec((COMBINE_TILE, D_MODEL), lambda i, sl: (i, 0)),
                pl.BlockSpec((1, D_MODEL), lambda i, sl: (0, 0)),
                pl.BlockSpec((1, D_MODEL), lambda i, sl: (0, 0)),
            ],
            out_specs=[
                pl.BlockSpec((COMBINE_TILE, D_MODEL), lambda i, sl: (jnp.minimum(i, n_first - 1), 0)),
                pl.BlockSpec((COMBINE_TILE, D_MODEL), lambda i, sl: (jnp.maximum(i - n_first, 0), 0)),
            ],
            scratch_shapes=[pltpu.VMEM((2, 2, COMBINE_TILE * TOKEN_TILE_ROWS, LANES), BF16),
                            pltpu.SemaphoreType.DMA((2,))],
        ),
        out_shape=[jax.ShapeDtypeStruct((n_rows_first, D_MODEL), F32),
                   jax.ShapeDtypeStruct((n_rows - n_rows_first, D_MODEL), F32)],
        compiler_params=pltpu.CompilerParams(dimension_semantics=("arbitrary",),
                                             vmem_limit_bytes=VMEM_LIMIT),
        name="combine",
    )(slots, y_sorted, gates, x_all, g, b)


def _dispatch_plan(routed, counts):
    slots = routed[:, _OUT_SLOT1:_OUT_SLOT2 + 1].astype(jnp.int32).reshape(-1)
    gates = routed[:, _OUT_GATE1:_OUT_GATE2 + 1]
    cnt = counts[0, _EXPERT_LANE0:_EXPERT_LANE0 + N_EXPERTS].astype(jnp.int32)
    tile_end = jnp.cumsum((cnt + MOE_TILE - 1) // MOE_TILE)
    tile_start = jnp.concatenate([jnp.zeros((1,), jnp.int32), tile_end]).astype(jnp.int32)
    return tile_start, tile_end[-1:].astype(jnp.int32), slots, gates


def kernel(x_prompt, x_sample, mem_prompt, state_pool, state_ret, cache_mem_k, cache_mem_v, w_in, w_pool, b_pool,
           pool_scale, ret_gn_g, ret_gn_b, w_out, ln1_g, ln1_b, w_mq, w_mk, w_mv, w_mo, ln2_g, ln2_b, w_rg, w_re,
           w1, w3, w2, ln3_g, ln3_b):
    assert w_in.shape[0] == DEPTH == 1
    bp_n, tp, _ = x_prompt.shape
    bs_n, ts, _ = x_sample.shape
    rows_p, rows_s = bp_n * tp, bs_n * ts
    assert rows_s == ROW_TILE and rows_p % ROW_TILE == 0
    n_rows = rows_p + rows_s
    tiles_p = rows_p // ROW_TILE

    w_in_b = w_in[0].astype(BF16)
    w_out_b = w_out[0].astype(BF16)
    w_mq_b = w_mq[0].astype(BF16)
    w_mk_b = w_mk[0].astype(BF16)
    w_mv_b = w_mv[0].astype(BF16)
    w_mo_b = w_mo[0].astype(BF16)
    wp_b = w_pool[0].astype(BF16)
    bp = b_pool[0].reshape(1, POOL_WIDTH)
    ps = pool_scale[0].reshape(1, POOL_WIDTH)
    gng = ret_gn_g[0].reshape(1, RET_WIDTH)
    gnb = ret_gn_b[0].reshape(1, RET_WIDTH)
    row = lambda p: p[0].reshape(1, D_MODEL)

    xp2d = x_prompt.reshape(rows_p, D_MODEL)
    xs2d = x_sample.reshape(rows_s, D_MODEL)

    mem2d = mem_prompt.reshape(bp_n * N_MEM, D_MODEL)
    mk_p = _matmul(mem2d, w_mk_b, F32, "mem_k").reshape(bp_n, N_MEM, D_MODEL)
    mv_p = _matmul(mem2d, w_mv_b, F32, "mem_v").reshape(bp_n, N_MEM, D_MODEL)

    zeros_s = jnp.zeros((bp_n, RET_HEADS, RET_HEAD_DIM, RET_HEAD_DIM), F32)
    zeros_h = jnp.zeros((bp_n, POOL_HIST, POOL_WIDTH), F32)
    cat_p, ret_p, pool_p = _mixer(xp2d, w_in_b, wp_b, bp, ps, gng, gnb, zeros_s, zeros_h,
                                  n_streams=bp_n, t_len=tp, tile_rows=256, seg_len=256, chunk=256, pos0=0)
    cat_s, ret_s, pool_s = _mixer(xs2d, w_in_b, wp_b, bp, ps, gng, gnb, state_ret[0], state_pool[0],
                                  n_streams=bs_n, t_len=ts, tile_rows=rows_s, seg_len=ts, chunk=ts, pos0=PAST_LEN)
    (x1,) = _mm_ln([cat_p, cat_s], w_out_b, [xp2d, xs2d], row(ln1_g), row(ln1_b),
                   n_rows=n_rows, n_tiles_first=tiles_p, name="out_ln1")

    q_all = _matmul(x1, w_mq_b, BF16, "mem_q")
    o_p = _attention(q_all, mk_p, mv_p, n_streams=bp_n, t_len=tp, q_rows=ROW_TILE, row_offset=0, name="attn_prompt")
    o_s = _attention(q_all, cache_mem_k[0].reshape(bs_n, N_MEM, D_MODEL), cache_mem_v[0].reshape(bs_n, N_MEM, D_MODEL),
                     n_streams=bs_n, t_len=ts, q_rows=ts, row_offset=rows_p, name="attn_sample")
    x2, x2_tiles = _mm_ln([o_p, o_s], w_mo_b, [x1], row(ln2_g), row(ln2_b),
                          n_rows=n_rows, n_tiles_first=tiles_p, name="mo_ln2", emit_packed=True)

    w_router = jnp.concatenate([w_rg[0], w_re[0].reshape(D_MODEL, N_EXPERTS),
                                jnp.zeros((D_MODEL, LANES - N_GROUPS - N_EXPERTS), F32)], axis=1)
    routed, counts = _router(x2, w_router)
    n_tiles = (2 * n_rows) // MOE_TILE + N_EXPERTS
    tile_start, n_active, slots, gates = _dispatch_plan(routed, counts)
    y_sorted = _experts(tile_start, n_active, slots, x2_tiles, w1, w3, w2, n_tiles)
    y_p, y_s = _combine(slots, y_sorted, gates, x2, row(ln3_g), row(ln3_b), rows_p)

    kv_shape = (DEPTH, bp_n, N_MEM, MEM_HEADS, MEM_HEAD_DIM)
    return (y_p.reshape(bp_n, tp, D_MODEL), y_s.reshape(bs_n, ts, D_MODEL), pool_p[None], ret_p[None],
            mk_p.reshape(kv_shape), mv_p.reshape(kv_shape), pool_s[None], ret_s[None])
```

```python
import functools
import math

import jax
import jax.numpy as jnp
from jax import lax
from jax.experimental import pallas as pl
from jax.experimental.pallas import tpu as pltpu

F32 = jnp.float32
BF16 = jnp.bfloat16

D_MODEL = 2048
POOL_WIDTH = 1024
POOL_WINDOWS = (2, 4, 8, 16)
POOL_CH = 256
POOL_HIST = 15
RET_WIDTH = 1024
RET_HEADS = 8
RET_HEAD_DIM = 128
IN_WIDTH = POOL_WIDTH + 4 * RET_WIDTH
ROPE_BASE = 10000.0
N_MEM = 256
MEM_HEADS = 4
MEM_HEAD_DIM = 512
N_GROUPS = 4
EXPERTS_PER_GROUP = 8
N_EXPERTS = N_GROUPS * EXPERTS_PER_GROUP
EXPERT_HIDDEN = 512
LN_EPS = 1e-5
GN_EPS = 1e-6
DEPTH = 1
ALPHA = (2.0 * DEPTH) ** 0.25
PAST_LEN = 2048
PROMPT_CHUNK = 64

LANES = 128
HIST_PAD = 16
ROW_TILE = 512
MIXER_TILE = 256
MOE_TILE = 256
COMBINE_TILE = 256
VMEM_LIMIT = 58 * 1024 * 1024

_NEG_INF = float("-inf")


def _const_spec(shape):
    zeros = (0,) * len(shape)
    return pl.BlockSpec(shape, lambda *_: zeros, pipeline_mode=pl.Buffered(1))


def _when(cond, fn):
    if cond is True:
        fn()
    else:
        pl.when(cond)(fn)


def _layer_norm(z, g, b):
    mu = jnp.mean(z, axis=-1, keepdims=True)
    zc = z - mu
    var = jnp.mean(zc * zc, axis=-1, keepdims=True)
    return zc * lax.rsqrt(var + LN_EPS) * g + b


def _silu(a):
    return a * (1.0 / (1.0 + jnp.exp(-a)))


TOKEN_TILE_ROWS = D_MODEL // LANES


def _to_token_tiles(x):
    rows = x.shape[0]
    return x.astype(BF16).reshape(rows, TOKEN_TILE_ROWS, LANES).reshape(rows * TOKEN_TILE_ROWS, LANES)


def _from_token_tiles(tiles):
    rows = tiles.shape[0] // TOKEN_TILE_ROWS
    return tiles.reshape(rows, TOKEN_TILE_ROWS, LANES).reshape(rows, D_MODEL)


def _mix_segment(h_ref, r0, cos_ref, sin_ref, dec_ref, kd_ref, qd_ref, wp_ref, bp_ref, ps_ref, gng_ref, gnb_ref,
                 s0_ref, h0_ref, cat_ref, snew_ref, hnew_ref, s_ref, u_ref,
                 *, seg_len, chunk, pos_start, first, g_chunk):
    hist = jnp.concatenate([jnp.zeros((1, POOL_WIDTH), F32), h0_ref[0]], axis=0)
    if first is True:
        s_ref[...] = s0_ref[0]
        u_ref[0:HIST_PAD, :] = hist
    else:
        s_ref[...] = jnp.where(first, s0_ref[0], s_ref[...])
        u_ref[0:HIST_PAD, :] = jnp.where(first, hist, u_ref[0:HIST_PAD, :])

    u_new = h_ref[pl.ds(r0, seg_len), 0:POOL_WIDTH]
    u_ref[HIST_PAD:HIST_PAD + seg_len, :] = u_new
    pos = (pos_start + lax.broadcasted_iota(jnp.int32, (seg_len, 1), 0)).astype(F32)
    for gi, w in enumerate(POOL_WINDOWS):
        cols = slice(gi * POOL_CH, (gi + 1) * POOL_CH)
        win = u_ref[HIST_PAD:HIST_PAD + seg_len, cols]
        for back in range(1, w):
            win = win + u_ref[HIST_PAD - back:HIST_PAD - back + seg_len, cols]
        cnt = jnp.minimum(float(w), pos + 1.0)
        d = win * (1.0 / cnt) - u_ref[HIST_PAD:HIST_PAD + seg_len, cols]
        pooled = jnp.dot(d.astype(BF16), wp_ref[gi], preferred_element_type=F32) + bp_ref[:, cols]
        cat_ref[:, cols] = (pooled * ps_ref[:, cols]).astype(BF16)

    hnew_ref[0] = u_ref[seg_len + 1:seg_len + HIST_PAD, :]
    u_ref[0:HIST_PAD, :] = u_ref[seg_len:seg_len + HIST_PAD, :]

    scale = RET_HEAD_DIM ** -0.5
    for c in range(seg_len // chunk):
        rows = pl.ds(r0 + c * chunk, chunk)
        trows = slice(c * chunk, (c + 1) * chunk)
        cos_t = cos_ref[trows, :]
        sin_t = sin_ref[trows, :]
        for hd in range(RET_HEADS):
            lo = hd * RET_HEAD_DIM
            hcols = slice(lo, lo + RET_HEAD_DIM)
            q = h_ref[rows, POOL_WIDTH + lo:POOL_WIDTH + lo + RET_HEAD_DIM]
            k = h_ref[rows, POOL_WIDTH + RET_WIDTH + lo:POOL_WIDTH + RET_WIDTH + lo + RET_HEAD_DIM]
            v = h_ref[rows, POOL_WIDTH + 2 * RET_WIDTH + lo:POOL_WIDTH + 2 * RET_WIDTH + lo + RET_HEAD_DIM]
            gate = h_ref[rows, POOL_WIDTH + 3 * RET_WIDTH + lo:POOL_WIDTH + 3 * RET_WIDTH + lo + RET_HEAD_DIM]
            qr = q * cos_t + pltpu.roll(q, RET_HEAD_DIM // 2, 1) * sin_t
            kr = (k * cos_t + pltpu.roll(k, RET_HEAD_DIM // 2, 1) * sin_t) * scale
            vb = v.astype(BF16)
            scores = lax.dot_general(qr.astype(BF16), kr.astype(BF16), (((1,), (1,)), ((), ())),
                                     preferred_element_type=F32) * dec_ref[hd]
            o = jnp.dot(scores.astype(BF16), vb, preferred_element_type=F32)
            s_prev = s_ref[hd]
            o = o + jnp.dot((qr * qd_ref[hd]).astype(BF16), s_prev.astype(BF16), preferred_element_type=F32)
            upd = lax.dot_general((kr * kd_ref[hd]).astype(BF16), vb, (((0,), (0,)), ((), ())),
                                  preferred_element_type=F32)
            s_ref[hd] = g_chunk[hd] * s_prev + upd
            mu = jnp.mean(o, axis=-1, keepdims=True)
            oc = o - mu
            var = jnp.mean(oc * oc, axis=-1, keepdims=True)
            on = oc * lax.rsqrt(var + GN_EPS) * gng_ref[:, hcols] + gnb_ref[:, hcols]
            cat_ref[trows, POOL_WIDTH + lo:POOL_WIDTH + lo + RET_HEAD_DIM] = (on * _silu(gate)).astype(BF16)

    snew_ref[0] = s_ref[...]


def _mixer_segments_kernel(x_ref, w_in_ref, *refs, seg_len, chunk, pos0, g_chunk):
    (*mix_refs, h_ref, s_ref, u_ref) = refs
    j = pl.program_id(0)

    @pl.when(j == 0)
    def _project():
        h_ref[...] = jnp.dot(x_ref[...].astype(BF16), w_in_ref[...], preferred_element_type=F32)

    _mix_segment(h_ref, pl.multiple_of(j * seg_len, seg_len), *mix_refs, s_ref, u_ref,
                 seg_len=seg_len, chunk=chunk, pos_start=pos0, first=True, g_chunk=g_chunk)


def _mixer_pipelined_kernel(x_ref, w_in_ref, *refs, tiles_per_stream, seg_len, chunk, pos0, g_chunk):
    (*mix_refs, ha_ref, hb_ref, s_ref, u_ref) = refs
    g = pl.program_id(0)
    m = jnp.maximum(g - 1, 0)
    first = ((m % tiles_per_stream) == 0) | (g == 0)
    pos_start = pos0 + (m % tiles_per_stream) * seg_len

    @pl.when(g == 0)
    def _():
        hb_ref[...] = jnp.zeros_like(hb_ref)

    def step(h_write, h_read):
        _mix_segment(h_read, 0, *mix_refs, s_ref, u_ref, seg_len=seg_len, chunk=chunk, pos_start=pos_start,
                     first=first, g_chunk=g_chunk)
        h_write[...] = jnp.dot(x_ref[...].astype(BF16), w_in_ref[...], preferred_element_type=F32)

    pl.when(g % 2 == 0)(lambda: step(ha_ref, hb_ref))
    pl.when(g % 2 == 1)(lambda: step(hb_ref, ha_ref))


def _retention_tables(chunk, t_len, pos0):
    log_gamma = jnp.log(1.0 - 2.0 ** (-5.0 - jnp.arange(RET_HEADS, dtype=F32)))
    idx = jnp.arange(chunk, dtype=F32)
    diff = idx[:, None] - idx[None, :]
    dec = jnp.where(diff >= 0, jnp.exp(log_gamma[:, None, None] * jnp.maximum(diff, 0.0)), 0.0)
    kd = jnp.exp(log_gamma[:, None] * (chunk - 1.0 - idx)[None, :])
    qd = jnp.exp(log_gamma[:, None] * (idx + 1.0)[None, :])
    kd = jnp.broadcast_to(kd[:, :, None], (RET_HEADS, chunk, RET_HEAD_DIM))
    qd = jnp.broadcast_to(qd[:, :, None], (RET_HEADS, chunk, RET_HEAD_DIM))
    half = RET_HEAD_DIM // 2
    freqs = ROPE_BASE ** (-jnp.arange(half, dtype=F32) / half)
    pos = pos0 + jnp.arange(t_len, dtype=F32)
    ang = pos[:, None] * freqs[None, :]
    cos = jnp.cos(ang)
    sin = jnp.sin(ang)
    cos_t = jnp.concatenate([cos, cos], axis=-1)
    sin_t = jnp.concatenate([-sin, sin], axis=-1)
    g_chunk = tuple(math.exp(math.log(1.0 - 2.0 ** (-5.0 - h)) * chunk) for h in range(RET_HEADS))
    return dec, kd, qd, cos_t, sin_t, g_chunk


def _mixer(x2d, w_in_b, wp_b, bp, ps, gng, gnb, s0, h0, *, n_streams, t_len, tile_rows, chunk, pos0):
    rows = n_streams * t_len
    pipelined = t_len > tile_rows
    seg_len = tile_rows if pipelined else t_len
    assert (t_len % tile_rows == 0) if pipelined else (rows == tile_rows)
    assert seg_len % chunk == 0 and t_len >= POOL_HIST
    dec, kd, qd, cos_t, sin_t, g_chunk = _retention_tables(chunk, t_len, pos0)
    state_block = (1, RET_HEADS, RET_HEAD_DIM, RET_HEAD_DIM)
    hist_block = (1, POOL_HIST, POOL_WIDTH)

    if pipelined:
        n_tiles = rows // tile_rows
        tiles_per_stream = t_len // tile_rows
        grid = (n_tiles + 1,)
        mixed = lambda g: jnp.maximum(g - 1, 0)
        x_map = lambda g: (jnp.minimum(g, n_tiles - 1), 0)
        time_map = lambda g: (mixed(g) % tiles_per_stream, 0)
        cat_map = lambda g: (mixed(g), 0)
        state_map = lambda g: (mixed(g) // tiles_per_stream, 0, 0, 0)
        hist_map = lambda g: (mixed(g) // tiles_per_stream, 0, 0)
        kern = functools.partial(_mixer_pipelined_kernel, tiles_per_stream=tiles_per_stream, seg_len=seg_len,
                                 chunk=chunk, pos0=pos0, g_chunk=g_chunk)
        h_scratch = [pltpu.VMEM((tile_rows, IN_WIDTH), F32), pltpu.VMEM((tile_rows, IN_WIDTH), F32)]
    else:
        grid = (n_streams,)
        x_map = lambda j: (0, 0)
        time_map = lambda j: (0, 0)
        cat_map = lambda j: (j, 0)
        state_map = lambda j: (j, 0, 0, 0)
        hist_map = lambda j: (j, 0, 0)
        kern = functools.partial(_mixer_segments_kernel, seg_len=seg_len, chunk=chunk, pos0=pos0, g_chunk=g_chunk)
        h_scratch = [pltpu.VMEM((tile_rows, IN_WIDTH), F32)]

    return pl.pallas_call(
        kern,
        grid=grid,
        in_specs=[
            pl.BlockSpec((tile_rows, D_MODEL), x_map),
            _const_spec((D_MODEL, IN_WIDTH)),
            pl.BlockSpec((seg_len, RET_HEAD_DIM), time_map),
            pl.BlockSpec((seg_len, RET_HEAD_DIM), time_map),
            _const_spec((RET_HEADS, chunk, chunk)),
            _const_spec((RET_HEADS, chunk, RET_HEAD_DIM)),
            _const_spec((RET_HEADS, chunk, RET_HEAD_DIM)),
            _const_spec((len(POOL_WINDOWS), POOL_CH, POOL_CH)),
            _const_spec((1, POOL_WIDTH)),
            _const_spec((1, POOL_WIDTH)),
            _const_spec((1, RET_WIDTH)),
            _const_spec((1, RET_WIDTH)),
            pl.BlockSpec(state_block, state_map),
            pl.BlockSpec(hist_block, hist_map),
        ],
        out_specs=[
            pl.BlockSpec((seg_len, D_MODEL), cat_map),
            pl.BlockSpec(state_block, state_map),
            pl.BlockSpec(hist_block, hist_map),
        ],
        out_shape=[
            jax.ShapeDtypeStruct((rows, D_MODEL), BF16),
            jax.ShapeDtypeStruct((n_streams, RET_HEADS, RET_HEAD_DIM, RET_HEAD_DIM), F32),
            jax.ShapeDtypeStruct((n_streams, POOL_HIST, POOL_WIDTH), F32),
        ],
        scratch_shapes=h_scratch + [
            pltpu.VMEM((RET_HEADS, RET_HEAD_DIM, RET_HEAD_DIM), F32),
            pltpu.VMEM((HIST_PAD + seg_len, POOL_WIDTH), F32),
        ],
        compiler_params=pltpu.CompilerParams(dimension_semantics=("arbitrary",),
                                             vmem_limit_bytes=VMEM_LIMIT),
        name="mixer",
    )(x2d, w_in_b, cos_t, sin_t, dec, kd, qd, wp_b, bp, ps, gng, gnb, s0, h0)


def _row_sources(arrays, n_tiles_first):
    if len(arrays) == 1:
        return [pl.BlockSpec((ROW_TILE, arrays[0].shape[1]), lambda i: (i, 0))]
    first, second = arrays
    return [
        pl.BlockSpec((ROW_TILE, first.shape[1]), lambda i: (jnp.minimum(i, n_tiles_first - 1), 0)),
        pl.BlockSpec((ROW_TILE, second.shape[1]), lambda i: (jnp.maximum(i - n_tiles_first, 0), 0)),
    ]


def _mm_ln_kernel(*refs, n_a, n_res, n_tiles_first, emit_packed):
    a_refs = refs[:n_a]
    w_ref = refs[n_a]
    res_refs = refs[n_a + 1:n_a + 1 + n_res]
    g_ref, b_ref, o_ref = refs[n_a + 1 + n_res:n_a + 4 + n_res]

    def body(a_ref, res_ref):
        acc = jnp.dot(a_ref[...].astype(BF16), w_ref[...], preferred_element_type=F32)
        out = _layer_norm(ALPHA * res_ref[...] + acc, g_ref[...], b_ref[...])
        o_ref[...] = out
        if emit_packed:
            refs[-1][...] = _to_token_tiles(out)

    i = pl.program_id(0)
    pl.when(i < n_tiles_first)(lambda: body(a_refs[0], res_refs[0]))
    pl.when(i >= n_tiles_first)(lambda: body(a_refs[-1], res_refs[-1]))


def _mm_ln(a_arrays, w_b, res_arrays, g, b, *, n_rows, n_tiles_first, name, emit_packed=False):
    n_tiles = n_rows // ROW_TILE
    kern = functools.partial(_mm_ln_kernel, n_a=len(a_arrays), n_res=len(res_arrays),
                             n_tiles_first=n_tiles_first, emit_packed=emit_packed)
    out_specs = [pl.BlockSpec((ROW_TILE, D_MODEL), lambda i: (i, 0))]
    out_shape = [jax.ShapeDtypeStruct((n_rows, D_MODEL), F32)]
    if emit_packed:
        out_specs.append(pl.BlockSpec((ROW_TILE * TOKEN_TILE_ROWS, LANES), lambda i: (i, 0)))
        out_shape.append(jax.ShapeDtypeStruct((n_rows * TOKEN_TILE_ROWS, LANES), BF16))
    return pl.pallas_call(
        kern,
        grid=(n_tiles,),
        in_specs=(_row_sources(a_arrays, n_tiles_first) + [_const_spec(w_b.shape)]
                  + _row_sources(res_arrays, n_tiles_first)
                  + [_const_spec((1, D_MODEL)), _const_spec((1, D_MODEL))]),
        out_specs=out_specs,
        out_shape=out_shape,
        compiler_params=pltpu.CompilerParams(dimension_semantics=("arbitrary",),
                                             vmem_limit_bytes=VMEM_LIMIT),
        name=name,
    )(*a_arrays, w_b, *res_arrays, g, b)


def _matmul_kernel(a_ref, w_ref, o_ref):
    o_ref[...] = jnp.dot(a_ref[...].astype(BF16), w_ref[...], preferred_element_type=F32).astype(o_ref.dtype)


def _matmul(a, w_b, out_dtype, name):
    n_rows = a.shape[0]
    return pl.pallas_call(
        _matmul_kernel,
        grid=(n_rows // ROW_TILE,),
        in_specs=[pl.BlockSpec((ROW_TILE, a.shape[1]), lambda i: (i, 0)), _const_spec(w_b.shape)],
        out_specs=pl.BlockSpec((ROW_TILE, w_b.shape[1]), lambda i: (i, 0)),
        out_shape=jax.ShapeDtypeStruct((n_rows, w_b.shape[1]), out_dtype),
        compiler_params=pltpu.CompilerParams(dimension_semantics=("arbitrary",),
                                             vmem_limit_bytes=VMEM_LIMIT),
        name=name,
    )(a, w_b)


def _attention_kernel(q_ref, k_ref, v_ref, o_ref):
    scale = MEM_HEAD_DIM ** -0.5
    for h in range(MEM_HEADS):
        cols = slice(h * MEM_HEAD_DIM, (h + 1) * MEM_HEAD_DIM)
        s = lax.dot_general(q_ref[:, cols], k_ref[0, :, cols].astype(BF16), (((1,), (1,)), ((), ())),
                            preferred_element_type=F32) * scale
        m = jnp.max(s, axis=-1, keepdims=True)
        p = jnp.exp(s - m)
        p = p * (1.0 / jnp.sum(p, axis=-1, keepdims=True))
        o_ref[:, cols] = jnp.dot(p.astype(BF16), v_ref[0, :, cols].astype(BF16),
                                 preferred_element_type=F32).astype(o_ref.dtype)


def _attention(q_all, mem_k, mem_v, *, n_streams, t_len, q_rows, row_offset, name):
    tiles_per_stream = t_len // q_rows
    base = row_offset // q_rows
    return pl.pallas_call(
        _attention_kernel,
        grid=(n_streams, tiles_per_stream),
        in_specs=[
            pl.BlockSpec((q_rows, D_MODEL), lambda b, t: (base + b * tiles_per_stream + t, 0)),
            pl.BlockSpec((1, N_MEM, D_MODEL), lambda b, t: (b, 0, 0)),
            pl.BlockSpec((1, N_MEM, D_MODEL), lambda b, t: (b, 0, 0)),
        ],
        out_specs=pl.BlockSpec((q_rows, D_MODEL), lambda b, t: (b * tiles_per_stream + t, 0)),
        out_shape=jax.ShapeDtypeStruct((n_streams * t_len, D_MODEL), BF16),
        compiler_params=pltpu.CompilerParams(dimension_semantics=("arbitrary", "arbitrary"),
                                             vmem_limit_bytes=VMEM_LIMIT),
        name=name,
    )(q_all, mem_k, mem_v)


_GROUP_LANE0 = 0
_EXPERT_LANE0 = N_GROUPS
(_META_LANE1, _META_LANE2, _META_POS1, _META_POS2, _META_GATE1, _META_GATE2) = range(6)
(_OUT_SLOT1, _OUT_SLOT2, _OUT_GATE1, _OUT_GATE2) = range(4)
_MOE_TILE_LOG2 = MOE_TILE.bit_length() - 1
assert 1 << _MOE_TILE_LOG2 == MOE_TILE


def _router_kernel(x_ref, wh_ref, wl_ref, out_ref, counts_ref, carry_ref, meta_ref, start_ref):
    phase = pl.program_id(0)
    i = pl.program_id(1)
    rows = x_ref.shape[0]
    tile_rows = pl.ds(pl.multiple_of(i * rows, rows), rows)
    lane = lax.broadcasted_iota(jnp.int32, (rows, LANES), 1)

    @pl.when((phase == 0) & (i == 0))
    def _():
        carry_ref[...] = jnp.zeros_like(carry_ref)

    @pl.when(phase == 0)
    def _route():
        x = x_ref[...]
        xh = x.astype(BF16)
        xl = (x - xh.astype(F32)).astype(BF16)
        logits = (jnp.dot(xh, wh_ref[...], preferred_element_type=F32)
                  + jnp.dot(xl, wh_ref[...], preferred_element_type=F32)
                  + jnp.dot(xh, wl_ref[...], preferred_element_type=F32))

        def first_argmax(vals):
            m = jnp.max(vals, axis=-1, keepdims=True)
            idx = jnp.min(jnp.where(vals == m, lane, LANES), axis=-1, keepdims=True)
            return m, idx

        gl = jnp.where(lane < N_GROUPS, logits, _NEG_INF)
        gm, g_idx = first_argmax(gl)
        g_w = 1.0 / jnp.sum(jnp.exp(gl - gm), axis=-1, keepdims=True)

        in_group = ((lane >= _EXPERT_LANE0) & (lane < _EXPERT_LANE0 + N_EXPERTS)
                    & (((lane - _EXPERT_LANE0) >> 3) == g_idx))
        el = jnp.where(in_group, logits, _NEG_INF)
        m1, i1 = first_argmax(el)
        z = jnp.sum(jnp.exp(el - m1), axis=-1, keepdims=True)
        m2, i2 = first_argmax(jnp.where(lane == i1, _NEG_INF, el))
        p1 = 1.0 / z
        p2 = jnp.exp(m2 - m1) / z
        den = p1 + p2
        gate1 = p1 / den * g_w
        gate2 = p2 / den * g_w

        hit1 = lane == i1
        hit2 = lane == i2
        onehot = (hit1 | hit2).astype(BF16)
        earlier = (lax.broadcasted_iota(jnp.int32, (rows, rows), 1)
                   < lax.broadcasted_iota(jnp.int32, (rows, rows), 0)).astype(BF16)
        rank = jnp.dot(earlier, onehot, preferred_element_type=F32) + carry_ref[...]
        pos1 = jnp.sum(jnp.where(hit1, rank, 0.0), axis=-1, keepdims=True)
        pos2 = jnp.sum(jnp.where(hit2, rank, 0.0), axis=-1, keepdims=True)
        carry_ref[...] = carry_ref[...] + jnp.sum(onehot.astype(F32), axis=0, keepdims=True)

        meta = jnp.zeros((rows, LANES), F32)
        for col, val in ((_META_LANE1, i1.astype(F32)), (_META_LANE2, i2.astype(F32)), (_META_POS1, pos1),
                         (_META_POS2, pos2), (_META_GATE1, gate1), (_META_GATE2, gate2)):
            meta = jnp.where(lane == col, val, meta)
        meta_ref[tile_rows, :] = meta

    @pl.when((phase == 1) & (i == 0))
    def _segment_starts():
        counts_ref[...] = carry_ref[...]
        tiles = ((carry_ref[...].astype(jnp.int32) + (MOE_TILE - 1)) >> _MOE_TILE_LOG2).astype(F32).astype(BF16)
        before = (lax.broadcasted_iota(jnp.int32, (LANES, LANES), 0)
                  < lax.broadcasted_iota(jnp.int32, (LANES, LANES), 1)).astype(BF16)
        tiles8 = jnp.broadcast_to(tiles, (8, LANES))
        start_ref[...] = jnp.dot(tiles8, before, preferred_element_type=F32)[0:1, :] * float(MOE_TILE)

    @pl.when(phase == 1)
    def _slots():
        meta = meta_ref[tile_rows, :]
        starts = start_ref[...]

        def col(c):
            return jnp.sum(jnp.where(lane == c, meta, 0.0), axis=-1, keepdims=True)

        def start_of(expert_lane):
            return jnp.sum(jnp.where(lane == expert_lane.astype(jnp.int32), starts, 0.0), axis=-1, keepdims=True)

        slot1 = start_of(col(_META_LANE1)) + col(_META_POS1)
        slot2 = start_of(col(_META_LANE2)) + col(_META_POS2)
        out = jnp.zeros((rows, LANES), F32)
        for c, val in ((_OUT_SLOT1, slot1), (_OUT_SLOT2, slot2), (_OUT_GATE1, col(_META_GATE1)),
                       (_OUT_GATE2, col(_META_GATE2))):
            out = jnp.where(lane == c, val, out)
        out_ref[...] = out


def _router(x_all, w_router):
    n_rows = x_all.shape[0]
    n_tiles = n_rows // ROW_TILE
    w_hi = w_router.astype(BF16)
    w_lo = (w_router - w_hi.astype(F32)).astype(BF16)
    return pl.pallas_call(
        _router_kernel,
        grid=(2, n_tiles),
        in_specs=[pl.BlockSpec((ROW_TILE, D_MODEL), lambda p, i: (i * (1 - p) + (n_tiles - 1) * p, 0)),
                  _const_spec((D_MODEL, LANES)), _const_spec((D_MODEL, LANES))],
        out_specs=[pl.BlockSpec((ROW_TILE, LANES), lambda p, i: (i * p, 0)),
                   pl.BlockSpec((1, LANES), lambda p, i: (0, 0))],
        out_shape=[jax.ShapeDtypeStruct((n_rows, LANES), F32), jax.ShapeDtypeStruct((1, LANES), F32)],
        scratch_shapes=[pltpu.VMEM((1, LANES), F32), pltpu.VMEM((n_rows, LANES), F32), pltpu.VMEM((1, LANES), F32)],
        compiler_params=pltpu.CompilerParams(dimension_semantics=("arbitrary", "arbitrary"),
                                             vmem_limit_bytes=VMEM_LIMIT),
        name="router",
    )(x_all, w_hi, w_lo)


def _experts_kernel(tile_start_ref, n_active_ref, slots_ref, xg_hbm, w1_ref, w3_ref, w2_ref, y_hbm,
                    xbuf, ybuf, w1b, w3b, w2b, src_ref, gsem, ysem, *, n_tiles_max):
    e = pl.program_id(0)
    n_active = n_active_ref[0]
    tile_words = MOE_TILE * TOKEN_TILE_ROWS

    def gather(tile, buf):
        base = tile * MOE_TILE
        for r in range(MOE_TILE):
            tok = src_ref[base + r]
            pltpu.make_async_copy(xg_hbm.at[pl.ds(pl.multiple_of(tok * TOKEN_TILE_ROWS, TOKEN_TILE_ROWS),
                                                  TOKEN_TILE_ROWS), :],
                                  xbuf.at[buf, pl.ds(r * TOKEN_TILE_ROWS, TOKEN_TILE_ROWS), :],
                                  gsem.at[buf]).start(priority=r % 2)

    def gather_wait(buf):
        pltpu.make_async_copy(xg_hbm.at[pl.ds(0, tile_words), :], xbuf.at[buf], gsem.at[buf]).wait()

    def y_copy(tile, buf):
        rows = pl.ds(pl.multiple_of(tile * tile_words, tile_words), tile_words)
        return pltpu.make_async_copy(ybuf.at[buf], y_hbm.at[rows, :], ysem.at[buf])

    @pl.when(e == 0)
    def _plan():
        def clear(p, carry):
            src_ref[p] = 0
            return carry
        lax.fori_loop(0, src_ref.shape[0], clear, 0, unroll=16)

        def put(n, carry):
            src_ref[slots_ref[n]] = n >> 1
            return carry
        lax.fori_loop(0, slots_ref.shape[0], put, 0, unroll=16)
        gather(0, 0)

    w1b[...] = w1_ref[0, 0, 0].astype(BF16)
    w3b[...] = w3_ref[0, 0, 0].astype(BF16)
    w2b[...] = w2_ref[0, 0, 0].astype(BF16)
    first_tile = tile_start_ref[e]

    def tile_body(t, carry):
        g = first_tile + t
        buf = g % 2
        gather_wait(buf)

        @pl.when(g >= 2)
        def _():
            y_copy(g, buf).wait()

        gather(g + 1, 1 - buf)
        x = _from_token_tiles(xbuf[buf])
        a = jnp.dot(x, w1b[...], preferred_element_type=F32)
        b = jnp.dot(x, w3b[...], preferred_element_type=F32)
        hdn = (_silu(a) * b).astype(BF16)
        y = jnp.dot(hdn, w2b[...], preferred_element_type=F32)
        ybuf[buf] = _to_token_tiles(y)
        y_copy(g, buf).start()
        return carry

    lax.fori_loop(0, tile_start_ref[e + 1] - first_tile, tile_body, 0)

    @pl.when(e == N_EXPERTS - 1)
    def _drain():
        gather_wait(n_active % 2)

        @pl.when(n_active >= 2)
        def _():
            y_copy(0, n_active % 2).wait()

        @pl.when(n_active >= 1)
        def _():
            y_copy(0, (n_active + 1) % 2).wait()

        ybuf[0] = jnp.zeros(ybuf.shape[1:], ybuf.dtype)

        def fill(g, carry):
            cp = y_copy(g, 0)
            cp.start()
            cp.wait()
            return carry
        lax.fori_loop(n_active, n_tiles_max, fill, 0)


def _experts(tile_start, n_active, slots, xg, w1, w3, w2, n_tiles_max):
    def expert_block(e, ts, na, sl):
        return (0, e // EXPERTS_PER_GROUP, e % EXPERTS_PER_GROUP, 0, 0)

    tile_words = MOE_TILE * TOKEN_TILE_ROWS
    kern = functools.partial(_experts_kernel, n_tiles_max=n_tiles_max)
    return pl.pallas_call(
        kern,
        grid_spec=pltpu.PrefetchScalarGridSpec(
            num_scalar_prefetch=3,
            grid=(N_EXPERTS,),
            in_specs=[
                pl.BlockSpec(memory_space=pl.ANY),
                pl.BlockSpec((1, 1, 1, D_MODEL, EXPERT_HIDDEN), expert_block),
                pl.BlockSpec((1, 1, 1, D_MODEL, EXPERT_HIDDEN), expert_block),
                pl.BlockSpec((1, 1, 1, EXPERT_HIDDEN, D_MODEL), expert_block),
            ],
            out_specs=pl.BlockSpec(memory_space=pl.ANY),
            scratch_shapes=[
                pltpu.VMEM((2, tile_words, LANES), BF16),
                pltpu.VMEM((2, tile_words, LANES), BF16),
                pltpu.VMEM((D_MODEL, EXPERT_HIDDEN), BF16),
                pltpu.VMEM((D_MODEL, EXPERT_HIDDEN), BF16),
                pltpu.VMEM((EXPERT_HIDDEN, D_MODEL), BF16),
                pltpu.SMEM(((n_tiles_max + 1) * MOE_TILE,), jnp.int32),
                pltpu.SemaphoreType.DMA((2,)),
                pltpu.SemaphoreType.DMA((2,)),
            ],
        ),
        out_shape=jax.ShapeDtypeStruct((n_tiles_max * tile_words, LANES), BF16),
        compiler_params=pltpu.CompilerParams(dimension_semantics=("arbitrary",),
                                             vmem_limit_bytes=VMEM_LIMIT),
        name="experts",
    )(tile_start, n_active, slots, xg, w1, w3, w2)


def _combine_kernel(slot_ref, y_hbm, gates_ref, x_ref, g_ref, b_ref, op_ref, os_ref, ybuf, sem,
                    *, n_tiles, n_tiles_first):
    i = pl.program_id(0)

    tile_words = COMBINE_TILE * TOKEN_TILE_ROWS

    def gather(tile, buf):
        for r in range(COMBINE_TILE):
            tok = tile * COMBINE_TILE + r
            for k in range(2):
                row0 = pl.multiple_of(slot_ref[2 * tok + k] * TOKEN_TILE_ROWS, TOKEN_TILE_ROWS)
                pltpu.make_async_copy(y_hbm.at[pl.ds(row0, TOKEN_TILE_ROWS), :],
                                      ybuf.at[buf, k, pl.ds(r * TOKEN_TILE_ROWS, TOKEN_TILE_ROWS), :],
                                      sem.at[buf]).start(priority=k)

    buf = i % 2

    def wait_all():
        for k in range(2):
            pltpu.make_async_copy(y_hbm.at[pl.ds(0, tile_words), :], ybuf.at[buf, k], sem.at[buf]).wait()

    def compute():
        y = (gates_ref[:, 0:1] * _from_token_tiles(ybuf[buf, 0]).astype(F32)
             + gates_ref[:, 1:2] * _from_token_tiles(ybuf[buf, 1]).astype(F32))
        out = _layer_norm(ALPHA * x_ref[...] + y, g_ref[...], b_ref[...])

        @pl.when(i < n_tiles_first)
        def _():
            op_ref[...] = out

        @pl.when(i >= n_tiles_first)
        def _():
            os_ref[...] = out

    @pl.when(i == 0)
    def _():
        gather(0, 0)

    @pl.when(i + 1 < n_tiles)
    def _steady():
        wait_all()
        gather(i + 1, 1 - buf)
        compute()

    @pl.when(i + 1 == n_tiles)
    def _last():
        wait_all()
        compute()


def _combine(slots, y_sorted, gates, x_all, g, b, n_rows_first):
    n_rows = x_all.shape[0]
    n_tiles = n_rows // COMBINE_TILE
    n_first = n_rows_first // COMBINE_TILE
    kern = functools.partial(_combine_kernel, n_tiles=n_tiles, n_tiles_first=n_first)
    return pl.pallas_call(
        kern,
        grid_spec=pltpu.PrefetchScalarGridSpec(
            num_scalar_prefetch=1,
            grid=(n_tiles,),
            in_specs=[
                pl.BlockSpec(memory_space=pl.ANY),
                pl.BlockSpec((COMBINE_TILE, 2), lambda i, sl: (i, 0)),
                pl.BlockSpec((COMBINE_TILE, D_MODEL), lambda i, sl: (i, 0)),
                pl.BlockSpec((1, D_MODEL), lambda i, sl: (0, 0)),
                pl.BlockSpec((1, D_MODEL), lambda i, sl: (0, 0)),
            ],
            out_specs=[
                pl.BlockSpec((COMBINE_TILE, D_MODEL), lambda i, sl: (jnp.minimum(i, n_first - 1), 0)),
                pl.BlockSpec((COMBINE_TILE, D_MODEL), lambda i, sl: (jnp.maximum(i - n_first, 0), 0)),
            ],
            scratch_shapes=[pltpu.VMEM((2, 2, COMBINE_TILE * TOKEN_TILE_ROWS, LANES), BF16),
                            pltpu.SemaphoreType.DMA((2,))],
        ),
        out_shape=[jax.ShapeDtypeStruct((n_rows_first, D_MODEL), F32),
                   jax.ShapeDtypeStruct((n_rows - n_rows_first, D_MODEL), F32)],
        compiler_params=pltpu.CompilerParams(dimension_semantics=("arbitrary",),
                                             vmem_limit_bytes=VMEM_LIMIT),
        name="combine",
    )(slots, y_sorted, gates, x_all, g, b)


def _dispatch_plan(routed, counts):
    slots = routed[:, _OUT_SLOT1:_OUT_SLOT2 + 1].astype(jnp.int32).reshape(-1)
    gates = routed[:, _OUT_GATE1:_OUT_GATE2 + 1]
    cnt = counts[0, _EXPERT_LANE0:_EXPERT_LANE0 + N_EXPERTS].astype(jnp.int32)
    tile_end = jnp.cumsum((cnt + MOE_TILE - 1) // MOE_TILE)
    tile_start = jnp.concatenate([jnp.zeros((1,), jnp.int32), tile_end]).astype(jnp.int32)
    return tile_start, tile_end[-1:].astype(jnp.int32), slots, gates


def kernel(x_prompt, x_sample, mem_prompt, state_pool, state_ret, cache_mem_k, cache_mem_v, w_in, w_pool, b_pool,
           pool_scale, ret_gn_g, ret_gn_b, w_out, ln1_g, ln1_b, w_mq, w_mk, w_mv, w_mo, ln2_g, ln2_b, w_rg, w_re,
           w1, w3, w2, ln3_g, ln3_b):
    assert w_in.shape[0] == DEPTH == 1
    bp_n, tp, _ = x_prompt.shape
    bs_n, ts, _ = x_sample.shape
    rows_p, rows_s = bp_n * tp, bs_n * ts
    assert rows_s == ROW_TILE and rows_p % ROW_TILE == 0
    n_rows = rows_p + rows_s
    tiles_p = rows_p // ROW_TILE

    w_in_b = w_in[0].astype(BF16)
    w_out_b = w_out[0].astype(BF16)
    w_mq_b = w_mq[0].astype(BF16)
    w_mk_b = w_mk[0].astype(BF16)
    w_mv_b = w_mv[0].astype(BF16)
    w_mo_b = w_mo[0].astype(BF16)
    wp_b = w_pool[0].astype(BF16)
    bp = b_pool[0].reshape(1, POOL_WIDTH)
    ps = pool_scale[0].reshape(1, POOL_WIDTH)
    gng = ret_gn_g[0].reshape(1, RET_WIDTH)
    gnb = ret_gn_b[0].reshape(1, RET_WIDTH)
    row = lambda p: p[0].reshape(1, D_MODEL)

    xp2d = x_prompt.reshape(rows_p, D_MODEL)
    xs2d = x_sample.reshape(rows_s, D_MODEL)

    mem2d = mem_prompt.reshape(bp_n * N_MEM, D_MODEL)
    mk_p = _matmul(mem2d, w_mk_b, F32, "mem_k").reshape(bp_n, N_MEM, D_MODEL)
    mv_p = _matmul(mem2d, w_mv_b, F32, "mem_v").reshape(bp_n, N_MEM, D_MODEL)

    zeros_s = jnp.zeros((bp_n, RET_HEADS, RET_HEAD_DIM, RET_HEAD_DIM), F32)
    zeros_h = jnp.zeros((bp_n, POOL_HIST, POOL_WIDTH), F32)
    cat_p, ret_p, pool_p = _mixer(xp2d, w_in_b, wp_b, bp, ps, gng, gnb, zeros_s, zeros_h,
                                  n_streams=bp_n, t_len=tp, tile_rows=MIXER_TILE, chunk=MIXER_TILE, pos0=0)
    cat_s, ret_s, pool_s = _mixer(xs2d, w_in_b, wp_b, bp, ps, gng, gnb, state_ret[0], state_pool[0],
                                  n_streams=bs_n, t_len=ts, tile_rows=rows_s, chunk=ts, pos0=PAST_LEN)
    (x1,) = _mm_ln([cat_p, cat_s], w_out_b, [xp2d, xs2d], row(ln1_g), row(ln1_b),
                   n_rows=n_rows, n_tiles_first=tiles_p, name="out_ln1")

    q_all = _matmul(x1, w_mq_b, BF16, "mem_q")
    o_p = _attention(q_all, mk_p, mv_p, n_streams=bp_n, t_len=tp, q_rows=ROW_TILE, row_offset=0, name="attn_prompt")
    o_s = _attention(q_all, cache_mem_k[0].reshape(bs_n, N_MEM, D_MODEL), cache_mem_v[0].reshape(bs_n, N_MEM, D_MODEL),
                     n_streams=bs_n, t_len=ts, q_rows=ts, row_offset=rows_p, name="attn_sample")
    x2, x2_tiles = _mm_ln([o_p, o_s], w_mo_b, [x1], row(ln2_g), row(ln2_b),
                          n_rows=n_rows, n_tiles_first=tiles_p, name="mo_ln2", emit_packed=True)

    w_router = jnp.concatenate([w_rg[0], w_re[0].reshape(D_MODEL, N_EXPERTS),
                                jnp.zeros((D_MODEL, LANES - N_GROUPS - N_EXPERTS), F32)], axis=1)
    routed, counts = _router(x2, w_router)
    n_tiles = (2 * n_rows) // MOE_TILE + N_EXPERTS
    tile_start, n_active, slots, gates = _dispatch_plan(routed, counts)
    y_sorted = _experts(tile_start, n_active, slots, x2_tiles, w1, w3, w2, n_tiles)
    y_p, y_s = _combine(slots, y_sorted, gates, x2, row(ln3_g), row(ln3_b), rows_p)

    kv_shape = (DEPTH, bp_n, N_MEM, MEM_HEADS, MEM_HEAD_DIM)
    return (y_p.reshape(bp_n, tp, D_MODEL), y_s.reshape(bs_n, ts, D_MODEL), pool_p[None], ret_p[None],
            mk_p.reshape(kv_shape), mv_p.reshape(kv_shape), pool_s[None], ret_s[None])
```

```python
import functools
import math

import jax
import jax.numpy as jnp
from jax import lax
from jax.experimental import pallas as pl
from jax.experimental.pallas import tpu as pltpu

F32 = jnp.float32
BF16 = jnp.bfloat16

D_MODEL = 2048
POOL_WIDTH = 1024
POOL_WINDOWS = (2, 4, 8, 16)
POOL_CH = 256
POOL_HIST = 15
RET_WIDTH = 1024
RET_HEADS = 8
RET_HEAD_DIM = 128
IN_WIDTH = POOL_WIDTH + 4 * RET_WIDTH
ROPE_BASE = 10000.0
N_MEM = 256
MEM_HEADS = 4
MEM_HEAD_DIM = 512
N_GROUPS = 4
EXPERTS_PER_GROUP = 8
N_EXPERTS = N_GROUPS * EXPERTS_PER_GROUP
EXPERT_HIDDEN = 512
LN_EPS = 1e-5
GN_EPS = 1e-6
DEPTH = 1
ALPHA = (2.0 * DEPTH) ** 0.25
PAST_LEN = 2048
PROMPT_CHUNK = 64

LANES = 128
HIST_PAD = 16
ROW_TILE = 512
MIXER_TILE = 256
MOE_TILE = 256
COMBINE_TILE = 256
VMEM_LIMIT = 58 * 1024 * 1024

_NEG_INF = float("-inf")


def _const_spec(shape):
    zeros = (0,) * len(shape)
    return pl.BlockSpec(shape, lambda *_: zeros, pipeline_mode=pl.Buffered(1))


def _when(cond, fn):
    if cond is True:
        fn()
    else:
        pl.when(cond)(fn)


def _layer_norm(z, g, b):
    mu = jnp.mean(z, axis=-1, keepdims=True)
    zc = z - mu
    var = jnp.mean(zc * zc, axis=-1, keepdims=True)
    return zc * lax.rsqrt(var + LN_EPS) * g + b


def _silu(a):
    return a * (1.0 / (1.0 + jnp.exp(-a)))


TOKEN_TILE_ROWS = D_MODEL // LANES


def _to_token_tiles(x):
    rows = x.shape[0]
    return x.astype(BF16).reshape(rows, TOKEN_TILE_ROWS, LANES).reshape(rows * TOKEN_TILE_ROWS, LANES)


def _from_token_tiles(tiles):
    rows = tiles.shape[0] // TOKEN_TILE_ROWS
    return tiles.reshape(rows, TOKEN_TILE_ROWS, LANES).reshape(rows, D_MODEL)


def _mix_segment(h_ref, r0, cos_ref, sin_ref, dec_ref, kd_ref, qd_ref, wp_ref, bp_ref, ps_ref, gng_ref, gnb_ref,
                 s0_ref, h0_ref, cat_ref, snew_ref, hnew_ref, s_ref, u_ref,
                 *, seg_len, chunk, pos_start, first, g_chunk):
    hist = jnp.concatenate([jnp.zeros((1, POOL_WIDTH), F32), h0_ref[0]], axis=0)
    if first is True:
        s_ref[...] = s0_ref[0]
        u_ref[0:HIST_PAD, :] = hist
    else:
        s_ref[...] = jnp.where(first, s0_ref[0], s_ref[...])
        u_ref[0:HIST_PAD, :] = jnp.where(first, hist, u_ref[0:HIST_PAD, :])

    u_new = h_ref[pl.ds(r0, seg_len), 0:POOL_WIDTH]
    u_ref[HIST_PAD:HIST_PAD + seg_len, :] = u_new
    pos = (pos_start + lax.broadcasted_iota(jnp.int32, (seg_len, 1), 0)).astype(F32)
    for gi, w in enumerate(POOL_WINDOWS):
        cols = slice(gi * POOL_CH, (gi + 1) * POOL_CH)
        win = u_ref[HIST_PAD:HIST_PAD + seg_len, cols]
        for back in range(1, w):
            win = win + u_ref[HIST_PAD - back:HIST_PAD - back + seg_len, cols]
        cnt = jnp.minimum(float(w), pos + 1.0)
        d = win * (1.0 / cnt) - u_ref[HIST_PAD:HIST_PAD + seg_len, cols]
        pooled = jnp.dot(d.astype(BF16), wp_ref[gi], preferred_element_type=F32) + bp_ref[:, cols]
        cat_ref[:, cols] = (pooled * ps_ref[:, cols]).astype(BF16)

    hnew_ref[0] = u_ref[seg_len + 1:seg_len + HIST_PAD, :]
    u_ref[0:HIST_PAD, :] = u_ref[seg_len:seg_len + HIST_PAD, :]

    scale = RET_HEAD_DIM ** -0.5
    for c in range(seg_len // chunk):
        rows = pl.ds(r0 + c * chunk, chunk)
        trows = slice(c * chunk, (c + 1) * chunk)
        cos_t = cos_ref[trows, :]
        sin_t = sin_ref[trows, :]
        for hd in range(RET_HEADS):
            lo = hd * RET_HEAD_DIM
            hcols = slice(lo, lo + RET_HEAD_DIM)
            q = h_ref[rows, POOL_WIDTH + lo:POOL_WIDTH + lo + RET_HEAD_DIM]
            k = h_ref[rows, POOL_WIDTH + RET_WIDTH + lo:POOL_WIDTH + RET_WIDTH + lo + RET_HEAD_DIM]
            v = h_ref[rows, POOL_WIDTH + 2 * RET_WIDTH + lo:POOL_WIDTH + 2 * RET_WIDTH + lo + RET_HEAD_DIM]
            gate = h_ref[rows, POOL_WIDTH + 3 * RET_WIDTH + lo:POOL_WIDTH + 3 * RET_WIDTH + lo + RET_HEAD_DIM]
            qr = q * cos_t + pltpu.roll(q, RET_HEAD_DIM // 2, 1) * sin_t
            kr = (k * cos_t + pltpu.roll(k, RET_HEAD_DIM // 2, 1) * sin_t) * scale
            vb = v.astype(BF16)
            scores = lax.dot_general(qr.astype(BF16), kr.astype(BF16), (((1,), (1,)), ((), ())),
                                     preferred_element_type=F32) * dec_ref[hd]
            o = jnp.dot(scores.astype(BF16), vb, preferred_element_type=F32)
            s_prev = s_ref[hd]
            o = o + jnp.dot((qr * qd_ref[hd]).astype(BF16), s_prev.astype(BF16), preferred_element_type=F32)
            upd = lax.dot_general((kr * kd_ref[hd]).astype(BF16), vb, (((0,), (0,)), ((), ())),
                                  preferred_element_type=F32)
            s_ref[hd] = g_chunk[hd] * s_prev + upd
            mu = jnp.mean(o, axis=-1, keepdims=True)
            oc = o - mu
            var = jnp.mean(oc * oc, axis=-1, keepdims=True)
            on = oc * lax.rsqrt(var + GN_EPS) * gng_ref[:, hcols] + gnb_ref[:, hcols]
            cat_ref[trows, POOL_WIDTH + lo:POOL_WIDTH + lo + RET_HEAD_DIM] = (on * _silu(gate)).astype(BF16)

    snew_ref[0] = s_ref[...]


def _mixer_segments_kernel(x_ref, w_in_ref, *refs, seg_len, chunk, pos0, g_chunk):
    (*mix_refs, h_ref, s_ref, u_ref) = refs
    j = pl.program_id(0)

    @pl.when(j == 0)
    def _project():
        h_ref[...] = jnp.dot(x_ref[...].astype(BF16), w_in_ref[...], preferred_element_type=F32)

    _mix_segment(h_ref, pl.multiple_of(j * seg_len, seg_len), *mix_refs, s_ref, u_ref,
                 seg_len=seg_len, chunk=chunk, pos_start=pos0, first=True, g_chunk=g_chunk)


def _mixer_pipelined_kernel(x_ref, w_in_ref, *refs, tiles_per_stream, seg_len, chunk, pos0, g_chunk):
    (*mix_refs, ha_ref, hb_ref, s_ref, u_ref) = refs
    g = pl.program_id(0)
    m = jnp.maximum(g - 1, 0)
    first = ((m % tiles_per_stream) == 0) | (g == 0)
    pos_start = pos0 + (m % tiles_per_stream) * seg_len

    @pl.when(g == 0)
    def _():
        hb_ref[...] = jnp.zeros_like(hb_ref)

    def step(h_write, h_read):
        _mix_segment(h_read, 0, *mix_refs, s_ref, u_ref, seg_len=seg_len, chunk=chunk, pos_start=pos_start,
                     first=first, g_chunk=g_chunk)
        h_write[...] = jnp.dot(x_ref[...].astype(BF16), w_in_ref[...], preferred_element_type=F32)

    pl.when(g % 2 == 0)(lambda: step(ha_ref, hb_ref))
    pl.when(g % 2 == 1)(lambda: step(hb_ref, ha_ref))


def _retention_tables(chunk, t_len, pos0):
    log_gamma = jnp.log(1.0 - 2.0 ** (-5.0 - jnp.arange(RET_HEADS, dtype=F32)))
    idx = jnp.arange(chunk, dtype=F32)
    diff = idx[:, None] - idx[None, :]
    dec = jnp.where(diff >= 0, jnp.exp(log_gamma[:, None, None] * jnp.maximum(diff, 0.0)), 0.0)
    kd = jnp.exp(log_gamma[:, None] * (chunk - 1.0 - idx)[None, :])
    qd = jnp.exp(log_gamma[:, None] * (idx + 1.0)[None, :])
    kd = jnp.broadcast_to(kd[:, :, None], (RET_HEADS, chunk, RET_HEAD_DIM))
    qd = jnp.broadcast_to(qd[:, :, None], (RET_HEADS, chunk, RET_HEAD_DIM))
    half = RET_HEAD_DIM // 2
    freqs = ROPE_BASE ** (-jnp.arange(half, dtype=F32) / half)
    pos = pos0 + jnp.arange(t_len, dtype=F32)
    ang = pos[:, None] * freqs[None, :]
    cos = jnp.cos(ang)
    sin = jnp.sin(ang)
    cos_t = jnp.concatenate([cos, cos], axis=-1)
    sin_t = jnp.concatenate([-sin, sin], axis=-1)
    g_chunk = tuple(math.exp(math.log(1.0 - 2.0 ** (-5.0 - h)) * chunk) for h in range(RET_HEADS))
    return dec, kd, qd, cos_t, sin_t, g_chunk


def _mixer(x2d, w_in_b, wp_b, bp, ps, gng, gnb, s0, h0, *, n_streams, t_len, tile_rows, chunk, pos0):
    rows = n_streams * t_len
    pipelined = t_len > tile_rows
    seg_len = tile_rows if pipelined else t_len
    assert (t_len % tile_rows == 0) if pipelined else (rows == tile_rows)
    assert seg_len % chunk == 0 and t_len >= POOL_HIST
    dec, kd, qd, cos_t, sin_t, g_chunk = _retention_tables(chunk, t_len, pos0)
    state_block = (1, RET_HEADS, RET_HEAD_DIM, RET_HEAD_DIM)
    hist_block = (1, POOL_HIST, POOL_WIDTH)

    if pipelined:
        n_tiles = rows // tile_rows
        tiles_per_stream = t_len // tile_rows
        grid = (n_tiles + 1,)
        mixed = lambda g: jnp.maximum(g - 1, 0)
        x_map = lambda g: (jnp.minimum(g, n_tiles - 1), 0)
        time_map = lambda g: (mixed(g) % tiles_per_stream, 0)
        cat_map = lambda g: (mixed(g), 0)
        state_map = lambda g: (mixed(g) // tiles_per_stream, 0, 0, 0)
        hist_map = lambda g: (mixed(g) // tiles_per_stream, 0, 0)
        kern = functools.partial(_mixer_pipelined_kernel, tiles_per_stream=tiles_per_stream, seg_len=seg_len,
                                 chunk=chunk, pos0=pos0, g_chunk=g_chunk)
        h_scratch = [pltpu.VMEM((tile_rows, IN_WIDTH), F32), pltpu.VMEM((tile_rows, IN_WIDTH), F32)]
    else:
        grid = (n_streams,)
        x_map = lambda j: (0, 0)
        time_map = lambda j: (0, 0)
        cat_map = lambda j: (j, 0)
        state_map = lambda j: (j, 0, 0, 0)
        hist_map = lambda j: (j, 0, 0)
        kern = functools.partial(_mixer_segments_kernel, seg_len=seg_len, chunk=chunk, pos0=pos0, g_chunk=g_chunk)
        h_scratch = [pltpu.VMEM((tile_rows, IN_WIDTH), F32)]

    return pl.pallas_call(
        kern,
        grid=grid,
        in_specs=[
            pl.BlockSpec((tile_rows, D_MODEL), x_map),
            _const_spec((D_MODEL, IN_WIDTH)),
            pl.BlockSpec((seg_len, RET_HEAD_DIM), time_map),
            pl.BlockSpec((seg_len, RET_HEAD_DIM), time_map),
            _const_spec((RET_HEADS, chunk, chunk)),
            _const_spec((RET_HEADS, chunk, RET_HEAD_DIM)),
            _const_spec((RET_HEADS, chunk, RET_HEAD_DIM)),
            _const_spec((len(POOL_WINDOWS), POOL_CH, POOL_CH)),
            _const_spec((1, POOL_WIDTH)),
            _const_spec((1, POOL_WIDTH)),
            _const_spec((1, RET_WIDTH)),
            _const_spec((1, RET_WIDTH)),
            pl.BlockSpec(state_block, state_map),
            pl.BlockSpec(hist_block, hist_map),
        ],
        out_specs=[
            pl.BlockSpec((seg_len, D_MODEL), cat_map),
            pl.BlockSpec(state_block, state_map),
            pl.BlockSpec(hist_block, hist_map),
        ],
        out_shape=[
            jax.ShapeDtypeStruct((rows, D_MODEL), BF16),
            jax.ShapeDtypeStruct((n_streams, RET_HEADS, RET_HEAD_DIM, RET_HEAD_DIM), F32),
            jax.ShapeDtypeStruct((n_streams, POOL_HIST, POOL_WIDTH), F32),
        ],
        scratch_shapes=h_scratch + [
            pltpu.VMEM((RET_HEADS, RET_HEAD_DIM, RET_HEAD_DIM), F32),
            pltpu.VMEM((HIST_PAD + seg_len, POOL_WIDTH), F32),
        ],
        compiler_params=pltpu.CompilerParams(dimension_semantics=("arbitrary",),
                                             vmem_limit_bytes=VMEM_LIMIT),
        name="mixer",
    )(x2d, w_in_b, cos_t, sin_t, dec, kd, qd, wp_b, bp, ps, gng, gnb, s0, h0)


def _row_sources(arrays, n_tiles_first):
    if len(arrays) == 1:
        return [pl.BlockSpec((ROW_TILE, arrays[0].shape[1]), lambda i: (i, 0))]
    first, second = arrays
    return [
        pl.BlockSpec((ROW_TILE, first.shape[1]), lambda i: (jnp.minimum(i, n_tiles_first - 1), 0)),
        pl.BlockSpec((ROW_TILE, second.shape[1]), lambda i: (jnp.maximum(i - n_tiles_first, 0), 0)),
    ]


def _mm_ln_kernel(*refs, n_a, n_res, n_tiles_first, emit_packed):
    a_refs = refs[:n_a]
    w_ref = refs[n_a]
    res_refs = refs[n_a + 1:n_a + 1 + n_res]
    g_ref, b_ref, o_ref = refs[n_a + 1 + n_res:n_a + 4 + n_res]

    def body(a_ref, res_ref):
        half = ROW_TILE // 2
        for h in range(2):
            rows = slice(h * half, (h + 1) * half)
            acc = jnp.dot(a_ref[rows, :].astype(BF16), w_ref[...], preferred_element_type=F32)
            out = _layer_norm(ALPHA * res_ref[rows, :] + acc, g_ref[...], b_ref[...])
            o_ref[rows, :] = out
            if emit_packed:
                tile_rows = slice(h * half * TOKEN_TILE_ROWS, (h + 1) * half * TOKEN_TILE_ROWS)
                refs[-1][tile_rows, :] = _to_token_tiles(out)

    i = pl.program_id(0)
    pl.when(i < n_tiles_first)(lambda: body(a_refs[0], res_refs[0]))
    pl.when(i >= n_tiles_first)(lambda: body(a_refs[-1], res_refs[-1]))


def _mm_ln(a_arrays, w_b, res_arrays, g, b, *, n_rows, n_tiles_first, name, emit_packed=False):
    n_tiles = n_rows // ROW_TILE
    kern = functools.partial(_mm_ln_kernel, n_a=len(a_arrays), n_res=len(res_arrays),
                             n_tiles_first=n_tiles_first, emit_packed=emit_packed)
    out_specs = [pl.BlockSpec((ROW_TILE, D_MODEL), lambda i: (i, 0))]
    out_shape = [jax.ShapeDtypeStruct((n_rows, D_MODEL), F32)]
    if emit_packed:
        out_specs.append(pl.BlockSpec((ROW_TILE * TOKEN_TILE_ROWS, LANES), lambda i: (i, 0)))
        out_shape.append(jax.ShapeDtypeStruct((n_rows * TOKEN_TILE_ROWS, LANES), BF16))
    return pl.pallas_call(
        kern,
        grid=(n_tiles,),
        in_specs=(_row_sources(a_arrays, n_tiles_first) + [_const_spec(w_b.shape)]
                  + _row_sources(res_arrays, n_tiles_first)
                  + [_const_spec((1, D_MODEL)), _const_spec((1, D_MODEL))]),
        out_specs=out_specs,
        out_shape=out_shape,
        compiler_params=pltpu.CompilerParams(dimension_semantics=("arbitrary",),
                                             vmem_limit_bytes=VMEM_LIMIT),
        name=name,
    )(*a_arrays, w_b, *res_arrays, g, b)


def _matmul_kernel(a_ref, w_ref, o_ref):
    o_ref[...] = jnp.dot(a_ref[...].astype(BF16), w_ref[...], preferred_element_type=F32).astype(o_ref.dtype)


def _matmul(a, w_b, out_dtype, name):
    n_rows = a.shape[0]
    return pl.pallas_call(
        _matmul_kernel,
        grid=(n_rows // ROW_TILE,),
        in_specs=[pl.BlockSpec((ROW_TILE, a.shape[1]), lambda i: (i, 0)), _const_spec(w_b.shape)],
        out_specs=pl.BlockSpec((ROW_TILE, w_b.shape[1]), lambda i: (i, 0)),
        out_shape=jax.ShapeDtypeStruct((n_rows, w_b.shape[1]), out_dtype),
        compiler_params=pltpu.CompilerParams(dimension_semantics=("arbitrary",),
                                             vmem_limit_bytes=VMEM_LIMIT),
        name=name,
    )(a, w_b)


def _attention_kernel(q_ref, k_ref, v_ref, o_ref):
    scale = MEM_HEAD_DIM ** -0.5
    for h in range(MEM_HEADS):
        cols = slice(h * MEM_HEAD_DIM, (h + 1) * MEM_HEAD_DIM)
        s = lax.dot_general(q_ref[:, cols], k_ref[0, :, cols].astype(BF16), (((1,), (1,)), ((), ())),
                            preferred_element_type=F32) * scale
        m = jnp.max(s, axis=-1, keepdims=True)
        p = jnp.exp(s - m)
        p = p * (1.0 / jnp.sum(p, axis=-1, keepdims=True))
        o_ref[:, cols] = jnp.dot(p.astype(BF16), v_ref[0, :, cols].astype(BF16),
                                 preferred_element_type=F32).astype(o_ref.dtype)


def _attention(q_all, mem_k, mem_v, *, n_streams, t_len, q_rows, row_offset, name):
    tiles_per_stream = t_len // q_rows
    base = row_offset // q_rows
    return pl.pallas_call(
        _attention_kernel,
        grid=(n_streams, tiles_per_stream),
        in_specs=[
            pl.BlockSpec((q_rows, D_MODEL), lambda b, t: (base + b * tiles_per_stream + t, 0)),
            pl.BlockSpec((1, N_MEM, D_MODEL), lambda b, t: (b, 0, 0)),
            pl.BlockSpec((1, N_MEM, D_MODEL), lambda b, t: (b, 0, 0)),
        ],
        out_specs=pl.BlockSpec((q_rows, D_MODEL), lambda b, t: (b * tiles_per_stream + t, 0)),
        out_shape=jax.ShapeDtypeStruct((n_streams * t_len, D_MODEL), BF16),
        compiler_params=pltpu.CompilerParams(dimension_semantics=("arbitrary", "arbitrary"),
                                             vmem_limit_bytes=VMEM_LIMIT),
        name=name,
    )(q_all, mem_k, mem_v)


_GROUP_LANE0 = 0
_EXPERT_LANE0 = N_GROUPS
(_META_LANE1, _META_LANE2, _META_POS1, _META_POS2, _META_GATE1, _META_GATE2) = range(6)
(_OUT_SLOT1, _OUT_SLOT2, _OUT_GATE1, _OUT_GATE2) = range(4)
_MOE_TILE_LOG2 = MOE_TILE.bit_length() - 1
assert 1 << _MOE_TILE_LOG2 == MOE_TILE


def _router_kernel(x_ref, wh_ref, wl_ref, out_ref, counts_ref, carry_ref, meta_ref, start_ref):
    phase = pl.program_id(0)
    i = pl.program_id(1)
    rows = x_ref.shape[0]
    tile_rows = pl.ds(pl.multiple_of(i * rows, rows), rows)
    lane = lax.broadcasted_iota(jnp.int32, (rows, LANES), 1)

    @pl.when((phase == 0) & (i == 0))
    def _():
        carry_ref[...] = jnp.zeros_like(carry_ref)

    @pl.when(phase == 0)
    def _route():
        x = x_ref[...]
        xh = x.astype(BF16)
        xl = (x - xh.astype(F32)).astype(BF16)
        logits = (jnp.dot(xh, wh_ref[...], preferred_element_type=F32)
                  + jnp.dot(xl, wh_ref[...], preferred_element_type=F32)
                  + jnp.dot(xh, wl_ref[...], preferred_element_type=F32))

        def first_argmax(vals):
            m = jnp.max(vals, axis=-1, keepdims=True)
            idx = jnp.min(jnp.where(vals == m, lane, LANES), axis=-1, keepdims=True)
            return m, idx

        gl = jnp.where(lane < N_GROUPS, logits, _NEG_INF)
        gm, g_idx = first_argmax(gl)
        g_w = 1.0 / jnp.sum(jnp.exp(gl - gm), axis=-1, keepdims=True)

        in_group = ((lane >= _EXPERT_LANE0) & (lane < _EXPERT_LANE0 + N_EXPERTS)
                    & (((lane - _EXPERT_LANE0) >> 3) == g_idx))
        el = jnp.where(in_group, logits, _NEG_INF)
        m1, i1 = first_argmax(el)
        z = jnp.sum(jnp.exp(el - m1), axis=-1, keepdims=True)
        m2, i2 = first_argmax(jnp.where(lane == i1, _NEG_INF, el))
        p1 = 1.0 / z
        p2 = jnp.exp(m2 - m1) / z
        den = p1 + p2
        gate1 = p1 / den * g_w
        gate2 = p2 / den * g_w

        hit1 = lane == i1
        hit2 = lane == i2
        onehot = (hit1 | hit2).astype(BF16)
        earlier = (lax.broadcasted_iota(jnp.int32, (rows, rows), 1)
                   < lax.broadcasted_iota(jnp.int32, (rows, rows), 0)).astype(BF16)
        rank = jnp.dot(earlier, onehot, preferred_element_type=F32) + carry_ref[...]
        pos1 = jnp.sum(jnp.where(hit1, rank, 0.0), axis=-1, keepdims=True)
        pos2 = jnp.sum(jnp.where(hit2, rank, 0.0), axis=-1, keepdims=True)
        carry_ref[...] = carry_ref[...] + jnp.sum(onehot.astype(F32), axis=0, keepdims=True)

        meta = jnp.zeros((rows, LANES), F32)
        for col, val in ((_META_LANE1, i1.astype(F32)), (_META_LANE2, i2.astype(F32)), (_META_POS1, pos1),
                         (_META_POS2, pos2), (_META_GATE1, gate1), (_META_GATE2, gate2)):
            meta = jnp.where(lane == col, val, meta)
        meta_ref[tile_rows, :] = meta

    @pl.when((phase == 1) & (i == 0))
    def _segment_starts():
        counts_ref[...] = carry_ref[...]
        tiles = ((carry_ref[...].astype(jnp.int32) + (MOE_TILE - 1)) >> _MOE_TILE_LOG2).astype(F32).astype(BF16)
        before = (lax.broadcasted_iota(jnp.int32, (LANES, LANES), 0)
                  < lax.broadcasted_iota(jnp.int32, (LANES, LANES), 1)).astype(BF16)
        tiles8 = jnp.broadcast_to(tiles, (8, LANES))
        start_ref[...] = jnp.dot(tiles8, before, preferred_element_type=F32)[0:1, :] * float(MOE_TILE)

    @pl.when(phase == 1)
    def _slots():
        meta = meta_ref[tile_rows, :]
        starts = start_ref[...]

        def col(c):
            return jnp.sum(jnp.where(lane == c, meta, 0.0), axis=-1, keepdims=True)

        def start_of(expert_lane):
            return jnp.sum(jnp.where(lane == expert_lane.astype(jnp.int32), starts, 0.0), axis=-1, keepdims=True)

        slot1 = start_of(col(_META_LANE1)) + col(_META_POS1)
        slot2 = start_of(col(_META_LANE2)) + col(_META_POS2)
        out = jnp.zeros((rows, LANES), F32)
        for c, val in ((_OUT_SLOT1, slot1), (_OUT_SLOT2, slot2), (_OUT_GATE1, col(_META_GATE1)),
                       (_OUT_GATE2, col(_META_GATE2))):
            out = jnp.where(lane == c, val, out)
        out_ref[...] = out


def _router(x_all, w_router):
    n_rows = x_all.shape[0]
    n_tiles = n_rows // ROW_TILE
    w_hi = w_router.astype(BF16)
    w_lo = (w_router - w_hi.astype(F32)).astype(BF16)
    return pl.pallas_call(
        _router_kernel,
        grid=(2, n_tiles),
        in_specs=[pl.BlockSpec((ROW_TILE, D_MODEL), lambda p, i: (i * (1 - p) + (n_tiles - 1) * p, 0)),
                  _const_spec((D_MODEL, LANES)), _const_spec((D_MODEL, LANES))],
        out_specs=[pl.BlockSpec((ROW_TILE, LANES), lambda p, i: (i * p, 0)),
                   pl.BlockSpec((1, LANES), lambda p, i: (0, 0))],
        out_shape=[jax.ShapeDtypeStruct((n_rows, LANES), F32), jax.ShapeDtypeStruct((1, LANES), F32)],
        scratch_shapes=[pltpu.VMEM((1, LANES), F32), pltpu.VMEM((n_rows, LANES), F32), pltpu.VMEM((1, LANES), F32)],
        compiler_params=pltpu.CompilerParams(dimension_semantics=("arbitrary", "arbitrary"),
                                             vmem_limit_bytes=VMEM_LIMIT),
        name="router",
    )(x_all, w_hi, w_lo)


def _token_rows(first_token, n_tokens):
    return pl.ds(pl.multiple_of(first_token * TOKEN_TILE_ROWS, TOKEN_TILE_ROWS), n_tokens * TOKEN_TILE_ROWS)


_TAIL_BITS = MOE_TILE.bit_length() - 1


def _dispatch_kernel(slots_ref, tail_start_ref, tail_len_ref, x_ref, xs_hbm, zeros_ref, sem, zsem):
    i = pl.program_id(0)
    n_tokens = x_ref.shape[0] // TOKEN_TILE_ROWS
    group = 16

    def issue(j, carry):
        for u in range(group):
            r = j * group + u
            src = x_ref.at[pl.ds(pl.multiple_of(r * TOKEN_TILE_ROWS, TOKEN_TILE_ROWS), TOKEN_TILE_ROWS), :]
            for k in range(2):
                slot = slots_ref[2 * (i * n_tokens + r) + k]
                pltpu.make_async_copy(src, xs_hbm.at[_token_rows(slot, 1), :], sem).start(priority=k)
        return carry
    lax.fori_loop(0, n_tokens // group, issue, 0)

    def drain(j, carry):
        pltpu.make_async_copy(x_ref.at[pl.ds(0, 2 * group * TOKEN_TILE_ROWS), :],
                              xs_hbm.at[_token_rows(0, 2 * group), :], sem).wait()
        return carry
    lax.fori_loop(0, n_tokens // group, drain, 0)

    def tail_copies(fn):
        def per_expert(ex, carry):
            length = tail_len_ref[ex]
            pos = tail_start_ref[ex]
            for bit in reversed(range(_TAIL_BITS)):
                piece = 1 << bit
                take = (length & piece) != 0
                cp = pltpu.make_async_copy(zeros_ref.at[pl.ds(0, piece * TOKEN_TILE_ROWS), :],
                                           xs_hbm.at[_token_rows(pos, piece), :], zsem)
                pl.when(take)(lambda cp=cp: fn(cp))
                pos = pos + jnp.where(take, piece, 0)
            return carry
        lax.fori_loop(0, N_EXPERTS, per_expert, 0)

    def unused_copies(fn):
        half = zeros_ref.shape[0] // TOKEN_TILE_ROWS
        first_unused = tail_start_ref[N_EXPERTS - 1] + tail_len_ref[N_EXPERTS - 1]

        def per_half(j, carry):
            fn(pltpu.make_async_copy(zeros_ref, xs_hbm.at[_token_rows(first_unused + j * half, half), :], zsem))
            return carry
        lax.fori_loop(0, (xs_hbm.shape[0] // TOKEN_TILE_ROWS - first_unused) // half, per_half, 0)

    @pl.when(i == pl.num_programs(0) - 1)
    def _tails():
        zeros_ref[...] = jnp.zeros_like(zeros_ref)
        tail_copies(lambda cp: cp.start())
        unused_copies(lambda cp: cp.start())
        tail_copies(lambda cp: cp.wait())
        unused_copies(lambda cp: cp.wait())


def _dispatch(slots, tail_start, tail_len, xg, n_tiles_max):
    block_rows = ROW_TILE * TOKEN_TILE_ROWS
    assert xg.shape[0] % block_rows == 0
    return pl.pallas_call(
        _dispatch_kernel,
        grid_spec=pltpu.PrefetchScalarGridSpec(
            num_scalar_prefetch=3,
            grid=(xg.shape[0] // block_rows,),
            in_specs=[pl.BlockSpec((block_rows, LANES), lambda i, sl, ts, tl: (i, 0))],
            out_specs=pl.BlockSpec(memory_space=pl.ANY),
            scratch_shapes=[pltpu.VMEM((MOE_TILE // 2 * TOKEN_TILE_ROWS, LANES), BF16),
                            pltpu.SemaphoreType.DMA(()), pltpu.SemaphoreType.DMA(())],
        ),
        out_shape=jax.ShapeDtypeStruct((n_tiles_max * MOE_TILE * TOKEN_TILE_ROWS, LANES), BF16),
        compiler_params=pltpu.CompilerParams(dimension_semantics=("arbitrary",)),
        name="dispatch",
    )(slots, tail_start, tail_len, xg)


def _experts_kernel(tile_start_ref, n_active_ref, xs_hbm, w1_ref, w3_ref, w2_ref, y_hbm,
                    xbuf, ybuf, w1b, w3b, w2b, gsem, ysem, *, n_tiles_max):
    e = pl.program_id(0)
    n_active = n_active_ref[0]

    def x_copy(tile, buf):
        return pltpu.make_async_copy(xs_hbm.at[_token_rows(tile * MOE_TILE, MOE_TILE), :], xbuf.at[buf], gsem.at[buf])

    def y_copy(tile, buf):
        return pltpu.make_async_copy(ybuf.at[buf], y_hbm.at[_token_rows(tile * MOE_TILE, MOE_TILE), :], ysem.at[buf])

    @pl.when(e == 0)
    def _():
        x_copy(0, 0).start()

    w1b[...] = w1_ref[0, 0, 0].astype(BF16)
    w3b[...] = w3_ref[0, 0, 0].astype(BF16)
    w2b[...] = w2_ref[0, 0, 0].astype(BF16)
    first_tile = tile_start_ref[e]

    def tile_body(t, carry):
        g = first_tile + t
        buf = g % 2
        x_copy(g, buf).wait()

        @pl.when(g >= 2)
        def _():
            y_copy(g, buf).wait()

        @pl.when(g + 1 < n_active)
        def _():
            x_copy(g + 1, 1 - buf).start()

        x = _from_token_tiles(xbuf[buf])
        a = jnp.dot(x, w1b[...], preferred_element_type=F32)
        b = jnp.dot(x, w3b[...], preferred_element_type=F32)
        hdn = (_silu(a) * b).astype(BF16)
        y = jnp.dot(hdn, w2b[...], preferred_element_type=F32)
        ybuf[buf] = _to_token_tiles(y)
        y_copy(g, buf).start()
        return carry

    lax.fori_loop(0, tile_start_ref[e + 1] - first_tile, tile_body, 0)

    @pl.when(e == N_EXPERTS - 1)
    def _drain():
        @pl.when(n_active >= 2)
        def _():
            y_copy(0, n_active % 2).wait()

        @pl.when(n_active >= 1)
        def _():
            y_copy(0, (n_active + 1) % 2).wait()

        ybuf[0] = jnp.zeros(ybuf.shape[1:], ybuf.dtype)

        def fill(g, carry):
            cp = y_copy(g, 0)
            cp.start()
            cp.wait()
            return carry
        lax.fori_loop(n_active, n_tiles_max, fill, 0)


def _experts(tile_start, n_active, x_sorted, w1, w3, w2, n_tiles_max):
    def expert_block(e, ts, na):
        return (0, e // EXPERTS_PER_GROUP, e % EXPERTS_PER_GROUP, 0, 0)

    tile_words = MOE_TILE * TOKEN_TILE_ROWS
    kern = functools.partial(_experts_kernel, n_tiles_max=n_tiles_max)
    return pl.pallas_call(
        kern,
        grid_spec=pltpu.PrefetchScalarGridSpec(
            num_scalar_prefetch=2,
            grid=(N_EXPERTS,),
            in_specs=[
                pl.BlockSpec(memory_space=pl.ANY),
                pl.BlockSpec((1, 1, 1, D_MODEL, EXPERT_HIDDEN), expert_block),
                pl.BlockSpec((1, 1, 1, D_MODEL, EXPERT_HIDDEN), expert_block),
                pl.BlockSpec((1, 1, 1, EXPERT_HIDDEN, D_MODEL), expert_block),
            ],
            out_specs=pl.BlockSpec(memory_space=pl.ANY),
            scratch_shapes=[
                pltpu.VMEM((2, tile_words, LANES), BF16),
                pltpu.VMEM((2, tile_words, LANES), BF16),
                pltpu.VMEM((D_MODEL, EXPERT_HIDDEN), BF16),
                pltpu.VMEM((D_MODEL, EXPERT_HIDDEN), BF16),
                pltpu.VMEM((EXPERT_HIDDEN, D_MODEL), BF16),
                pltpu.SemaphoreType.DMA((2,)),
                pltpu.SemaphoreType.DMA((2,)),
            ],
        ),
        out_shape=jax.ShapeDtypeStruct((n_tiles_max * tile_words, LANES), BF16),
        compiler_params=pltpu.CompilerParams(dimension_semantics=("arbitrary",),
                                             vmem_limit_bytes=VMEM_LIMIT),
        name="experts",
    )(tile_start, n_active, x_sorted, w1, w3, w2)


def _combine_kernel(slot_ref, y_hbm, gates_ref, x_ref, g_ref, b_ref, op_ref, os_ref, ybuf, sem,
                    *, n_tiles, n_tiles_first):
    i = pl.program_id(0)

    tile_words = COMBINE_TILE * TOKEN_TILE_ROWS

    def gather(tile, buf):
        for r in range(COMBINE_TILE):
            tok = tile * COMBINE_TILE + r
            for k in range(2):
                row0 = pl.multiple_of(slot_ref[2 * tok + k] * TOKEN_TILE_ROWS, TOKEN_TILE_ROWS)
                pltpu.make_async_copy(y_hbm.at[pl.ds(row0, TOKEN_TILE_ROWS), :],
                                      ybuf.at[buf, k, pl.ds(r * TOKEN_TILE_ROWS, TOKEN_TILE_ROWS), :],
                                      sem.at[buf]).start(priority=k)

    buf = i % 2

    def wait_all():
        for k in range(2):
            pltpu.make_async_copy(y_hbm.at[pl.ds(0, tile_words), :], ybuf.at[buf, k], sem.at[buf]).wait()

    def compute():
        y = (gates_ref[:, 0:1] * _from_token_tiles(ybuf[buf, 0]).astype(F32)
             + gates_ref[:, 1:2] * _from_token_tiles(ybuf[buf, 1]).astype(F32))
        out = _layer_norm(ALPHA * x_ref[...] + y, g_ref[...], b_ref[...])

        @pl.when(i < n_tiles_first)
        def _():
            op_ref[...] = out

        @pl.when(i >= n_tiles_first)
        def _():
            os_ref[...] = out

    @pl.when(i == 0)
    def _():
        gather(0, 0)

    @pl.when(i + 1 < n_tiles)
    def _steady():
        wait_all()
        gather(i + 1, 1 - buf)
        compute()

    @pl.when(i + 1 == n_tiles)
    def _last():
        wait_all()
        compute()


def _combine(slots, y_sorted, gates, x_all, g, b, n_rows_first):
    n_rows = x_all.shape[0]
    n_tiles = n_rows // COMBINE_TILE
    n_first = n_rows_first // COMBINE_TILE
    kern = functools.partial(_combine_kernel, n_tiles=n_tiles, n_tiles_first=n_first)
    return pl.pallas_call(
        kern,
        grid_spec=pltpu.PrefetchScalarGridSpec(
            num_scalar_prefetch=1,
            grid=(n_tiles,),
            in_specs=[
                pl.BlockSpec(memory_space=pl.ANY),
                pl.BlockSpec((COMBINE_TILE, 2), lambda i, sl: (i, 0)),
                pl.BlockSpec((COMBINE_TILE, D_MODEL), lambda i, sl: (i, 0)),
                pl.BlockSpec((1, D_MODEL), lambda i, sl: (0, 0)),
                pl.BlockSpec((1, D_MODEL), lambda i, sl: (0, 0)),
            ],
            out_specs=[
                pl.BlockSpec((COMBINE_TILE, D_MODEL), lambda i, sl: (jnp.minimum(i, n_first - 1), 0)),
                pl.BlockSpec((COMBINE_TILE, D_MODEL), lambda i, sl: (jnp.maximum(i - n_first, 0), 0)),
            ],
            scratch_shapes=[pltpu.VMEM((2, 2, COMBINE_TILE * TOKEN_TILE_ROWS, LANES), BF16),
                            pltpu.SemaphoreType.DMA((2,))],
        ),
        out_shape=[jax.ShapeDtypeStruct((n_rows_first, D_MODEL), F32),
                   jax.ShapeDtypeStruct((n_rows - n_rows_first, D_MODEL), F32)],
        compiler_params=pltpu.CompilerParams(dimension_semantics=("arbitrary",),
                                             vmem_limit_bytes=VMEM_LIMIT),
        name="combine",
    )(slots, y_sorted, gates, x_all, g, b)


def _dispatch_plan(routed, counts):
    slots = routed[:, _OUT_SLOT1:_OUT_SLOT2 + 1].astype(jnp.int32).reshape(-1)
    gates = routed[:, _OUT_GATE1:_OUT_GATE2 + 1]
    cnt = counts[0, _EXPERT_LANE0:_EXPERT_LANE0 + N_EXPERTS].astype(jnp.int32)
    tiles = (cnt + MOE_TILE - 1) // MOE_TILE
    tile_end = jnp.cumsum(tiles)
    tile_start = jnp.concatenate([jnp.zeros((1,), jnp.int32), tile_end]).astype(jnp.int32)
    tail_start = (tile_start[:-1] * MOE_TILE + cnt).astype(jnp.int32)
    tail_len = (tiles * MOE_TILE - cnt).astype(jnp.int32)
    return tile_start, tile_end[-1:].astype(jnp.int32), slots, gates, tail_start, tail_len


def kernel(x_prompt, x_sample, mem_prompt, state_pool, state_ret, cache_mem_k, cache_mem_v, w_in, w_pool, b_pool,
           pool_scale, ret_gn_g, ret_gn_b, w_out, ln1_g, ln1_b, w_mq, w_mk, w_mv, w_mo, ln2_g, ln2_b, w_rg, w_re,
           w1, w3, w2, ln3_g, ln3_b):
    assert w_in.shape[0] == DEPTH == 1
    bp_n, tp, _ = x_prompt.shape
    bs_n, ts, _ = x_sample.shape
    rows_p, rows_s = bp_n * tp, bs_n * ts
    assert rows_s == ROW_TILE and rows_p % ROW_TILE == 0
    n_rows = rows_p + rows_s
    tiles_p = rows_p // ROW_TILE

    w_in_b = w_in[0].astype(BF16)
    w_out_b = w_out[0].astype(BF16)
    w_mq_b = w_mq[0].astype(BF16)
    w_mk_b = w_mk[0].astype(BF16)
    w_mv_b = w_mv[0].astype(BF16)
    w_mo_b = w_mo[0].astype(BF16)
    wp_b = w_pool[0].astype(BF16)
    bp = b_pool[0].reshape(1, POOL_WIDTH)
    ps = pool_scale[0].reshape(1, POOL_WIDTH)
    gng = ret_gn_g[0].reshape(1, RET_WIDTH)
    gnb = ret_gn_b[0].reshape(1, RET_WIDTH)
    row = lambda p: p[0].reshape(1, D_MODEL)

    xp2d = x_prompt.reshape(rows_p, D_MODEL)
    xs2d = x_sample.reshape(rows_s, D_MODEL)

    mem2d = mem_prompt.reshape(bp_n * N_MEM, D_MODEL)
    mk_p = _matmul(mem2d, w_mk_b, F32, "mem_k").reshape(bp_n, N_MEM, D_MODEL)
    mv_p = _matmul(mem2d, w_mv_b, F32, "mem_v").reshape(bp_n, N_MEM, D_MODEL)

    zeros_s = jnp.zeros((bp_n, RET_HEADS, RET_HEAD_DIM, RET_HEAD_DIM), F32)
    zeros_h = jnp.zeros((bp_n, POOL_HIST, POOL_WIDTH), F32)
    cat_p, ret_p, pool_p = _mixer(xp2d, w_in_b, wp_b, bp, ps, gng, gnb, zeros_s, zeros_h,
                                  n_streams=bp_n, t_len=tp, tile_rows=MIXER_TILE, chunk=MIXER_TILE, pos0=0)
    cat_s, ret_s, pool_s = _mixer(xs2d, w_in_b, wp_b, bp, ps, gng, gnb, state_ret[0], state_pool[0],
                                  n_streams=bs_n, t_len=ts, tile_rows=rows_s, chunk=ts, pos0=PAST_LEN)
    (x1,) = _mm_ln([cat_p, cat_s], w_out_b, [xp2d, xs2d], row(ln1_g), row(ln1_b),
                   n_rows=n_rows, n_tiles_first=tiles_p, name="out_ln1")

    q_all = _matmul(x1, w_mq_b, BF16, "mem_q")
    o_p = _attention(q_all, mk_p, mv_p, n_streams=bp_n, t_len=tp, q_rows=ROW_TILE, row_offset=0, name="attn_prompt")
    o_s = _attention(q_all, cache_mem_k[0].reshape(bs_n, N_MEM, D_MODEL), cache_mem_v[0].reshape(bs_n, N_MEM, D_MODEL),
                     n_streams=bs_n, t_len=ts, q_rows=ts, row_offset=rows_p, name="attn_sample")
    x2, x2_tiles = _mm_ln([o_p, o_s], w_mo_b, [x1], row(ln2_g), row(ln2_b),
                          n_rows=n_rows, n_tiles_first=tiles_p, name="mo_ln2", emit_packed=True)

    w_router = jnp.concatenate([w_rg[0], w_re[0].reshape(D_MODEL, N_EXPERTS),
                                jnp.zeros((D_MODEL, LANES - N_GROUPS - N_EXPERTS), F32)], axis=1)
    routed, counts = _router(x2, w_router)
    n_tiles = (2 * n_rows) // MOE_TILE + N_EXPERTS
    tile_start, n_active, slots, gates, tail_start, tail_len = _dispatch_plan(routed, counts)
    x_sorted = _dispatch(slots, tail_start, tail_len, x2_tiles, n_tiles)
    y_sorted = _experts(tile_start, n_active, x_sorted, w1, w3, w2, n_tiles)
    y_p, y_s = _combine(slots, y_sorted, gates, x2, row(ln3_g), row(ln3_b), rows_p)

    kv_shape = (DEPTH, bp_n, N_MEM, MEM_HEADS, MEM_HEAD_DIM)
    return (y_p.reshape(bp_n, tp, D_MODEL), y_s.reshape(bs_n, ts, D_MODEL), pool_p[None], ret_p[None],
            mk_p.reshape(kv_shape), mv_p.reshape(kv_shape), pool_s[None], ret_s[None])
```

```python
import functools
import math

import jax
import jax.numpy as jnp
from jax import lax
from jax.experimental import pallas as pl
from jax.experimental.pallas import tpu as pltpu

F32 = jnp.float32
BF16 = jnp.bfloat16

D_MODEL = 2048
POOL_WIDTH = 1024
POOL_WINDOWS = (2, 4, 8, 16)
POOL_CH = 256
POOL_HIST = 15
RET_WIDTH = 1024
RET_HEADS = 8
RET_HEAD_DIM = 128
IN_WIDTH = POOL_WIDTH + 4 * RET_WIDTH
ROPE_BASE = 10000.0
N_MEM = 256
MEM_HEADS = 4
MEM_HEAD_DIM = 512
N_GROUPS = 4
EXPERTS_PER_GROUP = 8
N_EXPERTS = N_GROUPS * EXPERTS_PER_GROUP
EXPERT_HIDDEN = 512
LN_EPS = 1e-5
GN_EPS = 1e-6
DEPTH = 1
ALPHA = (2.0 * DEPTH) ** 0.25
PAST_LEN = 2048
PROMPT_CHUNK = 64

LANES = 128
HIST_PAD = 16
ROW_TILE = 512
MIXER_TILE = 256
MOE_TILE = 256
COMBINE_TILE = 256
VMEM_LIMIT = 58 * 1024 * 1024

_NEG_INF = float("-inf")


def _const_spec(shape):
    zeros = (0,) * len(shape)
    return pl.BlockSpec(shape, lambda *_: zeros, pipeline_mode=pl.Buffered(1))


def _when(cond, fn):
    if cond is True:
        fn()
    else:
        pl.when(cond)(fn)


def _layer_norm(z, g, b):
    mu = jnp.mean(z, axis=-1, keepdims=True)
    zc = z - mu
    var = jnp.mean(zc * zc, axis=-1, keepdims=True)
    return zc * lax.rsqrt(var + LN_EPS) * g + b


def _silu(a):
    return a * (1.0 / (1.0 + jnp.exp(-a)))


TOKEN_TILE_ROWS = D_MODEL // LANES


def _to_token_tiles(x):
    rows = x.shape[0]
    return x.astype(BF16).reshape(rows, TOKEN_TILE_ROWS, LANES).reshape(rows * TOKEN_TILE_ROWS, LANES)


def _from_token_tiles(tiles):
    rows = tiles.shape[0] // TOKEN_TILE_ROWS
    return tiles.reshape(rows, TOKEN_TILE_ROWS, LANES).reshape(rows, D_MODEL)


def _mix_segment(h_ref, r0, cos_ref, sin_ref, dec_ref, kd_ref, qd_ref, wp_ref, bp_ref, ps_ref, gng_ref, gnb_ref,
                 s0_ref, h0_ref, cat_ref, snew_ref, hnew_ref, s_ref, u_ref,
                 *, seg_len, chunk, pos_start, first, g_chunk):
    hist = jnp.concatenate([jnp.zeros((1, POOL_WIDTH), F32), h0_ref[0]], axis=0)
    if first is True:
        s_ref[...] = s0_ref[0]
        u_ref[0:HIST_PAD, :] = hist
    else:
        s_ref[...] = jnp.where(first, s0_ref[0], s_ref[...])
        u_ref[0:HIST_PAD, :] = jnp.where(first, hist, u_ref[0:HIST_PAD, :])

    u_new = h_ref[pl.ds(r0, seg_len), 0:POOL_WIDTH]
    u_ref[HIST_PAD:HIST_PAD + seg_len, :] = u_new
    pos = (pos_start + lax.broadcasted_iota(jnp.int32, (seg_len, 1), 0)).astype(F32)
    for gi, w in enumerate(POOL_WINDOWS):
        cols = slice(gi * POOL_CH, (gi + 1) * POOL_CH)
        win = u_ref[HIST_PAD:HIST_PAD + seg_len, cols]
        for back in range(1, w):
            win = win + u_ref[HIST_PAD - back:HIST_PAD - back + seg_len, cols]
        cnt = jnp.minimum(float(w), pos + 1.0)
        d = win * (1.0 / cnt) - u_ref[HIST_PAD:HIST_PAD + seg_len, cols]
        pooled = jnp.dot(d.astype(BF16), wp_ref[gi], preferred_element_type=F32) + bp_ref[:, cols]
        cat_ref[:, cols] = (pooled * ps_ref[:, cols]).astype(BF16)

    hnew_ref[0] = u_ref[seg_len + 1:seg_len + HIST_PAD, :]
    u_ref[0:HIST_PAD, :] = u_ref[seg_len:seg_len + HIST_PAD, :]

    scale = RET_HEAD_DIM ** -0.5
    for c in range(seg_len // chunk):
        rows = pl.ds(r0 + c * chunk, chunk)
        trows = slice(c * chunk, (c + 1) * chunk)
        cos_t = cos_ref[trows, :]
        sin_t = sin_ref[trows, :]
        for hd in range(RET_HEADS):
            lo = hd * RET_HEAD_DIM
            hcols = slice(lo, lo + RET_HEAD_DIM)
            q = h_ref[rows, POOL_WIDTH + lo:POOL_WIDTH + lo + RET_HEAD_DIM]
            k = h_ref[rows, POOL_WIDTH + RET_WIDTH + lo:POOL_WIDTH + RET_WIDTH + lo + RET_HEAD_DIM]
            v = h_ref[rows, POOL_WIDTH + 2 * RET_WIDTH + lo:POOL_WIDTH + 2 * RET_WIDTH + lo + RET_HEAD_DIM]
            gate = h_ref[rows, POOL_WIDTH + 3 * RET_WIDTH + lo:POOL_WIDTH + 3 * RET_WIDTH + lo + RET_HEAD_DIM]
            qr = q * cos_t + pltpu.roll(q, RET_HEAD_DIM // 2, 1) * sin_t
            kr = (k * cos_t + pltpu.roll(k, RET_HEAD_DIM // 2, 1) * sin_t) * scale
            vb = v.astype(BF16)
            scores = lax.dot_general(qr.astype(BF16), kr.astype(BF16), (((1,), (1,)), ((), ())),
                                     preferred_element_type=F32) * dec_ref[hd]
            o = jnp.dot(scores.astype(BF16), vb, preferred_element_type=F32)
            s_prev = s_ref[hd]
            o = o + jnp.dot((qr * qd_ref[hd]).astype(BF16), s_prev.astype(BF16), preferred_element_type=F32)
            upd = lax.dot_general((kr * kd_ref[hd]).astype(BF16), vb, (((0,), (0,)), ((), ())),
                                  preferred_element_type=F32)
            s_ref[hd] = g_chunk[hd] * s_prev + upd
            mu = jnp.mean(o, axis=-1, keepdims=True)
            oc = o - mu
            var = jnp.mean(oc * oc, axis=-1, keepdims=True)
            on = oc * lax.rsqrt(var + GN_EPS) * gng_ref[:, hcols] + gnb_ref[:, hcols]
            cat_ref[trows, POOL_WIDTH + lo:POOL_WIDTH + lo + RET_HEAD_DIM] = (on * _silu(gate)).astype(BF16)

    snew_ref[0] = s_ref[...]


def _mixer_segments_kernel(x_ref, w_in_ref, *refs, seg_len, chunk, pos0, g_chunk):
    (*mix_refs, h_ref, s_ref, u_ref) = refs
    j = pl.program_id(0)

    @pl.when(j == 0)
    def _project():
        h_ref[...] = jnp.dot(x_ref[...].astype(BF16), w_in_ref[...], preferred_element_type=F32)

    _mix_segment(h_ref, pl.multiple_of(j * seg_len, seg_len), *mix_refs, s_ref, u_ref,
                 seg_len=seg_len, chunk=chunk, pos_start=pos0, first=True, g_chunk=g_chunk)


def _mixer_pipelined_kernel(x_ref, w_in_ref, *refs, tiles_per_stream, seg_len, chunk, pos0, g_chunk):
    (*mix_refs, ha_ref, hb_ref, s_ref, u_ref) = refs
    g = pl.program_id(0)
    m = jnp.maximum(g - 1, 0)
    first = ((m % tiles_per_stream) == 0) | (g == 0)
    pos_start = pos0 + (m % tiles_per_stream) * seg_len

    @pl.when(g == 0)
    def _():
        hb_ref[...] = jnp.zeros_like(hb_ref)

    def step(h_write, h_read):
        _mix_segment(h_read, 0, *mix_refs, s_ref, u_ref, seg_len=seg_len, chunk=chunk, pos_start=pos_start,
                     first=first, g_chunk=g_chunk)
        h_write[...] = jnp.dot(x_ref[...].astype(BF16), w_in_ref[...], preferred_element_type=F32)

    pl.when(g % 2 == 0)(lambda: step(ha_ref, hb_ref))
    pl.when(g % 2 == 1)(lambda: step(hb_ref, ha_ref))


def _retention_tables(chunk, t_len, pos0):
    log_gamma = jnp.log(1.0 - 2.0 ** (-5.0 - jnp.arange(RET_HEADS, dtype=F32)))
    idx = jnp.arange(chunk, dtype=F32)
    diff = idx[:, None] - idx[None, :]
    dec = jnp.where(diff >= 0, jnp.exp(log_gamma[:, None, None] * jnp.maximum(diff, 0.0)), 0.0)
    kd = jnp.exp(log_gamma[:, None] * (chunk - 1.0 - idx)[None, :])
    qd = jnp.exp(log_gamma[:, None] * (idx + 1.0)[None, :])
    kd = jnp.broadcast_to(kd[:, :, None], (RET_HEADS, chunk, RET_HEAD_DIM))
    qd = jnp.broadcast_to(qd[:, :, None], (RET_HEADS, chunk, RET_HEAD_DIM))
    half = RET_HEAD_DIM // 2
    freqs = ROPE_BASE ** (-jnp.arange(half, dtype=F32) / half)
    pos = pos0 + jnp.arange(t_len, dtype=F32)
    ang = pos[:, None] * freqs[None, :]
    cos = jnp.cos(ang)
    sin = jnp.sin(ang)
    cos_t = jnp.concatenate([cos, cos], axis=-1)
    sin_t = jnp.concatenate([-sin, sin], axis=-1)
    g_chunk = tuple(math.exp(math.log(1.0 - 2.0 ** (-5.0 - h)) * chunk) for h in range(RET_HEADS))
    return dec, kd, qd, cos_t, sin_t, g_chunk


def _mixer(x2d, w_in_b, wp_b, bp, ps, gng, gnb, s0, h0, *, n_streams, t_len, tile_rows, chunk, pos0):
    rows = n_streams * t_len
    pipelined = t_len > tile_rows
    seg_len = tile_rows if pipelined else t_len
    assert (t_len % tile_rows == 0) if pipelined else (rows == tile_rows)
    assert seg_len % chunk == 0 and t_len >= POOL_HIST
    dec, kd, qd, cos_t, sin_t, g_chunk = _retention_tables(chunk, t_len, pos0)
    state_block = (1, RET_HEADS, RET_HEAD_DIM, RET_HEAD_DIM)
    hist_block = (1, POOL_HIST, POOL_WIDTH)

    if pipelined:
        n_tiles = rows // tile_rows
        tiles_per_stream = t_len // tile_rows
        grid = (n_tiles + 1,)
        mixed = lambda g: jnp.maximum(g - 1, 0)
        x_map = lambda g: (jnp.minimum(g, n_tiles - 1), 0)
        time_map = lambda g: (mixed(g) % tiles_per_stream, 0)
        cat_map = lambda g: (mixed(g), 0)
        state_map = lambda g: (mixed(g) // tiles_per_stream, 0, 0, 0)
        hist_map = lambda g: (mixed(g) // tiles_per_stream, 0, 0)
        kern = functools.partial(_mixer_pipelined_kernel, tiles_per_stream=tiles_per_stream, seg_len=seg_len,
                                 chunk=chunk, pos0=pos0, g_chunk=g_chunk)
        h_scratch = [pltpu.VMEM((tile_rows, IN_WIDTH), F32), pltpu.VMEM((tile_rows, IN_WIDTH), F32)]
    else:
        grid = (n_streams,)
        x_map = lambda j: (0, 0)
        time_map = lambda j: (0, 0)
        cat_map = lambda j: (j, 0)
        state_map = lambda j: (j, 0, 0, 0)
        hist_map = lambda j: (j, 0, 0)
        kern = functools.partial(_mixer_segments_kernel, seg_len=seg_len, chunk=chunk, pos0=pos0, g_chunk=g_chunk)
        h_scratch = [pltpu.VMEM((tile_rows, IN_WIDTH), F32)]

    return pl.pallas_call(
        kern,
        grid=grid,
        in_specs=[
            pl.BlockSpec((tile_rows, D_MODEL), x_map),
            _const_spec((D_MODEL, IN_WIDTH)),
            pl.BlockSpec((seg_len, RET_HEAD_DIM), time_map),
            pl.BlockSpec((seg_len, RET_HEAD_DIM), time_map),
            _const_spec((RET_HEADS, chunk, chunk)),
            _const_spec((RET_HEADS, chunk, RET_HEAD_DIM)),
            _const_spec((RET_HEADS, chunk, RET_HEAD_DIM)),
            _const_spec((len(POOL_WINDOWS), POOL_CH, POOL_CH)),
            _const_spec((1, POOL_WIDTH)),
            _const_spec((1, POOL_WIDTH)),
            _const_spec((1, RET_WIDTH)),
            _const_spec((1, RET_WIDTH)),
            pl.BlockSpec(state_block, state_map),
            pl.BlockSpec(hist_block, hist_map),
        ],
        out_specs=[
            pl.BlockSpec((seg_len, D_MODEL), cat_map),
            pl.BlockSpec(state_block, state_map),
            pl.BlockSpec(hist_block, hist_map),
        ],
        out_shape=[
            jax.ShapeDtypeStruct((rows, D_MODEL), BF16),
            jax.ShapeDtypeStruct((n_streams, RET_HEADS, RET_HEAD_DIM, RET_HEAD_DIM), F32),
            jax.ShapeDtypeStruct((n_streams, POOL_HIST, POOL_WIDTH), F32),
        ],
        scratch_shapes=h_scratch + [
            pltpu.VMEM((RET_HEADS, RET_HEAD_DIM, RET_HEAD_DIM), F32),
            pltpu.VMEM((HIST_PAD + seg_len, POOL_WIDTH), F32),
        ],
        compiler_params=pltpu.CompilerParams(dimension_semantics=("arbitrary",),
                                             vmem_limit_bytes=VMEM_LIMIT),
        name="mixer",
    )(x2d, w_in_b, cos_t, sin_t, dec, kd, qd, wp_b, bp, ps, gng, gnb, s0, h0)


def _row_sources(arrays, n_tiles_first):
    if len(arrays) == 1:
        return [pl.BlockSpec((ROW_TILE, arrays[0].shape[1]), lambda i: (i, 0))]
    first, second = arrays
    return [
        pl.BlockSpec((ROW_TILE, first.shape[1]), lambda i: (jnp.minimum(i, n_tiles_first - 1), 0)),
        pl.BlockSpec((ROW_TILE, second.shape[1]), lambda i: (jnp.maximum(i - n_tiles_first, 0), 0),
                     pipeline_mode=pl.Buffered(1)),
    ]


def _mm_ln_kernel(*refs, n_a, n_res, n_tiles_first, emit_packed):
    a_refs = refs[:n_a]
    w_ref = refs[n_a]
    res_refs = refs[n_a + 1:n_a + 1 + n_res]
    g_ref, b_ref, o_ref = refs[n_a + 1 + n_res:n_a + 4 + n_res]
    tiles_ref = refs[n_a + 4 + n_res] if emit_packed else None
    wb_ref = refs[-1]
    i = pl.program_id(0)

    @pl.when(i == 0)
    def _():
        wb_ref[...] = w_ref[0].astype(BF16)

    def body(a_ref, res_ref):
        half = ROW_TILE // 2
        for h in range(2):
            rows = slice(h * half, (h + 1) * half)
            acc = jnp.dot(a_ref[rows, :].astype(BF16), wb_ref[...], preferred_element_type=F32)
            out = _layer_norm(ALPHA * res_ref[rows, :] + acc, g_ref[...], b_ref[...])
            o_ref[rows, :] = out
            if emit_packed:
                tile_rows = slice(h * half * TOKEN_TILE_ROWS, (h + 1) * half * TOKEN_TILE_ROWS)
                tiles_ref[tile_rows, :] = _to_token_tiles(out)

    pl.when(i < n_tiles_first)(lambda: body(a_refs[0], res_refs[0]))
    pl.when(i >= n_tiles_first)(lambda: body(a_refs[-1], res_refs[-1]))


def _mm_ln(a_arrays, w, res_arrays, g, b, *, n_rows, n_tiles_first, name, emit_packed=False):
    n_tiles = n_rows // ROW_TILE
    kern = functools.partial(_mm_ln_kernel, n_a=len(a_arrays), n_res=len(res_arrays),
                             n_tiles_first=n_tiles_first, emit_packed=emit_packed)
    out_specs = [pl.BlockSpec((ROW_TILE, D_MODEL), lambda i: (i, 0))]
    out_shape = [jax.ShapeDtypeStruct((n_rows, D_MODEL), F32)]
    if emit_packed:
        out_specs.append(pl.BlockSpec((ROW_TILE * TOKEN_TILE_ROWS, LANES), lambda i: (i, 0)))
        out_shape.append(jax.ShapeDtypeStruct((n_rows * TOKEN_TILE_ROWS, LANES), BF16))
    return pl.pallas_call(
        kern,
        grid=(n_tiles,),
        in_specs=(_row_sources(a_arrays, n_tiles_first) + [_const_spec(w.shape)]
                  + _row_sources(res_arrays, n_tiles_first)
                  + [_const_spec((1, D_MODEL)), _const_spec((1, D_MODEL))]),
        out_specs=out_specs,
        out_shape=out_shape,
        scratch_shapes=[pltpu.VMEM(w.shape[1:], BF16)],
        compiler_params=pltpu.CompilerParams(dimension_semantics=("arbitrary",),
                                             vmem_limit_bytes=VMEM_LIMIT),
        name=name,
    )(*a_arrays, w, *res_arrays, g, b)


def _matmul_kernel(a_ref, w_ref, o_ref, wb_ref):
    @pl.when(pl.program_id(0) == 0)
    def _():
        wb_ref[...] = w_ref[0].astype(BF16)

    o_ref[...] = jnp.dot(a_ref[...].astype(BF16), wb_ref[...], preferred_element_type=F32).astype(o_ref.dtype)


def _matmul(a, w, out_dtype, name):
    n_rows = a.shape[0]
    return pl.pallas_call(
        _matmul_kernel,
        grid=(n_rows // ROW_TILE,),
        in_specs=[pl.BlockSpec((ROW_TILE, a.shape[1]), lambda i: (i, 0)), _const_spec(w.shape)],
        out_specs=pl.BlockSpec((ROW_TILE, w.shape[2]), lambda i: (i, 0)),
        out_shape=jax.ShapeDtypeStruct((n_rows, w.shape[2]), out_dtype),
        scratch_shapes=[pltpu.VMEM(w.shape[1:], BF16)],
        compiler_params=pltpu.CompilerParams(dimension_semantics=("arbitrary",),
                                             vmem_limit_bytes=VMEM_LIMIT),
        name=name,
    )(a, w)


def _attention_kernel(q_ref, k_ref, v_ref, o_ref):
    scale = MEM_HEAD_DIM ** -0.5
    heads_split = len(k_ref.shape) == 5

    def head(ref, h):
        if heads_split:
            return ref[0, 0, :, h, :].astype(BF16)
        return ref[0, :, h * MEM_HEAD_DIM:(h + 1) * MEM_HEAD_DIM].astype(BF16)

    for h in range(MEM_HEADS):
        cols = slice(h * MEM_HEAD_DIM, (h + 1) * MEM_HEAD_DIM)
        s = lax.dot_general(q_ref[:, cols], head(k_ref, h), (((1,), (1,)), ((), ())),
                            preferred_element_type=F32) * scale
        m = jnp.max(s, axis=-1, keepdims=True)
        p = jnp.exp(s - m)
        p = p * (1.0 / jnp.sum(p, axis=-1, keepdims=True))
        o_ref[:, cols] = jnp.dot(p.astype(BF16), head(v_ref, h), preferred_element_type=F32).astype(o_ref.dtype)


def _attention(q_all, mem_k, mem_v, *, n_streams, t_len, q_rows, row_offset, name):
    tiles_per_stream = t_len // q_rows
    base = row_offset // q_rows
    if mem_k.ndim == 5:
        kv_spec = pl.BlockSpec((1, 1, N_MEM, MEM_HEADS, MEM_HEAD_DIM), lambda b, t: (0, b, 0, 0, 0))
    else:
        kv_spec = pl.BlockSpec((1, N_MEM, D_MODEL), lambda b, t: (b, 0, 0))
    return pl.pallas_call(
        _attention_kernel,
        grid=(n_streams, tiles_per_stream),
        in_specs=[
            pl.BlockSpec((q_rows, D_MODEL), lambda b, t: (base + b * tiles_per_stream + t, 0)),
            kv_spec,
            kv_spec,
        ],
        out_specs=pl.BlockSpec((q_rows, D_MODEL), lambda b, t: (b * tiles_per_stream + t, 0)),
        out_shape=jax.ShapeDtypeStruct((n_streams * t_len, D_MODEL), BF16),
        compiler_params=pltpu.CompilerParams(dimension_semantics=("arbitrary", "arbitrary"),
                                             vmem_limit_bytes=VMEM_LIMIT),
        name=name,
    )(q_all, mem_k, mem_v)


_GROUP_LANE0 = 0
_EXPERT_LANE0 = N_GROUPS
(_META_LANE1, _META_LANE2, _META_POS1, _META_POS2, _META_GATE1, _META_GATE2) = range(6)
(_OUT_SLOT1, _OUT_SLOT2, _OUT_GATE1, _OUT_GATE2) = range(4)
_MOE_TILE_LOG2 = MOE_TILE.bit_length() - 1
assert 1 << _MOE_TILE_LOG2 == MOE_TILE


def _router_kernel(x_ref, wh_ref, wl_ref, out_ref, counts_ref, carry_ref, meta_ref, start_ref):
    phase = pl.program_id(0)
    i = pl.program_id(1)
    rows = x_ref.shape[0]
    tile_rows = pl.ds(pl.multiple_of(i * rows, rows), rows)
    lane = lax.broadcasted_iota(jnp.int32, (rows, LANES), 1)

    @pl.when((phase == 0) & (i == 0))
    def _():
        carry_ref[...] = jnp.zeros_like(carry_ref)

    @pl.when(phase == 0)
    def _route():
        x = x_ref[...]
        xh = x.astype(BF16)
        xl = (x - xh.astype(F32)).astype(BF16)
        logits = (jnp.dot(xh, wh_ref[...], preferred_element_type=F32)
                  + jnp.dot(xl, wh_ref[...], preferred_element_type=F32)
                  + jnp.dot(xh, wl_ref[...], preferred_element_type=F32))

        def first_argmax(vals):
            m = jnp.max(vals, axis=-1, keepdims=True)
            idx = jnp.min(jnp.where(vals == m, lane, LANES), axis=-1, keepdims=True)
            return m, idx

        gl = jnp.where(lane < N_GROUPS, logits, _NEG_INF)
        gm, g_idx = first_argmax(gl)
        g_w = 1.0 / jnp.sum(jnp.exp(gl - gm), axis=-1, keepdims=True)

        in_group = ((lane >= _EXPERT_LANE0) & (lane < _EXPERT_LANE0 + N_EXPERTS)
                    & (((lane - _EXPERT_LANE0) >> 3) == g_idx))
        el = jnp.where(in_group, logits, _NEG_INF)
        m1, i1 = first_argmax(el)
        z = jnp.sum(jnp.exp(el - m1), axis=-1, keepdims=True)
        m2, i2 = first_argmax(jnp.where(lane == i1, _NEG_INF, el))
        p1 = 1.0 / z
        p2 = jnp.exp(m2 - m1) / z
        den = p1 + p2
        gate1 = p1 / den * g_w
        gate2 = p2 / den * g_w

        hit1 = lane == i1
        hit2 = lane == i2
        onehot = (hit1 | hit2).astype(BF16)
        earlier = (lax.broadcasted_iota(jnp.int32, (rows, rows), 1)
                   < lax.broadcasted_iota(jnp.int32, (rows, rows), 0)).astype(BF16)
        rank = jnp.dot(earlier, onehot, preferred_element_type=F32) + carry_ref[...]
        pos1 = jnp.sum(jnp.where(hit1, rank, 0.0), axis=-1, keepdims=True)
        pos2 = jnp.sum(jnp.where(hit2, rank, 0.0), axis=-1, keepdims=True)
        carry_ref[...] = carry_ref[...] + jnp.sum(onehot.astype(F32), axis=0, keepdims=True)

        meta = jnp.zeros((rows, LANES), F32)
        for col, val in ((_META_LANE1, i1.astype(F32)), (_META_LANE2, i2.astype(F32)), (_META_POS1, pos1),
                         (_META_POS2, pos2), (_META_GATE1, gate1), (_META_GATE2, gate2)):
            meta = jnp.where(lane == col, val, meta)
        meta_ref[tile_rows, :] = meta

    @pl.when((phase == 1) & (i == 0))
    def _segment_starts():
        counts_ref[...] = carry_ref[...]
        tiles = ((carry_ref[...].astype(jnp.int32) + (MOE_TILE - 1)) >> _MOE_TILE_LOG2).astype(F32).astype(BF16)
        before = (lax.broadcasted_iota(jnp.int32, (LANES, LANES), 0)
                  < lax.broadcasted_iota(jnp.int32, (LANES, LANES), 1)).astype(BF16)
        tiles8 = jnp.broadcast_to(tiles, (8, LANES))
        start_ref[...] = jnp.dot(tiles8, before, preferred_element_type=F32)[0:1, :] * float(MOE_TILE)

    @pl.when(phase == 1)
    def _slots():
        meta = meta_ref[tile_rows, :]
        starts = start_ref[...]

        def col(c):
            return jnp.sum(jnp.where(lane == c, meta, 0.0), axis=-1, keepdims=True)

        def start_of(expert_lane):
            return jnp.sum(jnp.where(lane == expert_lane.astype(jnp.int32), starts, 0.0), axis=-1, keepdims=True)

        slot1 = start_of(col(_META_LANE1)) + col(_META_POS1)
        slot2 = start_of(col(_META_LANE2)) + col(_META_POS2)
        out = jnp.zeros((rows, LANES), F32)
        for c, val in ((_OUT_SLOT1, slot1), (_OUT_SLOT2, slot2), (_OUT_GATE1, col(_META_GATE1)),
                       (_OUT_GATE2, col(_META_GATE2))):
            out = jnp.where(lane == c, val, out)
        out_ref[...] = out


def _router(x_all, w_router):
    n_rows = x_all.shape[0]
    n_tiles = n_rows // ROW_TILE
    w_hi = w_router.astype(BF16)
    w_lo = (w_router - w_hi.astype(F32)).astype(BF16)
    return pl.pallas_call(
        _router_kernel,
        grid=(2, n_tiles),
        in_specs=[pl.BlockSpec((ROW_TILE, D_MODEL), lambda p, i: (i * (1 - p) + (n_tiles - 1) * p, 0)),
                  _const_spec((D_MODEL, LANES)), _const_spec((D_MODEL, LANES))],
        out_specs=[pl.BlockSpec((ROW_TILE, LANES), lambda p, i: (i * p, 0)),
                   pl.BlockSpec((1, LANES), lambda p, i: (0, 0))],
        out_shape=[jax.ShapeDtypeStruct((n_rows, LANES), F32), jax.ShapeDtypeStruct((1, LANES), F32)],
        scratch_shapes=[pltpu.VMEM((1, LANES), F32), pltpu.VMEM((n_rows, LANES), F32), pltpu.VMEM((1, LANES), F32)],
        compiler_params=pltpu.CompilerParams(dimension_semantics=("arbitrary", "arbitrary"),
                                             vmem_limit_bytes=VMEM_LIMIT),
        name="router",
    )(x_all, w_hi, w_lo)


def _token_rows(first_token, n_tokens):
    return pl.ds(pl.multiple_of(first_token * TOKEN_TILE_ROWS, TOKEN_TILE_ROWS), n_tokens * TOKEN_TILE_ROWS)


_TAIL_BITS = MOE_TILE.bit_length() - 1


def _dispatch_kernel(slots_ref, tail_start_ref, tail_len_ref, x_ref, xs_hbm, zeros_ref, sem, zsem):
    i = pl.program_id(0)
    n_tokens = x_ref.shape[0] // TOKEN_TILE_ROWS
    group = 16

    def issue(j, carry):
        for u in range(group):
            r = j * group + u
            src = x_ref.at[pl.ds(pl.multiple_of(r * TOKEN_TILE_ROWS, TOKEN_TILE_ROWS), TOKEN_TILE_ROWS), :]
            for k in range(2):
                slot = slots_ref[2 * (i * n_tokens + r) + k]
                pltpu.make_async_copy(src, xs_hbm.at[_token_rows(slot, 1), :], sem).start(priority=k)
        return carry
    lax.fori_loop(0, n_tokens // group, issue, 0)

    def drain(j, carry):
        pltpu.make_async_copy(x_ref.at[pl.ds(0, 2 * group * TOKEN_TILE_ROWS), :],
                              xs_hbm.at[_token_rows(0, 2 * group), :], sem).wait()
        return carry
    lax.fori_loop(0, n_tokens // group, drain, 0)

    def tail_copies(fn):
        def per_expert(ex, carry):
            length = tail_len_ref[ex]
            pos = tail_start_ref[ex]
            for bit in reversed(range(_TAIL_BITS)):
                piece = 1 << bit
                take = (length & piece) != 0
                cp = pltpu.make_async_copy(zeros_ref.at[pl.ds(0, piece * TOKEN_TILE_ROWS), :],
                                           xs_hbm.at[_token_rows(pos, piece), :], zsem)
                pl.when(take)(lambda cp=cp: fn(cp))
                pos = pos + jnp.where(take, piece, 0)
            return carry
        lax.fori_loop(0, N_EXPERTS, per_expert, 0)

    def unused_copies(fn):
        half = zeros_ref.shape[0] // TOKEN_TILE_ROWS
        first_unused = tail_start_ref[N_EXPERTS - 1] + tail_len_ref[N_EXPERTS - 1]

        def per_half(j, carry):
            fn(pltpu.make_async_copy(zeros_ref, xs_hbm.at[_token_rows(first_unused + j * half, half), :], zsem))
            return carry
        lax.fori_loop(0, (xs_hbm.shape[0] // TOKEN_TILE_ROWS - first_unused) // half, per_half, 0)

    @pl.when(i == pl.num_programs(0) - 1)
    def _tails():
        zeros_ref[...] = jnp.zeros_like(zeros_ref)
        tail_copies(lambda cp: cp.start())
        unused_copies(lambda cp: cp.start())
        tail_copies(lambda cp: cp.wait())
        unused_copies(lambda cp: cp.wait())


def _dispatch(slots, tail_start, tail_len, xg, n_tiles_max):
    block_rows = ROW_TILE * TOKEN_TILE_ROWS
    assert xg.shape[0] % block_rows == 0
    return pl.pallas_call(
        _dispatch_kernel,
        grid_spec=pltpu.PrefetchScalarGridSpec(
            num_scalar_prefetch=3,
            grid=(xg.shape[0] // block_rows,),
            in_specs=[pl.BlockSpec((block_rows, LANES), lambda i, sl, ts, tl: (i, 0))],
            out_specs=pl.BlockSpec(memory_space=pl.ANY),
            scratch_shapes=[pltpu.VMEM((MOE_TILE // 2 * TOKEN_TILE_ROWS, LANES), BF16),
                            pltpu.SemaphoreType.DMA(()), pltpu.SemaphoreType.DMA(())],
        ),
        out_shape=jax.ShapeDtypeStruct((n_tiles_max * MOE_TILE * TOKEN_TILE_ROWS, LANES), BF16),
        compiler_params=pltpu.CompilerParams(dimension_semantics=("arbitrary",)),
        name="dispatch",
    )(slots, tail_start, tail_len, xg)


def _experts_kernel(tile_start_ref, n_active_ref, xs_hbm, w1_ref, w3_ref, w2_ref, y_hbm,
                    xbuf, ybuf, w1b, w3b, w2b, gsem, ysem, *, n_tiles_max):
    e = pl.program_id(0)
    n_active = n_active_ref[0]

    def x_copy(tile, buf):
        return pltpu.make_async_copy(xs_hbm.at[_token_rows(tile * MOE_TILE, MOE_TILE), :], xbuf.at[buf], gsem.at[buf])

    def y_copy(tile, buf):
        return pltpu.make_async_copy(ybuf.at[buf], y_hbm.at[_token_rows(tile * MOE_TILE, MOE_TILE), :], ysem.at[buf])

    @pl.when(e == 0)
    def _():
        x_copy(0, 0).start()

    w1b[...] = w1_ref[0, 0, 0].astype(BF16)
    w3b[...] = w3_ref[0, 0, 0].astype(BF16)
    w2b[...] = w2_ref[0, 0, 0].astype(BF16)
    first_tile = tile_start_ref[e]

    def tile_body(t, carry):
        g = first_tile + t
        buf = g % 2
        x_copy(g, buf).wait()

        @pl.when(g >= 2)
        def _():
            y_copy(g, buf).wait()

        @pl.when(g + 1 < n_active)
        def _():
            x_copy(g + 1, 1 - buf).start()

        x = _from_token_tiles(xbuf[buf])
        a = jnp.dot(x, w1b[...], preferred_element_type=F32)
        b = jnp.dot(x, w3b[...], preferred_element_type=F32)
        hdn = (_silu(a) * b).astype(BF16)
        y = jnp.dot(hdn, w2b[...], preferred_element_type=F32)
        ybuf[buf] = _to_token_tiles(y)
        y_copy(g, buf).start()
        return carry

    lax.fori_loop(0, tile_start_ref[e + 1] - first_tile, tile_body, 0)

    @pl.when(e == N_EXPERTS - 1)
    def _drain():
        @pl.when(n_active >= 2)
        def _():
            y_copy(0, n_active % 2).wait()

        @pl.when(n_active >= 1)
        def _():
            y_copy(0, (n_active + 1) % 2).wait()

        ybuf[0] = jnp.zeros(ybuf.shape[1:], ybuf.dtype)

        def fill(g, carry):
            cp = y_copy(g, 0)
            cp.start()
            cp.wait()
            return carry
        lax.fori_loop(n_active, n_tiles_max, fill, 0)


def _experts(tile_start, n_active, x_sorted, w1, w3, w2, n_tiles_max):
    def expert_block(e, ts, na):
        return (0, e // EXPERTS_PER_GROUP, e % EXPERTS_PER_GROUP, 0, 0)

    tile_words = MOE_TILE * TOKEN_TILE_ROWS
    kern = functools.partial(_experts_kernel, n_tiles_max=n_tiles_max)
    return pl.pallas_call(
        kern,
        grid_spec=pltpu.PrefetchScalarGridSpec(
            num_scalar_prefetch=2,
            grid=(N_EXPERTS,),
            in_specs=[
                pl.BlockSpec(memory_space=pl.ANY),
                pl.BlockSpec((1, 1, 1, D_MODEL, EXPERT_HIDDEN), expert_block),
                pl.BlockSpec((1, 1, 1, D_MODEL, EXPERT_HIDDEN), expert_block),
                pl.BlockSpec((1, 1, 1, EXPERT_HIDDEN, D_MODEL), expert_block),
            ],
            out_specs=pl.BlockSpec(memory_space=pl.ANY),
            scratch_shapes=[
                pltpu.VMEM((2, tile_words, LANES), BF16),
                pltpu.VMEM((2, tile_words, LANES), BF16),
                pltpu.VMEM((D_MODEL, EXPERT_HIDDEN), BF16),
                pltpu.VMEM((D_MODEL, EXPERT_HIDDEN), BF16),
                pltpu.VMEM((EXPERT_HIDDEN, D_MODEL), BF16),
                pltpu.SemaphoreType.DMA((2,)),
                pltpu.SemaphoreType.DMA((2,)),
            ],
        ),
        out_shape=jax.ShapeDtypeStruct((n_tiles_max * tile_words, LANES), BF16),
        compiler_params=pltpu.CompilerParams(dimension_semantics=("arbitrary",),
                                             vmem_limit_bytes=VMEM_LIMIT),
        name="experts",
    )(tile_start, n_active, x_sorted, w1, w3, w2)


def _combine_kernel(slot_ref, y_hbm, gates_ref, x_ref, g_ref, b_ref, op_ref, os_ref, ybuf, sem,
                    *, n_tiles, n_tiles_first):
    i = pl.program_id(0)

    tile_words = COMBINE_TILE * TOKEN_TILE_ROWS

    def gather(tile, buf):
        for r in range(COMBINE_TILE):
            tok = tile * COMBINE_TILE + r
            for k in range(2):
                row0 = pl.multiple_of(slot_ref[2 * tok + k] * TOKEN_TILE_ROWS, TOKEN_TILE_ROWS)
                pltpu.make_async_copy(y_hbm.at[pl.ds(row0, TOKEN_TILE_ROWS), :],
                                      ybuf.at[buf, k, pl.ds(r * TOKEN_TILE_ROWS, TOKEN_TILE_ROWS), :],
                                      sem.at[buf]).start(priority=k)

    buf = i % 2

    def wait_all():
        for k in range(2):
            pltpu.make_async_copy(y_hbm.at[pl.ds(0, tile_words), :], ybuf.at[buf, k], sem.at[buf]).wait()

    def compute():
        y = (gates_ref[:, 0:1] * _from_token_tiles(ybuf[buf, 0]).astype(F32)
             + gates_ref[:, 1:2] * _from_token_tiles(ybuf[buf, 1]).astype(F32))
        out = _layer_norm(ALPHA * x_ref[...] + y, g_ref[...], b_ref[...])

        @pl.when(i < n_tiles_first)
        def _():
            op_ref[...] = out

        @pl.when(i >= n_tiles_first)
        def _():
            os_ref[...] = out

    @pl.when(i == 0)
    def _():
        gather(0, 0)

    @pl.when(i + 1 < n_tiles)
    def _steady():
        wait_all()
        gather(i + 1, 1 - buf)
        compute()

    @pl.when(i + 1 == n_tiles)
    def _last():
        wait_all()
        compute()


def _combine(slots, y_sorted, gates, x_all, g, b, n_rows_first):
    n_rows = x_all.shape[0]
    n_tiles = n_rows // COMBINE_TILE
    n_first = n_rows_first // COMBINE_TILE
    kern = functools.partial(_combine_kernel, n_tiles=n_tiles, n_tiles_first=n_first)
    return pl.pallas_call(
        kern,
        grid_spec=pltpu.PrefetchScalarGridSpec(
            num_scalar_prefetch=1,
            grid=(n_tiles,),
            in_specs=[
                pl.BlockSpec(memory_space=pl.ANY),
                pl.BlockSpec((COMBINE_TILE, 2), lambda i, sl: (i, 0)),
                pl.BlockSpec((COMBINE_TILE, D_MODEL), lambda i, sl: (i, 0)),
                pl.BlockSpec((1, D_MODEL), lambda i, sl: (0, 0)),
                pl.BlockSpec((1, D_MODEL), lambda i, sl: (0, 0)),
            ],
            out_specs=[
                pl.BlockSpec((COMBINE_TILE, D_MODEL), lambda i, sl: (jnp.minimum(i, n_first - 1), 0)),
                pl.BlockSpec((COMBINE_TILE, D_MODEL), lambda i, sl: (jnp.maximum(i - n_first, 0), 0)),
            ],
            scratch_shapes=[pltpu.VMEM((2, 2, COMBINE_TILE * TOKEN_TILE_ROWS, LANES), BF16),
                            pltpu.SemaphoreType.DMA((2,))],
        ),
        out_shape=[jax.ShapeDtypeStruct((n_rows_first, D_MODEL), F32),
                   jax.ShapeDtypeStruct((n_rows - n_rows_first, D_MODEL), F32)],
        compiler_params=pltpu.CompilerParams(dimension_semantics=("arbitrary",),
                                             vmem_limit_bytes=VMEM_LIMIT),
        name="combine",
    )(slots, y_sorted, gates, x_all, g, b)


def _dispatch_plan(routed, counts):
    slots = routed[:, _OUT_SLOT1:_OUT_SLOT2 + 1].astype(jnp.int32).reshape(-1)
    gates = routed[:, _OUT_GATE1:_OUT_GATE2 + 1]
    cnt = counts[0, _EXPERT_LANE0:_EXPERT_LANE0 + N_EXPERTS].astype(jnp.int32)
    tiles = (cnt + MOE_TILE - 1) // MOE_TILE
    tile_end = jnp.cumsum(tiles)
    tile_start = jnp.concatenate([jnp.zeros((1,), jnp.int32), tile_end]).astype(jnp.int32)
    tail_start = (tile_start[:-1] * MOE_TILE + cnt).astype(jnp.int32)
    tail_len = (tiles * MOE_TILE - cnt).astype(jnp.int32)
    return tile_start, tile_end[-1:].astype(jnp.int32), slots, gates, tail_start, tail_len


def kernel(x_prompt, x_sample, mem_prompt, state_pool, state_ret, cache_mem_k, cache_mem_v, w_in, w_pool, b_pool,
           pool_scale, ret_gn_g, ret_gn_b, w_out, ln1_g, ln1_b, w_mq, w_mk, w_mv, w_mo, ln2_g, ln2_b, w_rg, w_re,
           w1, w3, w2, ln3_g, ln3_b):
    assert w_in.shape[0] == DEPTH == 1
    bp_n, tp, _ = x_prompt.shape
    bs_n, ts, _ = x_sample.shape
    rows_p, rows_s = bp_n * tp, bs_n * ts
    assert rows_s == ROW_TILE and rows_p % ROW_TILE == 0
    n_rows = rows_p + rows_s
    tiles_p = rows_p // ROW_TILE

    w_in_b = w_in[0].astype(BF16)
    wp_b = w_pool[0].astype(BF16)
    bp = b_pool[0].reshape(1, POOL_WIDTH)
    ps = pool_scale[0].reshape(1, POOL_WIDTH)
    gng = ret_gn_g[0].reshape(1, RET_WIDTH)
    gnb = ret_gn_b[0].reshape(1, RET_WIDTH)
    row = lambda p: p[0].reshape(1, D_MODEL)

    xp2d = x_prompt.reshape(rows_p, D_MODEL)
    xs2d = x_sample.reshape(rows_s, D_MODEL)

    mem2d = mem_prompt.reshape(bp_n * N_MEM, D_MODEL)
    mk_p = _matmul(mem2d, w_mk, F32, "mem_k").reshape(bp_n, N_MEM, D_MODEL)
    mv_p = _matmul(mem2d, w_mv, F32, "mem_v").reshape(bp_n, N_MEM, D_MODEL)

    zeros_s = jnp.zeros((bp_n, RET_HEADS, RET_HEAD_DIM, RET_HEAD_DIM), F32)
    zeros_h = jnp.zeros((bp_n, POOL_HIST, POOL_WIDTH), F32)
    cat_p, ret_p, pool_p = _mixer(xp2d, w_in_b, wp_b, bp, ps, gng, gnb, zeros_s, zeros_h,
                                  n_streams=bp_n, t_len=tp, tile_rows=MIXER_TILE, chunk=MIXER_TILE, pos0=0)
    cat_s, ret_s, pool_s = _mixer(xs2d, w_in_b, wp_b, bp, ps, gng, gnb, state_ret[0], state_pool[0],
                                  n_streams=bs_n, t_len=ts, tile_rows=rows_s, chunk=ts, pos0=PAST_LEN)
    (x1,) = _mm_ln([cat_p, cat_s], w_out, [xp2d, xs2d], row(ln1_g), row(ln1_b),
                   n_rows=n_rows, n_tiles_first=tiles_p, name="out_ln1")

    q_all = _matmul(x1, w_mq, BF16, "mem_q")
    o_p = _attention(q_all, mk_p, mv_p, n_streams=bp_n, t_len=tp, q_rows=ROW_TILE, row_offset=0, name="attn_prompt")
    o_s = _attention(q_all, cache_mem_k, cache_mem_v,
                     n_streams=bs_n, t_len=ts, q_rows=ts, row_offset=rows_p, name="attn_sample")
    x2, x2_tiles = _mm_ln([o_p, o_s], w_mo, [x1], row(ln2_g), row(ln2_b),
                          n_rows=n_rows, n_tiles_first=tiles_p, name="mo_ln2", emit_packed=True)

    w_router = jnp.concatenate([w_rg[0], w_re[0].reshape(D_MODEL, N_EXPERTS),
                                jnp.zeros((D_MODEL, LANES - N_GROUPS - N_EXPERTS), F32)], axis=1)
    routed, counts = _router(x2, w_router)
    n_tiles = (2 * n_rows) // MOE_TILE + N_EXPERTS
    tile_start, n_active, slots, gates, tail_start, tail_len = _dispatch_plan(routed, counts)
    x_sorted = _dispatch(slots, tail_start, tail_len, x2_tiles, n_tiles)
    y_sorted = _experts(tile_start, n_active, x_sorted, w1, w3, w2, n_tiles)
    y_p, y_s = _combine(slots, y_sorted, gates, x2, row(ln3_g), row(ln3_b), rows_p)

    kv_shape = (DEPTH, bp_n, N_MEM, MEM_HEADS, MEM_HEAD_DIM)
    return (y_p.reshape(bp_n, tp, D_MODEL), y_s.reshape(bs_n, ts, D_MODEL), pool_p[None], ret_p[None],
            mk_p.reshape(kv_shape), mv_p.reshape(kv_shape), pool_s[None], ret_s[None])
```

```python
import functools
import math

import jax
import jax.numpy as jnp
from jax import lax
from jax.experimental import pallas as pl
from jax.experimental.pallas import tpu as pltpu

F32 = jnp.float32
BF16 = jnp.bfloat16

D_MODEL = 2048
POOL_WIDTH = 1024
POOL_WINDOWS = (2, 4, 8, 16)
POOL_CH = 256
POOL_HIST = 15
RET_WIDTH = 1024
RET_HEADS = 8
RET_HEAD_DIM = 128
IN_WIDTH = POOL_WIDTH + 4 * RET_WIDTH
ROPE_BASE = 10000.0
N_MEM = 256
MEM_HEADS = 4
MEM_HEAD_DIM = 512
N_GROUPS = 4
EXPERTS_PER_GROUP = 8
N_EXPERTS = N_GROUPS * EXPERTS_PER_GROUP
EXPERT_HIDDEN = 512
LN_EPS = 1e-5
GN_EPS = 1e-6
DEPTH = 1
ALPHA = (2.0 * DEPTH) ** 0.25
PAST_LEN = 2048
PROMPT_CHUNK = 64

LANES = 128
HIST_PAD = 16
ROW_TILE = 512
MIXER_TILE = 256
MOE_TILE = 256
COMBINE_TILE = 256
VMEM_LIMIT = 58 * 1024 * 1024

_NEG_INF = float("-inf")


def _const_spec(shape):
    zeros = (0,) * len(shape)
    return pl.BlockSpec(shape, lambda *_: zeros, pipeline_mode=pl.Buffered(1))


def _when(cond, fn):
    if cond is True:
        fn()
    else:
        pl.when(cond)(fn)


def _layer_norm(z, g, b):
    mu = jnp.mean(z, axis=-1, keepdims=True)
    zc = z - mu
    var = jnp.mean(zc * zc, axis=-1, keepdims=True)
    return zc * lax.rsqrt(var + LN_EPS) * g + b


def _silu(a):
    return a * (1.0 / (1.0 + jnp.exp(-a)))


TOKEN_TILE_ROWS = D_MODEL // LANES


def _to_token_tiles(x):
    rows = x.shape[0]
    return x.astype(BF16).reshape(rows, TOKEN_TILE_ROWS, LANES).reshape(rows * TOKEN_TILE_ROWS, LANES)


def _from_token_tiles(tiles):
    rows = tiles.shape[0] // TOKEN_TILE_ROWS
    return tiles.reshape(rows, TOKEN_TILE_ROWS, LANES).reshape(rows, D_MODEL)


def _mix_segment(h_ref, r0, cos_ref, sin_ref, dec_ref, kd_ref, qd_ref, wp_ref, bp_ref, ps_ref, gng_ref, gnb_ref,
                 s0_ref, h0_ref, cat_ref, snew_ref, hnew_ref, s_ref, u_ref,
                 *, seg_len, chunk, pos_start, first, g_chunk):
    hist = jnp.concatenate([jnp.zeros((1, POOL_WIDTH), F32), h0_ref[0]], axis=0)
    if first is True:
        s_ref[...] = s0_ref[0]
        u_ref[0:HIST_PAD, :] = hist
    else:
        s_ref[...] = jnp.where(first, s0_ref[0], s_ref[...])
        u_ref[0:HIST_PAD, :] = jnp.where(first, hist, u_ref[0:HIST_PAD, :])

    u_new = h_ref[pl.ds(r0, seg_len), 0:POOL_WIDTH]
    u_ref[HIST_PAD:HIST_PAD + seg_len, :] = u_new
    pos = (pos_start + lax.broadcasted_iota(jnp.int32, (seg_len, 1), 0)).astype(F32)
    for gi, w in enumerate(POOL_WINDOWS):
        cols = slice(gi * POOL_CH, (gi + 1) * POOL_CH)
        win = u_ref[HIST_PAD:HIST_PAD + seg_len, cols]
        for back in range(1, w):
            win = win + u_ref[HIST_PAD - back:HIST_PAD - back + seg_len, cols]
        cnt = jnp.minimum(float(w), pos + 1.0)
        d = win * (1.0 / cnt) - u_ref[HIST_PAD:HIST_PAD + seg_len, cols]
        pooled = jnp.dot(d.astype(BF16), wp_ref[gi], preferred_element_type=F32) + bp_ref[:, cols]
        cat_ref[:, cols] = (pooled * ps_ref[:, cols]).astype(BF16)

    hnew_ref[0] = u_ref[seg_len + 1:seg_len + HIST_PAD, :]
    u_ref[0:HIST_PAD, :] = u_ref[seg_len:seg_len + HIST_PAD, :]

    scale = RET_HEAD_DIM ** -0.5
    for c in range(seg_len // chunk):
        rows = pl.ds(r0 + c * chunk, chunk)
        trows = slice(c * chunk, (c + 1) * chunk)
        cos_t = cos_ref[trows, :]
        sin_t = sin_ref[trows, :]
        for hd in range(RET_HEADS):
            lo = hd * RET_HEAD_DIM
            hcols = slice(lo, lo + RET_HEAD_DIM)
            q = h_ref[rows, POOL_WIDTH + lo:POOL_WIDTH + lo + RET_HEAD_DIM]
            k = h_ref[rows, POOL_WIDTH + RET_WIDTH + lo:POOL_WIDTH + RET_WIDTH + lo + RET_HEAD_DIM]
            v = h_ref[rows, POOL_WIDTH + 2 * RET_WIDTH + lo:POOL_WIDTH + 2 * RET_WIDTH + lo + RET_HEAD_DIM]
            gate = h_ref[rows, POOL_WIDTH + 3 * RET_WIDTH + lo:POOL_WIDTH + 3 * RET_WIDTH + lo + RET_HEAD_DIM]
            qr = q * cos_t + pltpu.roll(q, RET_HEAD_DIM // 2, 1) * sin_t
            kr = (k * cos_t + pltpu.roll(k, RET_HEAD_DIM // 2, 1) * sin_t) * scale
            vb = v.astype(BF16)
            scores = lax.dot_general(qr.astype(BF16), kr.astype(BF16), (((1,), (1,)), ((), ())),
                                     preferred_element_type=F32) * dec_ref[hd]
            o = jnp.dot(scores.astype(BF16), vb, preferred_element_type=F32)
            s_prev = s_ref[hd]
            o = o + jnp.dot((qr * qd_ref[hd]).astype(BF16), s_prev.astype(BF16), preferred_element_type=F32)
            upd = lax.dot_general((kr * kd_ref[hd]).astype(BF16), vb, (((0,), (0,)), ((), ())),
                                  preferred_element_type=F32)
            s_ref[hd] = g_chunk[hd] * s_prev + upd
            mu = jnp.mean(o, axis=-1, keepdims=True)
            oc = o - mu
            var = jnp.mean(oc * oc, axis=-1, keepdims=True)
            on = oc * lax.rsqrt(var + GN_EPS) * gng_ref[:, hcols] + gnb_ref[:, hcols]
            cat_ref[trows, POOL_WIDTH + lo:POOL_WIDTH + lo + RET_HEAD_DIM] = (on * _silu(gate)).astype(BF16)

    snew_ref[0] = s_ref[...]


def _mixer_segments_kernel(x_ref, w_in_ref, *refs, seg_len, chunk, pos0, g_chunk):
    (*mix_refs, h_ref, s_ref, u_ref) = refs
    j = pl.program_id(0)

    @pl.when(j == 0)
    def _project():
        h_ref[...] = jnp.dot(x_ref[...].astype(BF16), w_in_ref[...], preferred_element_type=F32)

    _mix_segment(h_ref, pl.multiple_of(j * seg_len, seg_len), *mix_refs, s_ref, u_ref,
                 seg_len=seg_len, chunk=chunk, pos_start=pos0, first=True, g_chunk=g_chunk)


def _mixer_pipelined_kernel(x_ref, w_in_hbm, *refs, tiles_per_stream, seg_len, chunk, pos0, g_chunk):
    (*mix_refs, w_bf16_hbm, ha_ref, hb_ref, s_ref, u_ref, wb_ref, wsem, osem) = refs
    g = pl.program_id(0)
    m = jnp.maximum(g - 1, 0)
    first = ((m % tiles_per_stream) == 0) | (g == 0)
    pos_start = pos0 + (m % tiles_per_stream) * seg_len
    w_out_copy = pltpu.make_async_copy(wb_ref, w_bf16_hbm, osem)

    @pl.when(g == 0)
    def _stage_weight():
        stage = (ha_ref, hb_ref)
        chunk_rows = ha_ref.shape[0]
        n_chunks = wb_ref.shape[0] // chunk_rows
        copies = [pltpu.make_async_copy(w_in_hbm.at[0, pl.ds(c * chunk_rows, chunk_rows), :], stage[c % 2],
                                        wsem.at[c % 2]) for c in range(n_chunks)]
        for c in range(min(2, n_chunks)):
            copies[c].start()
        for c in range(n_chunks):
            copies[c].wait()
            wb_ref[pl.ds(c * chunk_rows, chunk_rows), :] = stage[c % 2][...].astype(BF16)
            if c + 2 < n_chunks:
                copies[c + 2].start()
        w_out_copy.start()
        hb_ref[...] = jnp.zeros_like(hb_ref)

    def step(h_write, h_read):
        _mix_segment(h_read, 0, *mix_refs, s_ref, u_ref, seg_len=seg_len, chunk=chunk, pos_start=pos_start,
                     first=first, g_chunk=g_chunk)
        h_write[...] = jnp.dot(x_ref[...].astype(BF16), wb_ref[...], preferred_element_type=F32)

    pl.when(g % 2 == 0)(lambda: step(ha_ref, hb_ref))
    pl.when(g % 2 == 1)(lambda: step(hb_ref, ha_ref))

    @pl.when(g == pl.num_programs(0) - 1)
    def _():
        w_out_copy.wait()


def _retention_tables(chunk, t_len, pos0):
    log_gamma = jnp.log(1.0 - 2.0 ** (-5.0 - jnp.arange(RET_HEADS, dtype=F32)))
    idx = jnp.arange(chunk, dtype=F32)
    diff = idx[:, None] - idx[None, :]
    dec = jnp.where(diff >= 0, jnp.exp(log_gamma[:, None, None] * jnp.maximum(diff, 0.0)), 0.0)
    kd = jnp.exp(log_gamma[:, None] * (chunk - 1.0 - idx)[None, :])
    qd = jnp.exp(log_gamma[:, None] * (idx + 1.0)[None, :])
    kd = jnp.broadcast_to(kd[:, :, None], (RET_HEADS, chunk, RET_HEAD_DIM))
    qd = jnp.broadcast_to(qd[:, :, None], (RET_HEADS, chunk, RET_HEAD_DIM))
    half = RET_HEAD_DIM // 2
    freqs = ROPE_BASE ** (-jnp.arange(half, dtype=F32) / half)
    pos = pos0 + jnp.arange(t_len, dtype=F32)
    ang = pos[:, None] * freqs[None, :]
    cos = jnp.cos(ang)
    sin = jnp.sin(ang)
    cos_t = jnp.concatenate([cos, cos], axis=-1)
    sin_t = jnp.concatenate([-sin, sin], axis=-1)
    g_chunk = tuple(math.exp(math.log(1.0 - 2.0 ** (-5.0 - h)) * chunk) for h in range(RET_HEADS))
    return dec, kd, qd, cos_t, sin_t, g_chunk


def _mixer(x2d, w_in, wp_b, bp, ps, gng, gnb, s0, h0, *, n_streams, t_len, tile_rows, chunk, pos0):
    rows = n_streams * t_len
    pipelined = t_len > tile_rows
    seg_len = tile_rows if pipelined else t_len
    assert (t_len % tile_rows == 0) if pipelined else (rows == tile_rows)
    assert seg_len % chunk == 0 and t_len >= POOL_HIST
    dec, kd, qd, cos_t, sin_t, g_chunk = _retention_tables(chunk, t_len, pos0)
    state_block = (1, RET_HEADS, RET_HEAD_DIM, RET_HEAD_DIM)
    hist_block = (1, POOL_HIST, POOL_WIDTH)

    if pipelined:
        n_tiles = rows // tile_rows
        tiles_per_stream = t_len // tile_rows
        grid = (n_tiles + 1,)
        mixed = lambda g: jnp.maximum(g - 1, 0)
        x_map = lambda g: (jnp.minimum(g, n_tiles - 1), 0)
        time_map = lambda g: (mixed(g) % tiles_per_stream, 0)
        cat_map = lambda g: (mixed(g), 0)
        state_map = lambda g: (mixed(g) // tiles_per_stream, 0, 0, 0)
        hist_map = lambda g: (mixed(g) // tiles_per_stream, 0, 0)
        kern = functools.partial(_mixer_pipelined_kernel, tiles_per_stream=tiles_per_stream, seg_len=seg_len,
                                 chunk=chunk, pos0=pos0, g_chunk=g_chunk)
        h_scratch = [pltpu.VMEM((tile_rows, IN_WIDTH), F32), pltpu.VMEM((tile_rows, IN_WIDTH), F32)]
        w_spec = pl.BlockSpec(memory_space=pl.ANY)
        extra_out_specs = [pl.BlockSpec(memory_space=pl.ANY)]
        extra_out_shape = [jax.ShapeDtypeStruct((D_MODEL, IN_WIDTH), BF16)]
        extra_scratch = [pltpu.VMEM((D_MODEL, IN_WIDTH), BF16), pltpu.SemaphoreType.DMA((2,)),
                         pltpu.SemaphoreType.DMA(())]
        assert D_MODEL % tile_rows == 0
    else:
        grid = (n_streams,)
        x_map = lambda j: (0, 0)
        time_map = lambda j: (0, 0)
        cat_map = lambda j: (j, 0)
        state_map = lambda j: (j, 0, 0, 0)
        hist_map = lambda j: (j, 0, 0)
        kern = functools.partial(_mixer_segments_kernel, seg_len=seg_len, chunk=chunk, pos0=pos0, g_chunk=g_chunk)
        h_scratch = [pltpu.VMEM((tile_rows, IN_WIDTH), F32)]
        w_spec = _const_spec((D_MODEL, IN_WIDTH))
        extra_out_specs, extra_out_shape, extra_scratch = [], [], []

    return pl.pallas_call(
        kern,
        grid=grid,
        in_specs=[
            pl.BlockSpec((tile_rows, D_MODEL), x_map),
            w_spec,
            pl.BlockSpec((seg_len, RET_HEAD_DIM), time_map),
            pl.BlockSpec((seg_len, RET_HEAD_DIM), time_map),
            _const_spec((RET_HEADS, chunk, chunk)),
            _const_spec((RET_HEADS, chunk, RET_HEAD_DIM)),
            _const_spec((RET_HEADS, chunk, RET_HEAD_DIM)),
            _const_spec((len(POOL_WINDOWS), POOL_CH, POOL_CH)),
            _const_spec((1, POOL_WIDTH)),
            _const_spec((1, POOL_WIDTH)),
            _const_spec((1, RET_WIDTH)),
            _const_spec((1, RET_WIDTH)),
            pl.BlockSpec(state_block, state_map),
            pl.BlockSpec(hist_block, hist_map),
        ],
        out_specs=[
            pl.BlockSpec((seg_len, D_MODEL), cat_map),
            pl.BlockSpec(state_block, state_map),
            pl.BlockSpec(hist_block, hist_map),
        ] + extra_out_specs,
        out_shape=[
            jax.ShapeDtypeStruct((rows, D_MODEL), BF16),
            jax.ShapeDtypeStruct((n_streams, RET_HEADS, RET_HEAD_DIM, RET_HEAD_DIM), F32),
            jax.ShapeDtypeStruct((n_streams, POOL_HIST, POOL_WIDTH), F32),
        ] + extra_out_shape,
        scratch_shapes=h_scratch + [
            pltpu.VMEM((RET_HEADS, RET_HEAD_DIM, RET_HEAD_DIM), F32),
            pltpu.VMEM((HIST_PAD + seg_len, POOL_WIDTH), F32),
        ] + extra_scratch,
        compiler_params=pltpu.CompilerParams(dimension_semantics=("arbitrary",),
                                             vmem_limit_bytes=VMEM_LIMIT),
        name="mixer",
    )(x2d, w_in, cos_t, sin_t, dec, kd, qd, wp_b, bp, ps, gng, gnb, s0, h0)


def _row_sources(arrays, n_tiles_first):
    if len(arrays) == 1:
        return [pl.BlockSpec((ROW_TILE, arrays[0].shape[1]), lambda i: (i, 0))]
    first, second = arrays
    return [
        pl.BlockSpec((ROW_TILE, first.shape[1]), lambda i: (jnp.minimum(i, n_tiles_first - 1), 0)),
        pl.BlockSpec((ROW_TILE, second.shape[1]), lambda i: (jnp.maximum(i - n_tiles_first, 0), 0),
                     pipeline_mode=pl.Buffered(1)),
    ]


def _mm_ln_kernel(*refs, n_a, n_res, n_tiles_first, emit_packed):
    a_refs = refs[:n_a]
    w_ref = refs[n_a]
    res_refs = refs[n_a + 1:n_a + 1 + n_res]
    g_ref, b_ref, o_ref = refs[n_a + 1 + n_res:n_a + 4 + n_res]
    tiles_ref = refs[n_a + 4 + n_res] if emit_packed else None
    wb_ref = refs[-1]
    i = pl.program_id(0)

    @pl.when(i == 0)
    def _():
        wb_ref[...] = w_ref[0].astype(BF16)

    def body(a_ref, res_ref):
        half = ROW_TILE // 2
        for h in range(2):
            rows = slice(h * half, (h + 1) * half)
            acc = jnp.dot(a_ref[rows, :].astype(BF16), wb_ref[...], preferred_element_type=F32)
            out = _layer_norm(ALPHA * res_ref[rows, :] + acc, g_ref[...], b_ref[...])
            o_ref[rows, :] = out
            if emit_packed:
                tile_rows = slice(h * half * TOKEN_TILE_ROWS, (h + 1) * half * TOKEN_TILE_ROWS)
                tiles_ref[tile_rows, :] = _to_token_tiles(out)

    pl.when(i < n_tiles_first)(lambda: body(a_refs[0], res_refs[0]))
    pl.when(i >= n_tiles_first)(lambda: body(a_refs[-1], res_refs[-1]))


def _mm_ln(a_arrays, w, res_arrays, g, b, *, n_rows, n_tiles_first, name, emit_packed=False):
    n_tiles = n_rows // ROW_TILE
    kern = functools.partial(_mm_ln_kernel, n_a=len(a_arrays), n_res=len(res_arrays),
                             n_tiles_first=n_tiles_first, emit_packed=emit_packed)
    out_specs = [pl.BlockSpec((ROW_TILE, D_MODEL), lambda i: (i, 0))]
    out_shape = [jax.ShapeDtypeStruct((n_rows, D_MODEL), F32)]
    if emit_packed:
        out_specs.append(pl.BlockSpec((ROW_TILE * TOKEN_TILE_ROWS, LANES), lambda i: (i, 0)))
        out_shape.append(jax.ShapeDtypeStruct((n_rows * TOKEN_TILE_ROWS, LANES), BF16))
    return pl.pallas_call(
        kern,
        grid=(n_tiles,),
        in_specs=(_row_sources(a_arrays, n_tiles_first) + [_const_spec(w.shape)]
                  + _row_sources(res_arrays, n_tiles_first)
                  + [_const_spec((1, D_MODEL)), _const_spec((1, D_MODEL))]),
        out_specs=out_specs,
        out_shape=out_shape,
        scratch_shapes=[pltpu.VMEM(w.shape[1:], BF16)],
        compiler_params=pltpu.CompilerParams(dimension_semantics=("arbitrary",),
                                             vmem_limit_bytes=VMEM_LIMIT),
        name=name,
    )(*a_arrays, w, *res_arrays, g, b)


def _matmul_kernel(a_ref, w_ref, o_ref, wb_ref):
    @pl.when(pl.program_id(0) == 0)
    def _():
        wb_ref[...] = w_ref[0].astype(BF16)

    o_ref[...] = jnp.dot(a_ref[...].astype(BF16), wb_ref[...], preferred_element_type=F32).astype(o_ref.dtype)


def _matmul(a, w, out_dtype, name):
    n_rows = a.shape[0]
    return pl.pallas_call(
        _matmul_kernel,
        grid=(n_rows // ROW_TILE,),
        in_specs=[pl.BlockSpec((ROW_TILE, a.shape[1]), lambda i: (i, 0)), _const_spec(w.shape)],
        out_specs=pl.BlockSpec((ROW_TILE, w.shape[2]), lambda i: (i, 0)),
        out_shape=jax.ShapeDtypeStruct((n_rows, w.shape[2]), out_dtype),
        scratch_shapes=[pltpu.VMEM(w.shape[1:], BF16)],
        compiler_params=pltpu.CompilerParams(dimension_semantics=("arbitrary",),
                                             vmem_limit_bytes=VMEM_LIMIT),
        name=name,
    )(a, w)


def _attention_kernel(q_ref, k_ref, v_ref, o_ref):
    scale = MEM_HEAD_DIM ** -0.5
    heads_split = len(k_ref.shape) == 5

    def rows_by_model_dim(ref):
        if heads_split:
            return ref[0, 0].astype(BF16).reshape(N_MEM, D_MODEL)
        return ref[0].astype(BF16)

    k = rows_by_model_dim(k_ref)
    v = rows_by_model_dim(v_ref)
    for h in range(MEM_HEADS):
        cols = slice(h * MEM_HEAD_DIM, (h + 1) * MEM_HEAD_DIM)
        s = lax.dot_general(q_ref[:, cols], k[:, cols], (((1,), (1,)), ((), ())),
                            preferred_element_type=F32) * scale
        m = jnp.max(s, axis=-1, keepdims=True)
        p = jnp.exp(s - m)
        p = p * (1.0 / jnp.sum(p, axis=-1, keepdims=True))
        o_ref[:, cols] = jnp.dot(p.astype(BF16), v[:, cols], preferred_element_type=F32).astype(o_ref.dtype)


def _attention(q_all, mem_k, mem_v, *, n_streams, t_len, q_rows, row_offset, name):
    tiles_per_stream = t_len // q_rows
    base = row_offset // q_rows
    if mem_k.ndim == 5:
        kv_spec = pl.BlockSpec((1, 1, N_MEM, MEM_HEADS, MEM_HEAD_DIM), lambda b, t: (0, b, 0, 0, 0))
    else:
        kv_spec = pl.BlockSpec((1, N_MEM, D_MODEL), lambda b, t: (b, 0, 0))
    return pl.pallas_call(
        _attention_kernel,
        grid=(n_streams, tiles_per_stream),
        in_specs=[
            pl.BlockSpec((q_rows, D_MODEL), lambda b, t: (base + b * tiles_per_stream + t, 0)),
            kv_spec,
            kv_spec,
        ],
        out_specs=pl.BlockSpec((q_rows, D_MODEL), lambda b, t: (b * tiles_per_stream + t, 0)),
        out_shape=jax.ShapeDtypeStruct((n_streams * t_len, D_MODEL), BF16),
        compiler_params=pltpu.CompilerParams(dimension_semantics=("arbitrary", "arbitrary"),
                                             vmem_limit_bytes=VMEM_LIMIT),
        name=name,
    )(q_all, mem_k, mem_v)


_GROUP_LANE0 = 0
_EXPERT_LANE0 = N_GROUPS
(_META_LANE1, _META_LANE2, _META_POS1, _META_POS2, _META_GATE1, _META_GATE2) = range(6)
(_OUT_SLOT1, _OUT_SLOT2, _OUT_GATE1, _OUT_GATE2) = range(4)
_MOE_TILE_LOG2 = MOE_TILE.bit_length() - 1
assert 1 << _MOE_TILE_LOG2 == MOE_TILE


def _router_kernel(x_ref, whl_ref, wh_ref, out_ref, counts_ref, carry_ref, meta_ref, start_ref, earlier_ref):
    phase = pl.program_id(0)
    i = pl.program_id(1)
    rows = x_ref.shape[0]
    tile_rows = pl.ds(pl.multiple_of(i * rows, rows), rows)
    lane = lax.broadcasted_iota(jnp.int32, (rows, LANES), 1)

    @pl.when((phase == 0) & (i == 0))
    def _():
        carry_ref[...] = jnp.zeros_like(carry_ref)
        earlier_ref[...] = (lax.broadcasted_iota(jnp.int32, (rows, rows), 1)
                            < lax.broadcasted_iota(jnp.int32, (rows, rows), 0)).astype(BF16)

    @pl.when(phase == 0)
    def _route():
        x = x_ref[...]
        xh = x.astype(BF16)
        xl = (x - xh.astype(F32)).astype(BF16)
        both = jnp.dot(xh, whl_ref[...], preferred_element_type=F32)
        logits = both[:, :LANES] + jnp.dot(xl, wh_ref[...], preferred_element_type=F32) + both[:, LANES:]

        def first_argmax(vals):
            m = jnp.max(vals, axis=-1, keepdims=True)
            idx = jnp.min(jnp.where(vals == m, lane, LANES), axis=-1, keepdims=True)
            return m, idx

        gl = jnp.where(lane < N_GROUPS, logits, _NEG_INF)
        gm, g_idx = first_argmax(gl)
        g_w = 1.0 / jnp.sum(jnp.exp(gl - gm), axis=-1, keepdims=True)

        in_group = ((lane >= _EXPERT_LANE0) & (lane < _EXPERT_LANE0 + N_EXPERTS)
                    & (((lane - _EXPERT_LANE0) >> 3) == g_idx))
        el = jnp.where(in_group, logits, _NEG_INF)
        m1, i1 = first_argmax(el)
        z = jnp.sum(jnp.exp(el - m1), axis=-1, keepdims=True)
        m2, i2 = first_argmax(jnp.where(lane == i1, _NEG_INF, el))
        p1 = 1.0 / z
        p2 = jnp.exp(m2 - m1) / z
        den = p1 + p2
        gate1 = p1 / den * g_w
        gate2 = p2 / den * g_w

        hit1 = lane == i1
        hit2 = lane == i2
        onehot = (hit1 | hit2).astype(BF16)
        rank = jnp.dot(earlier_ref[...], onehot, preferred_element_type=F32) + carry_ref[...]
        pos1 = jnp.sum(jnp.where(hit1, rank, 0.0), axis=-1, keepdims=True)
        pos2 = jnp.sum(jnp.where(hit2, rank, 0.0), axis=-1, keepdims=True)
        carry_ref[...] = carry_ref[...] + jnp.sum(onehot.astype(F32), axis=0, keepdims=True)

        meta = jnp.zeros((rows, LANES), F32)
        for col, val in ((_META_LANE1, i1.astype(F32)), (_META_LANE2, i2.astype(F32)), (_META_POS1, pos1),
                         (_META_POS2, pos2), (_META_GATE1, gate1), (_META_GATE2, gate2)):
            meta = jnp.where(lane == col, val, meta)
        meta_ref[tile_rows, :] = meta

    @pl.when((phase == 1) & (i == 0))
    def _segment_starts():
        counts_ref[...] = carry_ref[...]
        tiles = ((carry_ref[...].astype(jnp.int32) + (MOE_TILE - 1)) >> _MOE_TILE_LOG2).astype(F32).astype(BF16)
        before = (lax.broadcasted_iota(jnp.int32, (LANES, LANES), 0)
                  < lax.broadcasted_iota(jnp.int32, (LANES, LANES), 1)).astype(BF16)
        tiles8 = jnp.broadcast_to(tiles, (8, LANES))
        start_ref[...] = jnp.dot(tiles8, before, preferred_element_type=F32)[0:1, :] * float(MOE_TILE)

    @pl.when(phase == 1)
    def _slots():
        meta = meta_ref[tile_rows, :]
        starts = start_ref[...]

        def col(c):
            return jnp.sum(jnp.where(lane == c, meta, 0.0), axis=-1, keepdims=True)

        def start_of(expert_lane):
            return jnp.sum(jnp.where(lane == expert_lane.astype(jnp.int32), starts, 0.0), axis=-1, keepdims=True)

        slot1 = start_of(col(_META_LANE1)) + col(_META_POS1)
        slot2 = start_of(col(_META_LANE2)) + col(_META_POS2)
        out = jnp.zeros((rows, LANES), F32)
        for c, val in ((_OUT_SLOT1, slot1), (_OUT_SLOT2, slot2), (_OUT_GATE1, col(_META_GATE1)),
                       (_OUT_GATE2, col(_META_GATE2))):
            out = jnp.where(lane == c, val, out)
        out_ref[...] = out


def _router(x_all, w_router):
    n_rows = x_all.shape[0]
    n_tiles = n_rows // ROW_TILE
    w_hi = w_router.astype(BF16)
    w_lo = (w_router - w_hi.astype(F32)).astype(BF16)
    w_hi_lo = jnp.concatenate([w_hi, w_lo], axis=1)
    return pl.pallas_call(
        _router_kernel,
        grid=(2, n_tiles),
        in_specs=[pl.BlockSpec((ROW_TILE, D_MODEL), lambda p, i: (i * (1 - p) + (n_tiles - 1) * p, 0)),
                  _const_spec((D_MODEL, 2 * LANES)), _const_spec((D_MODEL, LANES))],
        out_specs=[pl.BlockSpec((ROW_TILE, LANES), lambda p, i: (i * p, 0)),
                   pl.BlockSpec((1, LANES), lambda p, i: (0, 0))],
        out_shape=[jax.ShapeDtypeStruct((n_rows, LANES), F32), jax.ShapeDtypeStruct((1, LANES), F32)],
        scratch_shapes=[pltpu.VMEM((1, LANES), F32), pltpu.VMEM((n_rows, LANES), F32), pltpu.VMEM((1, LANES), F32),
                        pltpu.VMEM((ROW_TILE, ROW_TILE), BF16)],
        compiler_params=pltpu.CompilerParams(dimension_semantics=("arbitrary", "arbitrary"),
                                             vmem_limit_bytes=VMEM_LIMIT),
        name="router",
    )(x_all, w_hi_lo, w_hi)


def _token_rows(first_token, n_tokens):
    return pl.ds(pl.multiple_of(first_token * TOKEN_TILE_ROWS, TOKEN_TILE_ROWS), n_tokens * TOKEN_TILE_ROWS)


_TAIL_BITS = MOE_TILE.bit_length() - 1


def _dispatch_kernel(slots_ref, tail_start_ref, tail_len_ref, x_ref, xs_hbm, zeros_ref, sem, zsem):
    i = pl.program_id(0)
    n_tokens = x_ref.shape[0] // TOKEN_TILE_ROWS
    group = 16

    def issue(j, carry):
        for u in range(group):
            r = j * group + u
            src = x_ref.at[pl.ds(pl.multiple_of(r * TOKEN_TILE_ROWS, TOKEN_TILE_ROWS), TOKEN_TILE_ROWS), :]
            for k in range(2):
                slot = slots_ref[2 * (i * n_tokens + r) + k]
                pltpu.make_async_copy(src, xs_hbm.at[_token_rows(slot, 1), :], sem).start(priority=k)
        return carry
    lax.fori_loop(0, n_tokens // group, issue, 0)

    def drain(j, carry):
        pltpu.make_async_copy(x_ref.at[pl.ds(0, 2 * group * TOKEN_TILE_ROWS), :],
                              xs_hbm.at[_token_rows(0, 2 * group), :], sem).wait()
        return carry
    lax.fori_loop(0, n_tokens // group, drain, 0)

    def tail_copies(fn):
        def per_expert(ex, carry):
            length = tail_len_ref[ex]
            pos = tail_start_ref[ex]
            for bit in reversed(range(_TAIL_BITS)):
                piece = 1 << bit
                take = (length & piece) != 0
                cp = pltpu.make_async_copy(zeros_ref.at[pl.ds(0, piece * TOKEN_TILE_ROWS), :],
                                           xs_hbm.at[_token_rows(pos, piece), :], zsem)
                pl.when(take)(lambda cp=cp: fn(cp))
                pos = pos + jnp.where(take, piece, 0)
            return carry
        lax.fori_loop(0, N_EXPERTS, per_expert, 0)

    def unused_copies(fn):
        half = zeros_ref.shape[0] // TOKEN_TILE_ROWS
        first_unused = tail_start_ref[N_EXPERTS - 1] + tail_len_ref[N_EXPERTS - 1]

        def per_half(j, carry):
            fn(pltpu.make_async_copy(zeros_ref, xs_hbm.at[_token_rows(first_unused + j * half, half), :], zsem))
            return carry
        lax.fori_loop(0, (xs_hbm.shape[0] // TOKEN_TILE_ROWS - first_unused) // half, per_half, 0)

    @pl.when(i == pl.num_programs(0) - 1)
    def _tails():
        zeros_ref[...] = jnp.zeros_like(zeros_ref)
        tail_copies(lambda cp: cp.start())
        unused_copies(lambda cp: cp.start())
        tail_copies(lambda cp: cp.wait())
        unused_copies(lambda cp: cp.wait())


def _dispatch(slots, tail_start, tail_len, xg, n_tiles_max):
    block_rows = ROW_TILE * TOKEN_TILE_ROWS
    assert xg.shape[0] % block_rows == 0
    return pl.pallas_call(
        _dispatch_kernel,
        grid_spec=pltpu.PrefetchScalarGridSpec(
            num_scalar_prefetch=3,
            grid=(xg.shape[0] // block_rows,),
            in_specs=[pl.BlockSpec((block_rows, LANES), lambda i, sl, ts, tl: (i, 0))],
            out_specs=pl.BlockSpec(memory_space=pl.ANY),
            scratch_shapes=[pltpu.VMEM((MOE_TILE // 2 * TOKEN_TILE_ROWS, LANES), BF16),
                            pltpu.SemaphoreType.DMA(()), pltpu.SemaphoreType.DMA(())],
        ),
        out_shape=jax.ShapeDtypeStruct((n_tiles_max * MOE_TILE * TOKEN_TILE_ROWS, LANES), BF16),
        compiler_params=pltpu.CompilerParams(dimension_semantics=("arbitrary",)),
        name="dispatch",
    )(slots, tail_start, tail_len, xg)


def _experts_kernel(tile_start_ref, n_active_ref, xs_hbm, w1_ref, w3_ref, w2_ref, y_hbm,
                    xbuf, ybuf, w1b, w3b, w2b, gsem, ysem, *, n_tiles_max):
    e = pl.program_id(0)
    n_active = n_active_ref[0]

    def x_copy(tile, buf):
        return pltpu.make_async_copy(xs_hbm.at[_token_rows(tile * MOE_TILE, MOE_TILE), :], xbuf.at[buf], gsem.at[buf])

    def y_copy(tile, buf):
        return pltpu.make_async_copy(ybuf.at[buf], y_hbm.at[_token_rows(tile * MOE_TILE, MOE_TILE), :], ysem.at[buf])

    @pl.when(e == 0)
    def _():
        x_copy(0, 0).start()

    w1b[...] = w1_ref[0, 0, 0].astype(BF16)
    w3b[...] = w3_ref[0, 0, 0].astype(BF16)
    w2b[...] = w2_ref[0, 0, 0].astype(BF16)
    first_tile = tile_start_ref[e]

    def tile_body(t, carry):
        g = first_tile + t
        buf = g % 2
        x_copy(g, buf).wait()

        @pl.when(g >= 2)
        def _():
            y_copy(g, buf).wait()

        @pl.when(g + 1 < n_active)
        def _():
            x_copy(g + 1, 1 - buf).start()

        x = _from_token_tiles(xbuf[buf])
        a = jnp.dot(x, w1b[...], preferred_element_type=F32)
        b = jnp.dot(x, w3b[...], preferred_element_type=F32)
        hdn = (_silu(a) * b).astype(BF16)
        y = jnp.dot(hdn, w2b[...], preferred_element_type=F32)
        ybuf[buf] = _to_token_tiles(y)
        y_copy(g, buf).start()
        return carry

    lax.fori_loop(0, tile_start_ref[e + 1] - first_tile, tile_body, 0)

    @pl.when(e == N_EXPERTS - 1)
    def _drain():
        @pl.when(n_active >= 2)
        def _():
            y_copy(0, n_active % 2).wait()

        @pl.when(n_active >= 1)
        def _():
            y_copy(0, (n_active + 1) % 2).wait()

        ybuf[0] = jnp.zeros(ybuf.shape[1:], ybuf.dtype)

        def fill(g, carry):
            cp = y_copy(g, 0)
            cp.start()
            cp.wait()
            return carry
        lax.fori_loop(n_active, n_tiles_max, fill, 0)


def _experts(tile_start, n_active, x_sorted, w1, w3, w2, n_tiles_max):
    def expert_block(e, ts, na):
        return (0, e // EXPERTS_PER_GROUP, e % EXPERTS_PER_GROUP, 0, 0)

    tile_words = MOE_TILE * TOKEN_TILE_ROWS
    kern = functools.partial(_experts_kernel, n_tiles_max=n_tiles_max)
    return pl.pallas_call(
        kern,
        grid_spec=pltpu.PrefetchScalarGridSpec(
            num_scalar_prefetch=2,
            grid=(N_EXPERTS,),
            in_specs=[
                pl.BlockSpec(memory_space=pl.ANY),
                pl.BlockSpec((1, 1, 1, D_MODEL, EXPERT_HIDDEN), expert_block),
                pl.BlockSpec((1, 1, 1, D_MODEL, EXPERT_HIDDEN), expert_block),
                pl.BlockSpec((1, 1, 1, EXPERT_HIDDEN, D_MODEL), expert_block),
            ],
            out_specs=pl.BlockSpec(memory_space=pl.ANY),
            scratch_shapes=[
                pltpu.VMEM((2, tile_words, LANES), BF16),
                pltpu.VMEM((2, tile_words, LANES), BF16),
                pltpu.VMEM((D_MODEL, EXPERT_HIDDEN), BF16),
                pltpu.VMEM((D_MODEL, EXPERT_HIDDEN), BF16),
                pltpu.VMEM((EXPERT_HIDDEN, D_MODEL), BF16),
                pltpu.SemaphoreType.DMA((2,)),
                pltpu.SemaphoreType.DMA((2,)),
            ],
        ),
        out_shape=jax.ShapeDtypeStruct((n_tiles_max * tile_words, LANES), BF16),
        compiler_params=pltpu.CompilerParams(dimension_semantics=("arbitrary",),
                                             vmem_limit_bytes=VMEM_LIMIT),
        name="experts",
    )(tile_start, n_active, x_sorted, w1, w3, w2)


def _combine_kernel(slot_ref, y_hbm, gates_ref, x_ref, g_ref, b_ref, op_ref, os_ref, ybuf, sem,
                    *, n_tiles, n_tiles_first):
    i = pl.program_id(0)

    tile_words = COMBINE_TILE * TOKEN_TILE_ROWS

    def gather(tile, buf):
        for r in range(COMBINE_TILE):
            tok = tile * COMBINE_TILE + r
            for k in range(2):
                row0 = pl.multiple_of(slot_ref[2 * tok + k] * TOKEN_TILE_ROWS, TOKEN_TILE_ROWS)
                pltpu.make_async_copy(y_hbm.at[pl.ds(row0, TOKEN_TILE_ROWS), :],
                                      ybuf.at[buf, k, pl.ds(r * TOKEN_TILE_ROWS, TOKEN_TILE_ROWS), :],
                                      sem.at[buf]).start(priority=k)

    buf = i % 2

    def wait_all():
        for k in range(2):
            pltpu.make_async_copy(y_hbm.at[pl.ds(0, tile_words), :], ybuf.at[buf, k], sem.at[buf]).wait()

    def compute():
        y = (gates_ref[:, 0:1] * _from_token_tiles(ybuf[buf, 0]).astype(F32)
             + gates_ref[:, 1:2] * _from_token_tiles(ybuf[buf, 1]).astype(F32))
        out = _layer_norm(ALPHA * x_ref[...] + y, g_ref[...], b_ref[...])

        @pl.when(i < n_tiles_first)
        def _():
            op_ref[...] = out

        @pl.when(i >= n_tiles_first)
        def _():
            os_ref[...] = out

    @pl.when(i == 0)
    def _():
        gather(0, 0)

    @pl.when(i + 1 < n_tiles)
    def _steady():
        wait_all()
        gather(i + 1, 1 - buf)
        compute()

    @pl.when(i + 1 == n_tiles)
    def _last():
        wait_all()
        compute()


def _combine(slots, y_sorted, gates, x_all, g, b, n_rows_first):
    n_rows = x_all.shape[0]
    n_tiles = n_rows // COMBINE_TILE
    n_first = n_rows_first // COMBINE_TILE
    kern = functools.partial(_combine_kernel, n_tiles=n_tiles, n_tiles_first=n_first)
    return pl.pallas_call(
        kern,
        grid_spec=pltpu.PrefetchScalarGridSpec(
            num_scalar_prefetch=1,
            grid=(n_tiles,),
            in_specs=[
                pl.BlockSpec(memory_space=pl.ANY),
                pl.BlockSpec((COMBINE_TILE, 2), lambda i, sl: (i, 0)),
                pl.BlockSpec((COMBINE_TILE, D_MODEL), lambda i, sl: (i, 0)),
                pl.BlockSpec((1, D_MODEL), lambda i, sl: (0, 0)),
                pl.BlockSpec((1, D_MODEL), lambda i, sl: (0, 0)),
            ],
            out_specs=[
                pl.BlockSpec((COMBINE_TILE, D_MODEL), lambda i, sl: (jnp.minimum(i, n_first - 1), 0)),
                pl.BlockSpec((COMBINE_TILE, D_MODEL), lambda i, sl: (jnp.maximum(i - n_first, 0), 0)),
            ],
            scratch_shapes=[pltpu.VMEM((2, 2, COMBINE_TILE * TOKEN_TILE_ROWS, LANES), BF16),
                            pltpu.SemaphoreType.DMA((2,))],
        ),
        out_shape=[jax.ShapeDtypeStruct((n_rows_first, D_MODEL), F32),
                   jax.ShapeDtypeStruct((n_rows - n_rows_first, D_MODEL), F32)],
        compiler_params=pltpu.CompilerParams(dimension_semantics=("arbitrary",),
                                             vmem_limit_bytes=VMEM_LIMIT),
        name="combine",
    )(slots, y_sorted, gates, x_all, g, b)


def _dispatch_plan(routed, counts):
    slots = routed[:, _OUT_SLOT1:_OUT_SLOT2 + 1].astype(jnp.int32).reshape(-1)
    gates = routed[:, _OUT_GATE1:_OUT_GATE2 + 1]
    cnt = counts[0, _EXPERT_LANE0:_EXPERT_LANE0 + N_EXPERTS].astype(jnp.int32)
    tiles = (cnt + MOE_TILE - 1) // MOE_TILE
    tile_end = jnp.cumsum(tiles)
    tile_start = jnp.concatenate([jnp.zeros((1,), jnp.int32), tile_end]).astype(jnp.int32)
    tail_start = (tile_start[:-1] * MOE_TILE + cnt).astype(jnp.int32)
    tail_len = (tiles * MOE_TILE - cnt).astype(jnp.int32)
    return tile_start, tile_end[-1:].astype(jnp.int32), slots, gates, tail_start, tail_len


def kernel(x_prompt, x_sample, mem_prompt, state_pool, state_ret, cache_mem_k, cache_mem_v, w_in, w_pool, b_pool,
           pool_scale, ret_gn_g, ret_gn_b, w_out, ln1_g, ln1_b, w_mq, w_mk, w_mv, w_mo, ln2_g, ln2_b, w_rg, w_re,
           w1, w3, w2, ln3_g, ln3_b):
    assert w_in.shape[0] == DEPTH == 1
    bp_n, tp, _ = x_prompt.shape
    bs_n, ts, _ = x_sample.shape
    rows_p, rows_s = bp_n * tp, bs_n * ts
    assert rows_s == ROW_TILE and rows_p % ROW_TILE == 0
    n_rows = rows_p + rows_s
    tiles_p = rows_p // ROW_TILE

    wp_b = w_pool[0].astype(BF16)
    bp = b_pool[0].reshape(1, POOL_WIDTH)
    ps = pool_scale[0].reshape(1, POOL_WIDTH)
    gng = ret_gn_g[0].reshape(1, RET_WIDTH)
    gnb = ret_gn_b[0].reshape(1, RET_WIDTH)
    row = lambda p: p[0].reshape(1, D_MODEL)

    xp2d = x_prompt.reshape(rows_p, D_MODEL)
    xs2d = x_sample.reshape(rows_s, D_MODEL)

    mem2d = mem_prompt.reshape(bp_n * N_MEM, D_MODEL)
    mk_p = _matmul(mem2d, w_mk, F32, "mem_k").reshape(bp_n, N_MEM, D_MODEL)
    mv_p = _matmul(mem2d, w_mv, F32, "mem_v").reshape(bp_n, N_MEM, D_MODEL)

    zeros_s = jnp.zeros((bp_n, RET_HEADS, RET_HEAD_DIM, RET_HEAD_DIM), F32)
    zeros_h = jnp.zeros((bp_n, POOL_HIST, POOL_WIDTH), F32)
    cat_p, ret_p, pool_p, w_in_b = _mixer(xp2d, w_in, wp_b, bp, ps, gng, gnb, zeros_s, zeros_h,
                                          n_streams=bp_n, t_len=tp, tile_rows=MIXER_TILE, chunk=MIXER_TILE, pos0=0)
    cat_s, ret_s, pool_s = _mixer(xs2d, w_in_b, wp_b, bp, ps, gng, gnb, state_ret[0], state_pool[0],
                                  n_streams=bs_n, t_len=ts, tile_rows=rows_s, chunk=ts, pos0=PAST_LEN)
    (x1,) = _mm_ln([cat_p, cat_s], w_out, [xp2d, xs2d], row(ln1_g), row(ln1_b),
                   n_rows=n_rows, n_tiles_first=tiles_p, name="out_ln1")

    q_all = _matmul(x1, w_mq, BF16, "mem_q")
    o_p = _attention(q_all, mk_p, mv_p, n_streams=bp_n, t_len=tp, q_rows=ROW_TILE, row_offset=0, name="attn_prompt")
    o_s = _attention(q_all, cache_mem_k, cache_mem_v,
                     n_streams=bs_n, t_len=ts, q_rows=ts, row_offset=rows_p, name="attn_sample")
    x2, x2_tiles = _mm_ln([o_p, o_s], w_mo, [x1], row(ln2_g), row(ln2_b),
                          n_rows=n_rows, n_tiles_first=tiles_p, name="mo_ln2", emit_packed=True)

    w_router = jnp.concatenate([w_rg[0], w_re[0].reshape(D_MODEL, N_EXPERTS),
                                jnp.zeros((D_MODEL, LANES - N_GROUPS - N_EXPERTS), F32)], axis=1)
    routed, counts = _router(x2, w_router)
    n_tiles = (2 * n_rows) // MOE_TILE + N_EXPERTS
    tile_start, n_active, slots, gates, tail_start, tail_len = _dispatch_plan(routed, counts)
    x_sorted = _dispatch(slots, tail_start, tail_len, x2_tiles, n_tiles)
    y_sorted = _experts(tile_start, n_active, x_sorted, w1, w3, w2, n_tiles)
    y_p, y_s = _combine(slots, y_sorted, gates, x2, row(ln3_g), row(ln3_b), rows_p)

    kv_shape = (DEPTH, bp_n, N_MEM, MEM_HEADS, MEM_HEAD_DIM)
    return (y_p.reshape(bp_n, tp, D_MODEL), y_s.reshape(bs_n, ts, D_MODEL), pool_p[None], ret_p[None],
            mk_p.reshape(kv_shape), mv_p.reshape(kv_shape), pool_s[None], ret_s[None])
```

```python
import functools
import math

import jax
import jax.numpy as jnp
from jax import lax
from jax.experimental import pallas as pl
from jax.experimental.pallas import tpu as pltpu

F32 = jnp.float32
BF16 = jnp.bfloat16

D_MODEL = 2048
POOL_WIDTH = 1024
POOL_WINDOWS = (2, 4, 8, 16)
POOL_CH = 256
POOL_HIST = 15
RET_WIDTH = 1024
RET_HEADS = 8
RET_HEAD_DIM = 128
IN_WIDTH = POOL_WIDTH + 4 * RET_WIDTH
ROPE_BASE = 10000.0
N_MEM = 256
MEM_HEADS = 4
MEM_HEAD_DIM = 512
N_GROUPS = 4
EXPERTS_PER_GROUP = 8
N_EXPERTS = N_GROUPS * EXPERTS_PER_GROUP
EXPERT_HIDDEN = 512
LN_EPS = 1e-5
GN_EPS = 1e-6
DEPTH = 1
ALPHA = (2.0 * DEPTH) ** 0.25
PAST_LEN = 2048

LANES = 128
HIST_PAD = 16
ROW_TILE = 512
MIXER_TILE = 256
MOE_TILE = 256
COMBINE_TILE = 256
DISPATCH_STEPS = 4
VMEM_LIMIT = 58 * 1024 * 1024

_NEG_INF = float("-inf")


def _const_spec(shape):
    zeros = (0,) * len(shape)
    return pl.BlockSpec(shape, lambda *_: zeros, pipeline_mode=pl.Buffered(1))


def _when(cond, fn):
    if cond is True:
        fn()
    else:
        pl.when(cond)(fn)


def _layer_norm(z, g, b):
    mu = jnp.mean(z, axis=-1, keepdims=True)
    zc = z - mu
    var = jnp.mean(zc * zc, axis=-1, keepdims=True)
    return zc * lax.rsqrt(var + LN_EPS) * g + b


def _silu(a):
    return a * (1.0 / (1.0 + jnp.exp(-a)))


TOKEN_TILE_ROWS = D_MODEL // LANES


def _to_token_tiles(x):
    rows = x.shape[0]
    return x.astype(BF16).reshape(rows, TOKEN_TILE_ROWS, LANES).reshape(rows * TOKEN_TILE_ROWS, LANES)


def _from_token_tiles(tiles):
    rows = tiles.shape[0] // TOKEN_TILE_ROWS
    return tiles.reshape(rows, TOKEN_TILE_ROWS, LANES).reshape(rows, D_MODEL)


def _mix_segment(h_ref, r0, cos_ref, sin_ref, dec_ref, kd_ref, qd_ref, wp_ref, bp_ref, ps_ref, gng_ref, gnb_ref,
                 s0_ref, h0_ref, cat_ref, snew_ref, hnew_ref, s_ref, u_ref,
                 *, seg_len, chunk, pos_start, first, g_chunk):
    hist = jnp.concatenate([jnp.zeros((1, POOL_WIDTH), F32), h0_ref[0]], axis=0)
    if first is True:
        s_ref[...] = s0_ref[0]
        u_ref[0:HIST_PAD, :] = hist
    else:
        s_ref[...] = jnp.where(first, s0_ref[0], s_ref[...])
        u_ref[0:HIST_PAD, :] = jnp.where(first, hist, u_ref[0:HIST_PAD, :])

    u_new = h_ref[pl.ds(r0, seg_len), 0:POOL_WIDTH]
    u_ref[HIST_PAD:HIST_PAD + seg_len, :] = u_new
    pos = (pos_start + lax.broadcasted_iota(jnp.int32, (seg_len, 1), 0)).astype(F32)
    for gi, w in enumerate(POOL_WINDOWS):
        cols = slice(gi * POOL_CH, (gi + 1) * POOL_CH)
        win = u_ref[HIST_PAD:HIST_PAD + seg_len, cols]
        for back in range(1, w):
            win = win + u_ref[HIST_PAD - back:HIST_PAD - back + seg_len, cols]
        cnt = jnp.minimum(float(w), pos + 1.0)
        d = win * (1.0 / cnt) - u_ref[HIST_PAD:HIST_PAD + seg_len, cols]
        pooled = jnp.dot(d.astype(BF16), wp_ref[gi], preferred_element_type=F32) + bp_ref[:, cols]
        cat_ref[:, cols] = (pooled * ps_ref[:, cols]).astype(BF16)

    hnew_ref[0] = u_ref[seg_len + 1:seg_len + HIST_PAD, :]
    u_ref[0:HIST_PAD, :] = u_ref[seg_len:seg_len + HIST_PAD, :]

    scale = RET_HEAD_DIM ** -0.5
    for c in range(seg_len // chunk):
        rows = pl.ds(r0 + c * chunk, chunk)
        trows = slice(c * chunk, (c + 1) * chunk)
        cos_t = cos_ref[trows, :]
        sin_t = sin_ref[trows, :]
        for hd in range(RET_HEADS):
            lo = hd * RET_HEAD_DIM
            hcols = slice(lo, lo + RET_HEAD_DIM)
            q = h_ref[rows, POOL_WIDTH + lo:POOL_WIDTH + lo + RET_HEAD_DIM]
            k = h_ref[rows, POOL_WIDTH + RET_WIDTH + lo:POOL_WIDTH + RET_WIDTH + lo + RET_HEAD_DIM]
            v = h_ref[rows, POOL_WIDTH + 2 * RET_WIDTH + lo:POOL_WIDTH + 2 * RET_WIDTH + lo + RET_HEAD_DIM]
            gate = h_ref[rows, POOL_WIDTH + 3 * RET_WIDTH + lo:POOL_WIDTH + 3 * RET_WIDTH + lo + RET_HEAD_DIM]
            qr = q * cos_t + pltpu.roll(q, RET_HEAD_DIM // 2, 1) * sin_t
            kr = (k * cos_t + pltpu.roll(k, RET_HEAD_DIM // 2, 1) * sin_t) * scale
            vb = v.astype(BF16)
            scores = lax.dot_general(qr.astype(BF16), kr.astype(BF16), (((1,), (1,)), ((), ())),
                                     preferred_element_type=F32) * dec_ref[hd]
            o = jnp.dot(scores.astype(BF16), vb, preferred_element_type=F32)
            s_prev = s_ref[hd]
            o = o + jnp.dot((qr * qd_ref[hd]).astype(BF16), s_prev.astype(BF16), preferred_element_type=F32)
            upd = lax.dot_general((kr * kd_ref[hd]).astype(BF16), vb, (((0,), (0,)), ((), ())),
                                  preferred_element_type=F32)
            s_ref[hd] = g_chunk[hd] * s_prev + upd
            mu = jnp.mean(o, axis=-1, keepdims=True)
            oc = o - mu
            var = jnp.mean(oc * oc, axis=-1, keepdims=True)
            on = oc * lax.rsqrt(var + GN_EPS) * gng_ref[:, hcols] + gnb_ref[:, hcols]
            cat_ref[trows, POOL_WIDTH + lo:POOL_WIDTH + lo + RET_HEAD_DIM] = (on * _silu(gate)).astype(BF16)

    snew_ref[0] = s_ref[...]


def _mixer_segments_kernel(x_ref, w_in_ref, *refs, seg_len, chunk, pos0, g_chunk):
    (*mix_refs, h_ref, s_ref, u_ref) = refs
    j = pl.program_id(0)

    @pl.when(j == 0)
    def _project():
        h_ref[...] = jnp.dot(x_ref[...].astype(BF16), w_in_ref[...], preferred_element_type=F32)

    _mix_segment(h_ref, pl.multiple_of(j * seg_len, seg_len), *mix_refs, s_ref, u_ref,
                 seg_len=seg_len, chunk=chunk, pos_start=pos0, first=True, g_chunk=g_chunk)


def _mixer_pipelined_kernel(x_ref, w_in_hbm, *refs, tiles_per_stream, seg_len, chunk, pos0, g_chunk):
    (*mix_refs, w_bf16_hbm, ha_ref, hb_ref, s_ref, u_ref, wb_ref, wsem, osem) = refs
    g = pl.program_id(0)
    m = jnp.maximum(g - 1, 0)
    first = ((m % tiles_per_stream) == 0) | (g == 0)
    pos_start = pos0 + (m % tiles_per_stream) * seg_len
    w_out_copy = pltpu.make_async_copy(wb_ref, w_bf16_hbm, osem)

    @pl.when(g == 0)
    def _stage_weight():
        stage = (ha_ref, hb_ref)
        chunk_rows = ha_ref.shape[0]
        n_chunks = wb_ref.shape[0] // chunk_rows
        copies = [pltpu.make_async_copy(w_in_hbm.at[0, pl.ds(c * chunk_rows, chunk_rows), :], stage[c % 2],
                                        wsem.at[c % 2]) for c in range(n_chunks)]
        for c in range(min(2, n_chunks)):
            copies[c].start()
        for c in range(n_chunks):
            copies[c].wait()
            wb_ref[pl.ds(c * chunk_rows, chunk_rows), :] = stage[c % 2][...].astype(BF16)
            if c + 2 < n_chunks:
                copies[c + 2].start()
        w_out_copy.start()
        hb_ref[...] = jnp.zeros_like(hb_ref)

    def step(h_write, h_read):
        _mix_segment(h_read, 0, *mix_refs, s_ref, u_ref, seg_len=seg_len, chunk=chunk, pos_start=pos_start,
                     first=first, g_chunk=g_chunk)
        h_write[...] = jnp.dot(x_ref[...].astype(BF16), wb_ref[...], preferred_element_type=F32)

    pl.when(g % 2 == 0)(lambda: step(ha_ref, hb_ref))
    pl.when(g % 2 == 1)(lambda: step(hb_ref, ha_ref))

    @pl.when(g == pl.num_programs(0) - 1)
    def _():
        w_out_copy.wait()


def _retention_tables(chunk, t_len, pos0):
    log_gamma = jnp.log(1.0 - 2.0 ** (-5.0 - jnp.arange(RET_HEADS, dtype=F32)))
    idx = jnp.arange(chunk, dtype=F32)
    diff = idx[:, None] - idx[None, :]
    dec = jnp.where(diff >= 0, jnp.exp(log_gamma[:, None, None] * jnp.maximum(diff, 0.0)), 0.0)
    kd = jnp.exp(log_gamma[:, None] * (chunk - 1.0 - idx)[None, :])
    qd = jnp.exp(log_gamma[:, None] * (idx + 1.0)[None, :])
    kd = jnp.broadcast_to(kd[:, :, None], (RET_HEADS, chunk, RET_HEAD_DIM))
    qd = jnp.broadcast_to(qd[:, :, None], (RET_HEADS, chunk, RET_HEAD_DIM))
    half = RET_HEAD_DIM // 2
    freqs = ROPE_BASE ** (-jnp.arange(half, dtype=F32) / half)
    pos = pos0 + jnp.arange(t_len, dtype=F32)
    ang = pos[:, None] * freqs[None, :]
    cos = jnp.cos(ang)
    sin = jnp.sin(ang)
    cos_t = jnp.concatenate([cos, cos], axis=-1)
    sin_t = jnp.concatenate([-sin, sin], axis=-1)
    g_chunk = tuple(math.exp(math.log(1.0 - 2.0 ** (-5.0 - h)) * chunk) for h in range(RET_HEADS))
    return dec, kd, qd, cos_t, sin_t, g_chunk


def _mixer(x2d, w_in, wp_b, bp, ps, gng, gnb, s0, h0, *, n_streams, t_len, tile_rows, chunk, pos0):
    rows = n_streams * t_len
    pipelined = t_len > tile_rows
    seg_len = tile_rows if pipelined else t_len
    assert (t_len % tile_rows == 0) if pipelined else (rows == tile_rows)
    assert seg_len % chunk == 0 and t_len >= POOL_HIST
    dec, kd, qd, cos_t, sin_t, g_chunk = _retention_tables(chunk, t_len, pos0)
    state_block = (1, RET_HEADS, RET_HEAD_DIM, RET_HEAD_DIM)
    hist_block = (1, POOL_HIST, POOL_WIDTH)

    if pipelined:
        n_tiles = rows // tile_rows
        tiles_per_stream = t_len // tile_rows
        grid = (n_tiles + 1,)
        mixed = lambda g: jnp.maximum(g - 1, 0)
        x_map = lambda g: (jnp.minimum(g, n_tiles - 1), 0)
        time_map = lambda g: (mixed(g) % tiles_per_stream, 0)
        cat_map = lambda g: (mixed(g), 0)
        state_map = lambda g: (mixed(g) // tiles_per_stream, 0, 0, 0)
        hist_map = lambda g: (mixed(g) // tiles_per_stream, 0, 0)
        kern = functools.partial(_mixer_pipelined_kernel, tiles_per_stream=tiles_per_stream, seg_len=seg_len,
                                 chunk=chunk, pos0=pos0, g_chunk=g_chunk)
        h_scratch = [pltpu.VMEM((tile_rows, IN_WIDTH), F32), pltpu.VMEM((tile_rows, IN_WIDTH), F32)]
        w_spec = pl.BlockSpec(memory_space=pl.ANY)
        extra_out_specs = [pl.BlockSpec(memory_space=pl.ANY)]
        extra_out_shape = [jax.ShapeDtypeStruct((D_MODEL, IN_WIDTH), BF16)]
        extra_scratch = [pltpu.VMEM((D_MODEL, IN_WIDTH), BF16), pltpu.SemaphoreType.DMA((2,)),
                         pltpu.SemaphoreType.DMA(())]
        assert D_MODEL % tile_rows == 0
    else:
        grid = (n_streams,)
        x_map = lambda j: (0, 0)
        time_map = lambda j: (0, 0)
        cat_map = lambda j: (j, 0)
        state_map = lambda j: (j, 0, 0, 0)
        hist_map = lambda j: (j, 0, 0)
        kern = functools.partial(_mixer_segments_kernel, seg_len=seg_len, chunk=chunk, pos0=pos0, g_chunk=g_chunk)
        h_scratch = [pltpu.VMEM((tile_rows, IN_WIDTH), F32)]
        w_spec = _const_spec((D_MODEL, IN_WIDTH))
        extra_out_specs, extra_out_shape, extra_scratch = [], [], []

    return pl.pallas_call(
        kern,
        grid=grid,
        in_specs=[
            pl.BlockSpec((tile_rows, D_MODEL), x_map),
            w_spec,
            pl.BlockSpec((seg_len, RET_HEAD_DIM), time_map),
            pl.BlockSpec((seg_len, RET_HEAD_DIM), time_map),
            _const_spec((RET_HEADS, chunk, chunk)),
            _const_spec((RET_HEADS, chunk, RET_HEAD_DIM)),
            _const_spec((RET_HEADS, chunk, RET_HEAD_DIM)),
            _const_spec((len(POOL_WINDOWS), POOL_CH, POOL_CH)),
            _const_spec((1, POOL_WIDTH)),
            _const_spec((1, POOL_WIDTH)),
            _const_spec((1, RET_WIDTH)),
            _const_spec((1, RET_WIDTH)),
            pl.BlockSpec(state_block, state_map),
            pl.BlockSpec(hist_block, hist_map),
        ],
        out_specs=[
            pl.BlockSpec((seg_len, D_MODEL), cat_map),
            pl.BlockSpec(state_block, state_map),
            pl.BlockSpec(hist_block, hist_map),
        ] + extra_out_specs,
        out_shape=[
            jax.ShapeDtypeStruct((rows, D_MODEL), BF16),
            jax.ShapeDtypeStruct((n_streams, RET_HEADS, RET_HEAD_DIM, RET_HEAD_DIM), F32),
            jax.ShapeDtypeStruct((n_streams, POOL_HIST, POOL_WIDTH), F32),
        ] + extra_out_shape,
        scratch_shapes=h_scratch + [
            pltpu.VMEM((RET_HEADS, RET_HEAD_DIM, RET_HEAD_DIM), F32),
            pltpu.VMEM((HIST_PAD + seg_len, POOL_WIDTH), F32),
        ] + extra_scratch,
        compiler_params=pltpu.CompilerParams(dimension_semantics=("arbitrary",),
                                             vmem_limit_bytes=VMEM_LIMIT),
        name="mixer",
    )(x2d, w_in, cos_t, sin_t, dec, kd, qd, wp_b, bp, ps, gng, gnb, s0, h0)


def _row_sources(arrays, n_tiles_first):
    if len(arrays) == 1:
        return [pl.BlockSpec((ROW_TILE, arrays[0].shape[1]), lambda i: (i, 0))]
    first, second = arrays
    return [
        pl.BlockSpec((ROW_TILE, first.shape[1]), lambda i: (jnp.minimum(i, n_tiles_first - 1), 0)),
        pl.BlockSpec((ROW_TILE, second.shape[1]), lambda i: (jnp.maximum(i - n_tiles_first, 0), 0),
                     pipeline_mode=pl.Buffered(1)),
    ]


def _mm_ln_kernel(*refs, n_a, n_res, n_tiles_first, emit_packed):
    a_refs = refs[:n_a]
    w_ref = refs[n_a]
    res_refs = refs[n_a + 1:n_a + 1 + n_res]
    g_ref, b_ref, o_ref = refs[n_a + 1 + n_res:n_a + 4 + n_res]
    tiles_ref = refs[n_a + 4 + n_res] if emit_packed else None
    wb_ref = refs[-1]
    i = pl.program_id(0)

    @pl.when(i == 0)
    def _():
        wb_ref[...] = w_ref[0].astype(BF16)

    def body(a_ref, res_ref):
        half = ROW_TILE // 2
        for h in range(2):
            rows = slice(h * half, (h + 1) * half)
            acc = jnp.dot(a_ref[rows, :].astype(BF16), wb_ref[...], preferred_element_type=F32)
            out = _layer_norm(ALPHA * res_ref[rows, :] + acc, g_ref[...], b_ref[...])
            o_ref[rows, :] = out
            if emit_packed:
                tile_rows = slice(h * half * TOKEN_TILE_ROWS, (h + 1) * half * TOKEN_TILE_ROWS)
                tiles_ref[tile_rows, :] = _to_token_tiles(out)

    pl.when(i < n_tiles_first)(lambda: body(a_refs[0], res_refs[0]))
    pl.when(i >= n_tiles_first)(lambda: body(a_refs[-1], res_refs[-1]))


def _mm_ln(a_arrays, w, res_arrays, g, b, *, n_rows, n_tiles_first, name, emit_packed=False):
    n_tiles = n_rows // ROW_TILE
    kern = functools.partial(_mm_ln_kernel, n_a=len(a_arrays), n_res=len(res_arrays),
                             n_tiles_first=n_tiles_first, emit_packed=emit_packed)
    out_specs = [pl.BlockSpec((ROW_TILE, D_MODEL), lambda i: (i, 0))]
    out_shape = [jax.ShapeDtypeStruct((n_rows, D_MODEL), F32)]
    if emit_packed:
        out_specs.append(pl.BlockSpec((ROW_TILE * TOKEN_TILE_ROWS, LANES), lambda i: (i, 0)))
        out_shape.append(jax.ShapeDtypeStruct((n_rows * TOKEN_TILE_ROWS, LANES), BF16))
    return pl.pallas_call(
        kern,
        grid=(n_tiles,),
        in_specs=(_row_sources(a_arrays, n_tiles_first) + [_const_spec(w.shape)]
                  + _row_sources(res_arrays, n_tiles_first)
                  + [_const_spec((1, D_MODEL)), _const_spec((1, D_MODEL))]),
        out_specs=out_specs,
        out_shape=out_shape,
        scratch_shapes=[pltpu.VMEM(w.shape[1:], BF16)],
        compiler_params=pltpu.CompilerParams(dimension_semantics=("arbitrary",),
                                             vmem_limit_bytes=VMEM_LIMIT),
        name=name,
    )(*a_arrays, w, *res_arrays, g, b)


def _matmul_kernel(a_ref, w_ref, o_ref, wb_ref):
    @pl.when(pl.program_id(0) == 0)
    def _():
        wb_ref[...] = w_ref[0].astype(BF16)

    o_ref[...] = jnp.dot(a_ref[...].astype(BF16), wb_ref[...], preferred_element_type=F32).astype(o_ref.dtype)


def _matmul(a, w, out_dtype, name):
    n_rows = a.shape[0]
    return pl.pallas_call(
        _matmul_kernel,
        grid=(n_rows // ROW_TILE,),
        in_specs=[pl.BlockSpec((ROW_TILE, a.shape[1]), lambda i: (i, 0)), _const_spec(w.shape)],
        out_specs=pl.BlockSpec((ROW_TILE, w.shape[2]), lambda i: (i, 0)),
        out_shape=jax.ShapeDtypeStruct((n_rows, w.shape[2]), out_dtype),
        scratch_shapes=[pltpu.VMEM(w.shape[1:], BF16)],
        compiler_params=pltpu.CompilerParams(dimension_semantics=("arbitrary",),
                                             vmem_limit_bytes=VMEM_LIMIT),
        name=name,
    )(a, w)


def _attention_kernel(q_ref, k_ref, v_ref, o_ref):
    scale = MEM_HEAD_DIM ** -0.5
    heads_split = len(k_ref.shape) == 5

    def rows_by_model_dim(ref):
        if heads_split:
            return ref[0, 0].astype(BF16).reshape(N_MEM, D_MODEL)
        return ref[0].astype(BF16)

    k = rows_by_model_dim(k_ref)
    v = rows_by_model_dim(v_ref)
    for h in range(MEM_HEADS):
        cols = slice(h * MEM_HEAD_DIM, (h + 1) * MEM_HEAD_DIM)
        s = lax.dot_general(q_ref[:, cols], k[:, cols], (((1,), (1,)), ((), ())),
                            preferred_element_type=F32) * scale
        m = jnp.max(s, axis=-1, keepdims=True)
        p = jnp.exp(s - m)
        p = p * (1.0 / jnp.sum(p, axis=-1, keepdims=True))
        o_ref[:, cols] = jnp.dot(p.astype(BF16), v[:, cols], preferred_element_type=F32).astype(o_ref.dtype)


def _attention(q_all, mem_k, mem_v, *, n_streams, t_len, q_rows, row_offset, name):
    tiles_per_stream = t_len // q_rows
    base = row_offset // q_rows
    if mem_k.ndim == 5:
        kv_spec = pl.BlockSpec((1, 1, N_MEM, MEM_HEADS, MEM_HEAD_DIM), lambda b, t: (0, b, 0, 0, 0))
    else:
        kv_spec = pl.BlockSpec((1, N_MEM, D_MODEL), lambda b, t: (b, 0, 0))
    return pl.pallas_call(
        _attention_kernel,
        grid=(n_streams, tiles_per_stream),
        in_specs=[
            pl.BlockSpec((q_rows, D_MODEL), lambda b, t: (base + b * tiles_per_stream + t, 0)),
            kv_spec,
            kv_spec,
        ],
        out_specs=pl.BlockSpec((q_rows, D_MODEL), lambda b, t: (b * tiles_per_stream + t, 0)),
        out_shape=jax.ShapeDtypeStruct((n_streams * t_len, D_MODEL), BF16),
        compiler_params=pltpu.CompilerParams(dimension_semantics=("arbitrary", "arbitrary"),
                                             vmem_limit_bytes=VMEM_LIMIT),
        name=name,
    )(q_all, mem_k, mem_v)


_EXPERT_LANE0 = N_GROUPS
(_META_LANE1, _META_LANE2, _META_POS1, _META_POS2, _META_GATE1, _META_GATE2) = range(6)
(_OUT_SLOT1, _OUT_SLOT2, _OUT_GATE1, _OUT_GATE2) = range(4)
_MOE_TILE_LOG2 = MOE_TILE.bit_length() - 1
assert 1 << _MOE_TILE_LOG2 == MOE_TILE


_ROUTE_ROWS = 48


def _router_kernel(x_ref, whl_ref, wh_ref, out_ref, counts_ref, carry_ref, meta_ref, start_ref, later_ref):
    phase = pl.program_id(0)
    i = pl.program_id(1)
    n_tok = x_ref.shape[0]
    row = lax.broadcasted_iota(jnp.int32, (_ROUTE_ROWS, n_tok), 0)

    @pl.when((phase == 0) & (i == 0))
    def _():
        carry_ref[...] = jnp.zeros_like(carry_ref)
        later_ref[...] = (lax.broadcasted_iota(jnp.int32, (n_tok, n_tok), 0)
                          < lax.broadcasted_iota(jnp.int32, (n_tok, n_tok), 1)).astype(BF16)

    @pl.when(phase == 0)
    def _route():
        x = x_ref[...]
        xh = x.astype(BF16)
        xl = (x - xh.astype(F32)).astype(BF16)
        both = jnp.dot(xh, whl_ref[...], preferred_element_type=F32)
        logits = both[:, :LANES] + jnp.dot(xl, wh_ref[...], preferred_element_type=F32) + both[:, LANES:]
        lt = logits.T[0:_ROUTE_ROWS, :]

        def first_argmax(vals):
            m = jnp.max(vals, axis=0, keepdims=True)
            idx = jnp.min(jnp.where(vals == m, row, LANES), axis=0, keepdims=True)
            return m, idx

        gl = jnp.where(row < N_GROUPS, lt, _NEG_INF)
        gm, g_idx = first_argmax(gl)
        g_w = 1.0 / jnp.sum(jnp.exp(gl - gm), axis=0, keepdims=True)

        in_group = ((row >= _EXPERT_LANE0) & (row < _EXPERT_LANE0 + N_EXPERTS)
                    & (((row - _EXPERT_LANE0) >> 3) == g_idx))
        el = jnp.where(in_group, lt, _NEG_INF)
        m1, i1 = first_argmax(el)
        z = jnp.sum(jnp.exp(el - m1), axis=0, keepdims=True)
        m2, i2 = first_argmax(jnp.where(row == i1, _NEG_INF, el))
        p1 = 1.0 / z
        p2 = jnp.exp(m2 - m1) / z
        den = p1 + p2
        gate1 = p1 / den * g_w
        gate2 = p2 / den * g_w

        hit1 = row == i1
        hit2 = row == i2
        onehot = (hit1 | hit2).astype(BF16)
        carry = carry_ref[0:_ROUTE_ROWS, :]
        rank = jnp.dot(onehot, later_ref[...], preferred_element_type=F32) + carry
        pos1 = jnp.sum(jnp.where(hit1, rank, 0.0), axis=0, keepdims=True)
        pos2 = jnp.sum(jnp.where(hit2, rank, 0.0), axis=0, keepdims=True)
        carry_ref[0:_ROUTE_ROWS, :] = carry + jnp.sum(onehot.astype(F32), axis=1, keepdims=True)

        zero = jnp.zeros_like(pos1)
        meta_ref[i] = jnp.concatenate([i1.astype(F32), i2.astype(F32), pos1, pos2, gate1, gate2, zero, zero], axis=0)

    @pl.when((phase == 1) & (i == 0))
    def _segment_starts():
        counts_ref[...] = carry_ref[...]
        tiles = ((carry_ref[...].astype(jnp.int32) + (MOE_TILE - 1)) >> _MOE_TILE_LOG2).astype(F32).astype(BF16)
        before = (lax.broadcasted_iota(jnp.int32, (LANES, LANES), 1)
                  < lax.broadcasted_iota(jnp.int32, (LANES, LANES), 0)).astype(BF16)
        tiles_wide = jnp.broadcast_to(tiles, (LANES, LANES))
        start_ref[...] = jnp.dot(before, tiles_wide, preferred_element_type=F32)[:, 0:1] * float(MOE_TILE)

    @pl.when(phase == 1)
    def _slots():
        meta = meta_ref[i]
        starts = start_ref[0:_ROUTE_ROWS, :]

        def start_of(expert_row):
            return jnp.sum(jnp.where(row == expert_row.astype(jnp.int32), starts, 0.0), axis=0, keepdims=True)

        slot1 = start_of(meta[_META_LANE1:_META_LANE1 + 1]) + meta[_META_POS1:_META_POS1 + 1]
        slot2 = start_of(meta[_META_LANE2:_META_LANE2 + 1]) + meta[_META_POS2:_META_POS2 + 1]
        zero = jnp.zeros_like(slot1)
        out_ref[...] = jnp.concatenate([slot1, slot2, meta[_META_GATE1:_META_GATE1 + 1],
                                        meta[_META_GATE2:_META_GATE2 + 1], zero, zero, zero, zero], axis=0)


def _router(x_all, w_router):
    n_rows = x_all.shape[0]
    n_tiles = n_rows // ROW_TILE
    w_hi = w_router.astype(BF16)
    w_lo = (w_router - w_hi.astype(F32)).astype(BF16)
    w_hi_lo = jnp.concatenate([w_hi, w_lo], axis=1)
    return pl.pallas_call(
        _router_kernel,
        grid=(2, n_tiles),
        in_specs=[pl.BlockSpec((ROW_TILE, D_MODEL), lambda p, i: (i * (1 - p) + (n_tiles - 1) * p, 0)),
                  _const_spec((D_MODEL, 2 * LANES)), _const_spec((D_MODEL, LANES))],
        out_specs=[pl.BlockSpec((8, ROW_TILE), lambda p, i: (0, i * p)),
                   pl.BlockSpec((LANES, 1), lambda p, i: (0, 0))],
        out_shape=[jax.ShapeDtypeStruct((8, n_rows), F32), jax.ShapeDtypeStruct((LANES, 1), F32)],
        scratch_shapes=[pltpu.VMEM((LANES, 1), F32), pltpu.VMEM((n_tiles, 8, ROW_TILE), F32),
                        pltpu.VMEM((LANES, 1), F32), pltpu.VMEM((ROW_TILE, ROW_TILE), BF16)],
        compiler_params=pltpu.CompilerParams(dimension_semantics=("arbitrary", "arbitrary"),
                                             vmem_limit_bytes=VMEM_LIMIT),
        name="router",
    )(x_all, w_hi_lo, w_hi)


def _token_rows(first_token, n_tokens):
    return pl.ds(pl.multiple_of(first_token * TOKEN_TILE_ROWS, TOKEN_TILE_ROWS), n_tokens * TOKEN_TILE_ROWS)


_TAIL_BITS = MOE_TILE.bit_length() - 1


def _dispatch_kernel(slots_ref, tail_start_ref, tail_len_ref, x_ref, xs_hbm, zeros_ref, sem, zsem):
    i = pl.program_id(0)
    n_tokens = x_ref.shape[0] // TOKEN_TILE_ROWS
    group = 16

    def issue(j, carry):
        for u in range(group):
            r = j * group + u
            src = x_ref.at[pl.ds(pl.multiple_of(r * TOKEN_TILE_ROWS, TOKEN_TILE_ROWS), TOKEN_TILE_ROWS), :]
            for k in range(2):
                slot = slots_ref[2 * (i * n_tokens + r) + k]
                pltpu.make_async_copy(src, xs_hbm.at[_token_rows(slot, 1), :], sem).start(priority=k)
        return carry
    lax.fori_loop(0, n_tokens // group, issue, 0)

    def drain(j, carry):
        pltpu.make_async_copy(x_ref.at[pl.ds(0, 2 * group * TOKEN_TILE_ROWS), :],
                              xs_hbm.at[_token_rows(0, 2 * group), :], sem).wait()
        return carry
    lax.fori_loop(0, n_tokens // group, drain, 0)

    def tail_copies(fn):
        def per_expert(ex, carry):
            length = tail_len_ref[ex]
            pos = tail_start_ref[ex]
            for bit in reversed(range(_TAIL_BITS)):
                piece = 1 << bit
                take = (length & piece) != 0
                cp = pltpu.make_async_copy(zeros_ref.at[pl.ds(0, piece * TOKEN_TILE_ROWS), :],
                                           xs_hbm.at[_token_rows(pos, piece), :], zsem)
                pl.when(take)(lambda cp=cp: fn(cp))
                pos = pos + jnp.where(take, piece, 0)
            return carry
        lax.fori_loop(0, N_EXPERTS, per_expert, 0)

    def unused_copies(fn):
        half = zeros_ref.shape[0] // TOKEN_TILE_ROWS
        first_unused = tail_start_ref[N_EXPERTS - 1] + tail_len_ref[N_EXPERTS - 1]

        def per_half(j, carry):
            fn(pltpu.make_async_copy(zeros_ref, xs_hbm.at[_token_rows(first_unused + j * half, half), :], zsem))
            return carry
        lax.fori_loop(0, (xs_hbm.shape[0] // TOKEN_TILE_ROWS - first_unused) // half, per_half, 0)

    @pl.when(i == pl.num_programs(0) - 1)
    def _tails():
        zeros_ref[...] = jnp.zeros_like(zeros_ref)
        tail_copies(lambda cp: cp.start())
        unused_copies(lambda cp: cp.start())
        tail_copies(lambda cp: cp.wait())
        unused_copies(lambda cp: cp.wait())


def _dispatch(slots, tail_start, tail_len, xg, n_tiles_max):
    block_rows = xg.shape[0] // DISPATCH_STEPS
    assert xg.shape[0] % DISPATCH_STEPS == 0 and block_rows % (16 * TOKEN_TILE_ROWS) == 0
    return pl.pallas_call(
        _dispatch_kernel,
        grid_spec=pltpu.PrefetchScalarGridSpec(
            num_scalar_prefetch=3,
            grid=(xg.shape[0] // block_rows,),
            in_specs=[pl.BlockSpec((block_rows, LANES), lambda i, sl, ts, tl: (i, 0))],
            out_specs=pl.BlockSpec(memory_space=pl.ANY),
            scratch_shapes=[pltpu.VMEM((MOE_TILE // 2 * TOKEN_TILE_ROWS, LANES), BF16),
                            pltpu.SemaphoreType.DMA(()), pltpu.SemaphoreType.DMA(())],
        ),
        out_shape=jax.ShapeDtypeStruct((n_tiles_max * MOE_TILE * TOKEN_TILE_ROWS, LANES), BF16),
        compiler_params=pltpu.CompilerParams(dimension_semantics=("arbitrary",),
                                             vmem_limit_bytes=VMEM_LIMIT),
        name="dispatch",
    )(slots, tail_start, tail_len, xg)


def _experts_kernel(tile_start_ref, n_active_ref, xs_hbm, w1_ref, w3_ref, w2_ref, y_hbm,
                    xbuf, ybuf, w1b, w3b, w2b, gsem, ysem, *, n_tiles_max):
    e = pl.program_id(0)
    n_active = n_active_ref[0]

    def x_copy(tile, buf):
        return pltpu.make_async_copy(xs_hbm.at[_token_rows(tile * MOE_TILE, MOE_TILE), :], xbuf.at[buf], gsem.at[buf])

    def y_copy(tile, buf):
        return pltpu.make_async_copy(ybuf.at[buf], y_hbm.at[_token_rows(tile * MOE_TILE, MOE_TILE), :], ysem.at[buf])

    @pl.when(e == 0)
    def _():
        x_copy(0, 0).start()

    w1b[...] = w1_ref[0, 0, 0].astype(BF16)
    w3b[...] = w3_ref[0, 0, 0].astype(BF16)
    w2b[...] = w2_ref[0, 0, 0].astype(BF16)
    first_tile = tile_start_ref[e]

    def tile_body(t, carry):
        g = first_tile + t
        buf = g % 2
        x_copy(g, buf).wait()

        @pl.when(g >= 2)
        def _():
            y_copy(g, buf).wait()

        @pl.when(g + 1 < n_active)
        def _():
            x_copy(g + 1, 1 - buf).start()

        x = _from_token_tiles(xbuf[buf])
        a = jnp.dot(x, w1b[...], preferred_element_type=F32)
        b = jnp.dot(x, w3b[...], preferred_element_type=F32)
        hdn = (_silu(a) * b).astype(BF16)
        y = jnp.dot(hdn, w2b[...], preferred_element_type=F32)
        ybuf[buf] = _to_token_tiles(y)
        y_copy(g, buf).start()
        return carry

    lax.fori_loop(0, tile_start_ref[e + 1] - first_tile, tile_body, 0)

    @pl.when(e == N_EXPERTS - 1)
    def _drain():
        @pl.when(n_active >= 2)
        def _():
            y_copy(0, n_active % 2).wait()

        @pl.when(n_active >= 1)
        def _():
            y_copy(0, (n_active + 1) % 2).wait()

        ybuf[0] = jnp.zeros(ybuf.shape[1:], ybuf.dtype)

        def fill(g, carry):
            cp = y_copy(g, 0)
            cp.start()
            cp.wait()
            return carry
        lax.fori_loop(n_active, n_tiles_max, fill, 0)


def _experts(tile_start, n_active, x_sorted, w1, w3, w2, n_tiles_max):
    def expert_block(e, ts, na):
        return (0, e // EXPERTS_PER_GROUP, e % EXPERTS_PER_GROUP, 0, 0)

    tile_words = MOE_TILE * TOKEN_TILE_ROWS
    kern = functools.partial(_experts_kernel, n_tiles_max=n_tiles_max)
    return pl.pallas_call(
        kern,
        grid_spec=pltpu.PrefetchScalarGridSpec(
            num_scalar_prefetch=2,
            grid=(N_EXPERTS,),
            in_specs=[
                pl.BlockSpec(memory_space=pl.ANY),
                pl.BlockSpec((1, 1, 1, D_MODEL, EXPERT_HIDDEN), expert_block),
                pl.BlockSpec((1, 1, 1, D_MODEL, EXPERT_HIDDEN), expert_block),
                pl.BlockSpec((1, 1, 1, EXPERT_HIDDEN, D_MODEL), expert_block),
            ],
            out_specs=pl.BlockSpec(memory_space=pl.ANY),
            scratch_shapes=[
                pltpu.VMEM((2, tile_words, LANES), BF16),
                pltpu.VMEM((2, tile_words, LANES), BF16),
                pltpu.VMEM((D_MODEL, EXPERT_HIDDEN), BF16),
                pltpu.VMEM((D_MODEL, EXPERT_HIDDEN), BF16),
                pltpu.VMEM((EXPERT_HIDDEN, D_MODEL), BF16),
                pltpu.SemaphoreType.DMA((2,)),
                pltpu.SemaphoreType.DMA((2,)),
            ],
        ),
        out_shape=jax.ShapeDtypeStruct((n_tiles_max * tile_words, LANES), BF16),
        compiler_params=pltpu.CompilerParams(dimension_semantics=("arbitrary",),
                                             vmem_limit_bytes=VMEM_LIMIT),
        name="experts",
    )(tile_start, n_active, x_sorted, w1, w3, w2)


def _combine_kernel(slot_ref, y_hbm, gates_ref, x_ref, g_ref, b_ref, op_ref, os_ref, ybuf, sem,
                    *, n_tiles, n_tiles_first):
    i = pl.program_id(0)

    tile_words = COMBINE_TILE * TOKEN_TILE_ROWS

    def gather(tile, buf):
        for r in range(COMBINE_TILE):
            tok = tile * COMBINE_TILE + r
            for k in range(2):
                row0 = pl.multiple_of(slot_ref[2 * tok + k] * TOKEN_TILE_ROWS, TOKEN_TILE_ROWS)
                pltpu.make_async_copy(y_hbm.at[pl.ds(row0, TOKEN_TILE_ROWS), :],
                                      ybuf.at[buf, k, pl.ds(r * TOKEN_TILE_ROWS, TOKEN_TILE_ROWS), :],
                                      sem.at[buf]).start(priority=k)

    buf = i % 2

    def wait_all():
        for k in range(2):
            pltpu.make_async_copy(y_hbm.at[pl.ds(0, tile_words), :], ybuf.at[buf, k], sem.at[buf]).wait()

    def compute():
        y = (gates_ref[:, 0:1] * _from_token_tiles(ybuf[buf, 0]).astype(F32)
             + gates_ref[:, 1:2] * _from_token_tiles(ybuf[buf, 1]).astype(F32))
        out = _layer_norm(ALPHA * x_ref[...] + y, g_ref[...], b_ref[...])

        @pl.when(i < n_tiles_first)
        def _():
            op_ref[...] = out

        @pl.when(i >= n_tiles_first)
        def _():
            os_ref[...] = out

    @pl.when(i == 0)
    def _():
        gather(0, 0)

    @pl.when(i + 1 < n_tiles)
    def _steady():
        wait_all()
        gather(i + 1, 1 - buf)
        compute()

    @pl.when(i + 1 == n_tiles)
    def _last():
        wait_all()
        compute()


def _combine(slots, y_sorted, gates, x_all, g, b, n_rows_first):
    n_rows = x_all.shape[0]
    n_tiles = n_rows // COMBINE_TILE
    n_first = n_rows_first // COMBINE_TILE
    kern = functools.partial(_combine_kernel, n_tiles=n_tiles, n_tiles_first=n_first)
    return pl.pallas_call(
        kern,
        grid_spec=pltpu.PrefetchScalarGridSpec(
            num_scalar_prefetch=1,
            grid=(n_tiles,),
            in_specs=[
                pl.BlockSpec(memory_space=pl.ANY),
                pl.BlockSpec((COMBINE_TILE, 2), lambda i, sl: (i, 0)),
                pl.BlockSpec((COMBINE_TILE, D_MODEL), lambda i, sl: (i, 0)),
                pl.BlockSpec((1, D_MODEL), lambda i, sl: (0, 0)),
                pl.BlockSpec((1, D_MODEL), lambda i, sl: (0, 0)),
            ],
            out_specs=[
                pl.BlockSpec((COMBINE_TILE, D_MODEL), lambda i, sl: (jnp.minimum(i, n_first - 1), 0)),
                pl.BlockSpec((COMBINE_TILE, D_MODEL), lambda i, sl: (jnp.maximum(i - n_first, 0), 0)),
            ],
            scratch_shapes=[pltpu.VMEM((2, 2, COMBINE_TILE * TOKEN_TILE_ROWS, LANES), BF16),
                            pltpu.SemaphoreType.DMA((2,))],
        ),
        out_shape=[jax.ShapeDtypeStruct((n_rows_first, D_MODEL), F32),
                   jax.ShapeDtypeStruct((n_rows - n_rows_first, D_MODEL), F32)],
        compiler_params=pltpu.CompilerParams(dimension_semantics=("arbitrary",),
                                             vmem_limit_bytes=VMEM_LIMIT),
        name="combine",
    )(slots, y_sorted, gates, x_all, g, b)


def _dispatch_plan(routed, counts):
    slots = routed[_OUT_SLOT1:_OUT_SLOT2 + 1].T.astype(jnp.int32).reshape(-1)
    gates = routed[_OUT_GATE1:_OUT_GATE2 + 1].T
    cnt = counts[_EXPERT_LANE0:_EXPERT_LANE0 + N_EXPERTS, 0].astype(jnp.int32)
    tiles = (cnt + MOE_TILE - 1) // MOE_TILE
    tile_end = jnp.cumsum(tiles)
    tile_start = jnp.concatenate([jnp.zeros((1,), jnp.int32), tile_end]).astype(jnp.int32)
    tail_start = (tile_start[:-1] * MOE_TILE + cnt).astype(jnp.int32)
    tail_len = (tiles * MOE_TILE - cnt).astype(jnp.int32)
    return tile_start, tile_end[-1:].astype(jnp.int32), slots, gates, tail_start, tail_len


def kernel(x_prompt, x_sample, mem_prompt, state_pool, state_ret, cache_mem_k, cache_mem_v, w_in, w_pool, b_pool,
           pool_scale, ret_gn_g, ret_gn_b, w_out, ln1_g, ln1_b, w_mq, w_mk, w_mv, w_mo, ln2_g, ln2_b, w_rg, w_re,
           w1, w3, w2, ln3_g, ln3_b):
    assert w_in.shape[0] == DEPTH == 1
    bp_n, tp, _ = x_prompt.shape
    bs_n, ts, _ = x_sample.shape
    rows_p, rows_s = bp_n * tp, bs_n * ts
    assert rows_s == ROW_TILE and rows_p % ROW_TILE == 0
    n_rows = rows_p + rows_s
    tiles_p = rows_p // ROW_TILE

    wp_b = w_pool[0].astype(BF16)
    bp = b_pool[0].reshape(1, POOL_WIDTH)
    ps = pool_scale[0].reshape(1, POOL_WIDTH)
    gng = ret_gn_g[0].reshape(1, RET_WIDTH)
    gnb = ret_gn_b[0].reshape(1, RET_WIDTH)
    row = lambda p: p[0].reshape(1, D_MODEL)

    xp2d = x_prompt.reshape(rows_p, D_MODEL)
    xs2d = x_sample.reshape(rows_s, D_MODEL)

    mem2d = mem_prompt.reshape(bp_n * N_MEM, D_MODEL)
    mk_p = _matmul(mem2d, w_mk, F32, "mem_k").reshape(bp_n, N_MEM, D_MODEL)
    mv_p = _matmul(mem2d, w_mv, F32, "mem_v").reshape(bp_n, N_MEM, D_MODEL)

    zeros_s = jnp.zeros((bp_n, RET_HEADS, RET_HEAD_DIM, RET_HEAD_DIM), F32)
    zeros_h = jnp.zeros((bp_n, POOL_HIST, POOL_WIDTH), F32)
    cat_p, ret_p, pool_p, w_in_b = _mixer(xp2d, w_in, wp_b, bp, ps, gng, gnb, zeros_s, zeros_h,
                                          n_streams=bp_n, t_len=tp, tile_rows=MIXER_TILE, chunk=MIXER_TILE, pos0=0)
    cat_s, ret_s, pool_s = _mixer(xs2d, w_in_b, wp_b, bp, ps, gng, gnb, state_ret[0], state_pool[0],
                                  n_streams=bs_n, t_len=ts, tile_rows=rows_s, chunk=ts, pos0=PAST_LEN)
    (x1,) = _mm_ln([cat_p, cat_s], w_out, [xp2d, xs2d], row(ln1_g), row(ln1_b),
                   n_rows=n_rows, n_tiles_first=tiles_p, name="out_ln1")

    q_all = _matmul(x1, w_mq, BF16, "mem_q")
    o_p = _attention(q_all, mk_p, mv_p, n_streams=bp_n, t_len=tp, q_rows=ROW_TILE, row_offset=0, name="attn_prompt")
    o_s = _attention(q_all, cache_mem_k, cache_mem_v,
                     n_streams=bs_n, t_len=ts, q_rows=ts, row_offset=rows_p, name="attn_sample")
    x2, x2_tiles = _mm_ln([o_p, o_s], w_mo, [x1], row(ln2_g), row(ln2_b),
                          n_rows=n_rows, n_tiles_first=tiles_p, name="mo_ln2", emit_packed=True)

    w_router = jnp.concatenate([w_rg[0], w_re[0].reshape(D_MODEL, N_EXPERTS),
                                jnp.zeros((D_MODEL, LANES - N_GROUPS - N_EXPERTS), F32)], axis=1)
    routed, counts = _router(x2, w_router)
    n_tiles = (2 * n_rows) // MOE_TILE + N_EXPERTS
    tile_start, n_active, slots, gates, tail_start, tail_len = _dispatch_plan(routed, counts)
    x_sorted = _dispatch(slots, tail_start, tail_len, x2_tiles, n_tiles)
    y_sorted = _experts(tile_start, n_active, x_sorted, w1, w3, w2, n_tiles)
    y_p, y_s = _combine(slots, y_sorted, gates, x2, row(ln3_g), row(ln3_b), rows_p)

    kv_shape = (DEPTH, bp_n, N_MEM, MEM_HEADS, MEM_HEAD_DIM)
    return (y_p.reshape(bp_n, tp, D_MODEL), y_s.reshape(bs_n, ts, D_MODEL), pool_p[None], ret_p[None],
            mk_p.reshape(kv_shape), mv_p.reshape(kv_shape), pool_s[None], ret_s[None])
```

```python
import functools
import math

import jax
import jax.numpy as jnp
import numpy as np
from jax import lax
from jax.experimental import pallas as pl
from jax.experimental.pallas import tpu as pltpu

F32 = jnp.float32
BF16 = jnp.bfloat16

D_MODEL = 2048
POOL_WIDTH = 1024
POOL_WINDOWS = (2, 4, 8, 16)
POOL_CH = 256
POOL_HIST = 15
RET_WIDTH = 1024
RET_HEADS = 8
RET_HEAD_DIM = 128
IN_WIDTH = POOL_WIDTH + 4 * RET_WIDTH
ROPE_BASE = 10000.0
N_MEM = 256
MEM_HEADS = 4
MEM_HEAD_DIM = 512
N_GROUPS = 4
EXPERTS_PER_GROUP = 8
N_EXPERTS = N_GROUPS * EXPERTS_PER_GROUP
EXPERT_HIDDEN = 512
LN_EPS = 1e-5
GN_EPS = 1e-6
DEPTH = 1
ALPHA = (2.0 * DEPTH) ** 0.25
PAST_LEN = 2048

LANES = 128
HIST_PAD = 16
ROW_TILE = 512
MIXER_TILE = 256
MOE_TILE = 256
COMBINE_TILE = 256
DISPATCH_STEPS = 4
VMEM_LIMIT = 58 * 1024 * 1024

_NEG_INF = float("-inf")


def _const_spec(shape):
    zeros = (0,) * len(shape)
    return pl.BlockSpec(shape, lambda *_: zeros, pipeline_mode=pl.Buffered(1))


def _when(cond, fn):
    if cond is True:
        fn()
    else:
        pl.when(cond)(fn)


def _layer_norm(z, g, b):
    mu = jnp.mean(z, axis=-1, keepdims=True)
    zc = z - mu
    var = jnp.mean(zc * zc, axis=-1, keepdims=True)
    return zc * lax.rsqrt(var + LN_EPS) * g + b


def _silu(a):
    return a * (1.0 / (1.0 + jnp.exp(-a)))


TOKEN_TILE_ROWS = D_MODEL // LANES


def _to_token_tiles(x):
    rows = x.shape[0]
    return x.astype(BF16).reshape(rows, TOKEN_TILE_ROWS, LANES).reshape(rows * TOKEN_TILE_ROWS, LANES)


def _from_token_tiles(tiles):
    rows = tiles.shape[0] // TOKEN_TILE_ROWS
    return tiles.reshape(rows, TOKEN_TILE_ROWS, LANES).reshape(rows, D_MODEL)


def _mix_segment(h_ref, r0, cos_ref, sin_ref, dec_ref, kd_ref, qd_ref, wp_ref, bp_ref, ps_ref, gng_ref, gnb_ref,
                 s0_ref, h0_ref, cat_ref, snew_ref, hnew_ref, s_ref, u_ref,
                 *, seg_len, chunk, pos_start, first, g_chunk):
    hist = jnp.concatenate([jnp.zeros((1, POOL_WIDTH), F32), h0_ref[0]], axis=0)
    if first is True:
        s_ref[...] = s0_ref[0]
        u_ref[0:HIST_PAD, :] = hist
    else:
        s_ref[...] = jnp.where(first, s0_ref[0], s_ref[...])
        u_ref[0:HIST_PAD, :] = jnp.where(first, hist, u_ref[0:HIST_PAD, :])

    u_new = h_ref[pl.ds(r0, seg_len), 0:POOL_WIDTH]
    u_ref[HIST_PAD:HIST_PAD + seg_len, :] = u_new
    pos = (pos_start + lax.broadcasted_iota(jnp.int32, (seg_len, 1), 0)).astype(F32)
    for gi, w in enumerate(POOL_WINDOWS):
        cols = slice(gi * POOL_CH, (gi + 1) * POOL_CH)
        win = u_ref[HIST_PAD:HIST_PAD + seg_len, cols]
        for back in range(1, w):
            win = win + u_ref[HIST_PAD - back:HIST_PAD - back + seg_len, cols]
        cnt = jnp.minimum(float(w), pos + 1.0)
        d = win * (1.0 / cnt) - u_ref[HIST_PAD:HIST_PAD + seg_len, cols]
        pooled = jnp.dot(d.astype(BF16), wp_ref[gi], preferred_element_type=F32) + bp_ref[:, cols]
        cat_ref[:, cols] = (pooled * ps_ref[:, cols]).astype(BF16)

    hnew_ref[0] = u_ref[seg_len + 1:seg_len + HIST_PAD, :]
    u_ref[0:HIST_PAD, :] = u_ref[seg_len:seg_len + HIST_PAD, :]

    scale = RET_HEAD_DIM ** -0.5
    for c in range(seg_len // chunk):
        rows = pl.ds(r0 + c * chunk, chunk)
        trows = slice(c * chunk, (c + 1) * chunk)
        cos_t = cos_ref[trows, :]
        sin_t = sin_ref[trows, :]
        for hd in range(RET_HEADS):
            lo = hd * RET_HEAD_DIM
            hcols = slice(lo, lo + RET_HEAD_DIM)
            q = h_ref[rows, POOL_WIDTH + lo:POOL_WIDTH + lo + RET_HEAD_DIM]
            k = h_ref[rows, POOL_WIDTH + RET_WIDTH + lo:POOL_WIDTH + RET_WIDTH + lo + RET_HEAD_DIM]
            v = h_ref[rows, POOL_WIDTH + 2 * RET_WIDTH + lo:POOL_WIDTH + 2 * RET_WIDTH + lo + RET_HEAD_DIM]
            gate = h_ref[rows, POOL_WIDTH + 3 * RET_WIDTH + lo:POOL_WIDTH + 3 * RET_WIDTH + lo + RET_HEAD_DIM]
            qr = q * cos_t + pltpu.roll(q, RET_HEAD_DIM // 2, 1) * sin_t
            kr = (k * cos_t + pltpu.roll(k, RET_HEAD_DIM // 2, 1) * sin_t) * scale
            vb = v.astype(BF16)
            scores = lax.dot_general(qr.astype(BF16), kr.astype(BF16), (((1,), (1,)), ((), ())),
                                     preferred_element_type=F32) * dec_ref[hd]
            o = jnp.dot(scores.astype(BF16), vb, preferred_element_type=F32)
            s_prev = s_ref[hd]
            o = o + jnp.dot((qr * qd_ref[hd]).astype(BF16), s_prev.astype(BF16), preferred_element_type=F32)
            upd = lax.dot_general((kr * kd_ref[hd]).astype(BF16), vb, (((0,), (0,)), ((), ())),
                                  preferred_element_type=F32)
            s_ref[hd] = g_chunk[hd] * s_prev + upd
            mu = jnp.mean(o, axis=-1, keepdims=True)
            oc = o - mu
            var = jnp.mean(oc * oc, axis=-1, keepdims=True)
            on = oc * lax.rsqrt(var + GN_EPS) * gng_ref[:, hcols] + gnb_ref[:, hcols]
            cat_ref[trows, POOL_WIDTH + lo:POOL_WIDTH + lo + RET_HEAD_DIM] = (on * _silu(gate)).astype(BF16)

    snew_ref[0] = s_ref[...]


def _mixer_segments_kernel(x_ref, w_in_ref, *refs, seg_len, chunk, pos0, g_chunk):
    (*mix_refs, h_ref, s_ref, u_ref) = refs
    j = pl.program_id(0)

    @pl.when(j == 0)
    def _project():
        h_ref[...] = jnp.dot(x_ref[...].astype(BF16), w_in_ref[...], preferred_element_type=F32)

    _mix_segment(h_ref, pl.multiple_of(j * seg_len, seg_len), *mix_refs, s_ref, u_ref,
                 seg_len=seg_len, chunk=chunk, pos_start=pos0, first=True, g_chunk=g_chunk)


def _mixer_pipelined_kernel(x_ref, w_in_hbm, *refs, tiles_per_stream, seg_len, chunk, pos0, g_chunk):
    (*mix_refs, w_bf16_hbm, ha_ref, hb_ref, s_ref, u_ref, wb_ref, wsem, osem) = refs
    g = pl.program_id(0)
    m = jnp.maximum(g - 1, 0)
    first = ((m % tiles_per_stream) == 0) | (g == 0)
    pos_start = pos0 + (m % tiles_per_stream) * seg_len
    w_out_copy = pltpu.make_async_copy(wb_ref, w_bf16_hbm, osem)

    @pl.when(g == 0)
    def _stage_weight():
        stage = (ha_ref, hb_ref)
        chunk_rows = ha_ref.shape[0]
        n_chunks = wb_ref.shape[0] // chunk_rows
        copies = [pltpu.make_async_copy(w_in_hbm.at[0, pl.ds(c * chunk_rows, chunk_rows), :], stage[c % 2],
                                        wsem.at[c % 2]) for c in range(n_chunks)]
        for c in range(min(2, n_chunks)):
            copies[c].start()
        for c in range(n_chunks):
            copies[c].wait()
            wb_ref[pl.ds(c * chunk_rows, chunk_rows), :] = stage[c % 2][...].astype(BF16)
            if c + 2 < n_chunks:
                copies[c + 2].start()
        w_out_copy.start()
        hb_ref[...] = jnp.zeros_like(hb_ref)

    def step(h_write, h_read):
        _mix_segment(h_read, 0, *mix_refs, s_ref, u_ref, seg_len=seg_len, chunk=chunk, pos_start=pos_start,
                     first=first, g_chunk=g_chunk)
        h_write[...] = jnp.dot(x_ref[...].astype(BF16), wb_ref[...], preferred_element_type=F32)

    pl.when(g % 2 == 0)(lambda: step(ha_ref, hb_ref))
    pl.when(g % 2 == 1)(lambda: step(hb_ref, ha_ref))

    @pl.when(g == pl.num_programs(0) - 1)
    def _():
        w_out_copy.wait()


def _retention_tables(chunk, t_len, pos0):
    log_gamma = np.log(1.0 - 2.0 ** (-5.0 - np.arange(RET_HEADS, dtype=np.float64)))
    idx = np.arange(chunk, dtype=np.float64)
    diff = idx[:, None] - idx[None, :]
    dec = np.where(diff >= 0, np.exp(log_gamma[:, None, None] * np.maximum(diff, 0.0)), 0.0)
    kd = np.exp(log_gamma[:, None] * (chunk - 1.0 - idx)[None, :])
    qd = np.exp(log_gamma[:, None] * (idx + 1.0)[None, :])
    kd = np.broadcast_to(kd[:, :, None], (RET_HEADS, chunk, RET_HEAD_DIM))
    qd = np.broadcast_to(qd[:, :, None], (RET_HEADS, chunk, RET_HEAD_DIM))
    half = RET_HEAD_DIM // 2
    freqs = ROPE_BASE ** (-np.arange(half, dtype=np.float64) / half)
    pos = pos0 + np.arange(t_len, dtype=np.float64)
    ang = pos[:, None] * freqs[None, :]
    cos = np.cos(ang)
    sin = np.sin(ang)
    cos_t = np.concatenate([cos, cos], axis=-1)
    sin_t = np.concatenate([-sin, sin], axis=-1)
    g_chunk = tuple(math.exp(math.log(1.0 - 2.0 ** (-5.0 - h)) * chunk) for h in range(RET_HEADS))
    dec, kd, qd, cos_t, sin_t = (jnp.asarray(t, F32) for t in (dec, kd, qd, cos_t, sin_t))
    return dec, kd, qd, cos_t, sin_t, g_chunk


def _mixer(x2d, w_in, wp_b, bp, ps, gng, gnb, s0, h0, *, n_streams, t_len, tile_rows, chunk, pos0):
    rows = n_streams * t_len
    pipelined = t_len > tile_rows
    seg_len = tile_rows if pipelined else t_len
    assert (t_len % tile_rows == 0) if pipelined else (rows == tile_rows)
    assert seg_len % chunk == 0 and t_len >= POOL_HIST
    dec, kd, qd, cos_t, sin_t, g_chunk = _retention_tables(chunk, t_len, pos0)
    state_block = (1, RET_HEADS, RET_HEAD_DIM, RET_HEAD_DIM)
    hist_block = (1, POOL_HIST, POOL_WIDTH)

    if pipelined:
        n_tiles = rows // tile_rows
        tiles_per_stream = t_len // tile_rows
        grid = (n_tiles + 1,)
        mixed = lambda g: jnp.maximum(g - 1, 0)
        x_map = lambda g: (jnp.minimum(g, n_tiles - 1), 0)
        time_map = lambda g: (mixed(g) % tiles_per_stream, 0)
        cat_map = lambda g: (mixed(g), 0)
        state_map = lambda g: (mixed(g) // tiles_per_stream, 0, 0, 0)
        hist_map = lambda g: (mixed(g) // tiles_per_stream, 0, 0)
        kern = functools.partial(_mixer_pipelined_kernel, tiles_per_stream=tiles_per_stream, seg_len=seg_len,
                                 chunk=chunk, pos0=pos0, g_chunk=g_chunk)
        h_scratch = [pltpu.VMEM((tile_rows, IN_WIDTH), F32), pltpu.VMEM((tile_rows, IN_WIDTH), F32)]
        w_spec = pl.BlockSpec(memory_space=pl.ANY)
        extra_out_specs = [pl.BlockSpec(memory_space=pl.ANY)]
        extra_out_shape = [jax.ShapeDtypeStruct((D_MODEL, IN_WIDTH), BF16)]
        extra_scratch = [pltpu.VMEM((D_MODEL, IN_WIDTH), BF16), pltpu.SemaphoreType.DMA((2,)),
                         pltpu.SemaphoreType.DMA(())]
        assert D_MODEL % tile_rows == 0
    else:
        grid = (n_streams,)
        x_map = lambda j: (0, 0)
        time_map = lambda j: (0, 0)
        cat_map = lambda j: (j, 0)
        state_map = lambda j: (j, 0, 0, 0)
        hist_map = lambda j: (j, 0, 0)
        kern = functools.partial(_mixer_segments_kernel, seg_len=seg_len, chunk=chunk, pos0=pos0, g_chunk=g_chunk)
        h_scratch = [pltpu.VMEM((tile_rows, IN_WIDTH), F32)]
        w_spec = _const_spec((D_MODEL, IN_WIDTH))
        extra_out_specs, extra_out_shape, extra_scratch = [], [], []

    return pl.pallas_call(
        kern,
        grid=grid,
        in_specs=[
            pl.BlockSpec((tile_rows, D_MODEL), x_map),
            w_spec,
            pl.BlockSpec((seg_len, RET_HEAD_DIM), time_map),
            pl.BlockSpec((seg_len, RET_HEAD_DIM), time_map),
            _const_spec((RET_HEADS, chunk, chunk)),
            _const_spec((RET_HEADS, chunk, RET_HEAD_DIM)),
            _const_spec((RET_HEADS, chunk, RET_HEAD_DIM)),
            _const_spec((len(POOL_WINDOWS), POOL_CH, POOL_CH)),
            _const_spec((1, POOL_WIDTH)),
            _const_spec((1, POOL_WIDTH)),
            _const_spec((1, RET_WIDTH)),
            _const_spec((1, RET_WIDTH)),
            pl.BlockSpec(state_block, state_map),
            pl.BlockSpec(hist_block, hist_map),
        ],
        out_specs=[
            pl.BlockSpec((seg_len, D_MODEL), cat_map),
            pl.BlockSpec(state_block, state_map),
            pl.BlockSpec(hist_block, hist_map),
        ] + extra_out_specs,
        out_shape=[
            jax.ShapeDtypeStruct((rows, D_MODEL), BF16),
            jax.ShapeDtypeStruct((n_streams, RET_HEADS, RET_HEAD_DIM, RET_HEAD_DIM), F32),
            jax.ShapeDtypeStruct((n_streams, POOL_HIST, POOL_WIDTH), F32),
        ] + extra_out_shape,
        scratch_shapes=h_scratch + [
            pltpu.VMEM((RET_HEADS, RET_HEAD_DIM, RET_HEAD_DIM), F32),
            pltpu.VMEM((HIST_PAD + seg_len, POOL_WIDTH), F32),
        ] + extra_scratch,
        compiler_params=pltpu.CompilerParams(dimension_semantics=("arbitrary",),
                                             vmem_limit_bytes=VMEM_LIMIT),
        name="mixer",
    )(x2d, w_in, cos_t, sin_t, dec, kd, qd, wp_b, bp, ps, gng, gnb, s0, h0)


def _row_sources(arrays, n_tiles_first):
    if len(arrays) == 1:
        return [pl.BlockSpec((ROW_TILE, arrays[0].shape[1]), lambda i: (i, 0))]
    first, second = arrays
    return [
        pl.BlockSpec((ROW_TILE, first.shape[1]), lambda i: (jnp.minimum(i, n_tiles_first - 1), 0)),
        pl.BlockSpec((ROW_TILE, second.shape[1]), lambda i: (jnp.maximum(i - n_tiles_first, 0), 0),
                     pipeline_mode=pl.Buffered(1)),
    ]


def _mm_ln_kernel(*refs, n_a, n_res, n_tiles_first, emit_packed):
    a_refs = refs[:n_a]
    w_ref = refs[n_a]
    res_refs = refs[n_a + 1:n_a + 1 + n_res]
    g_ref, b_ref, o_ref = refs[n_a + 1 + n_res:n_a + 4 + n_res]
    tiles_ref = refs[n_a + 4 + n_res] if emit_packed else None
    wb_ref = refs[-1]
    i = pl.program_id(0)

    @pl.when(i == 0)
    def _():
        wb_ref[...] = w_ref[0].astype(BF16)

    def body(a_ref, res_ref):
        half = ROW_TILE // 2
        for h in range(2):
            rows = slice(h * half, (h + 1) * half)
            acc = jnp.dot(a_ref[rows, :].astype(BF16), wb_ref[...], preferred_element_type=F32)
            out = _layer_norm(ALPHA * res_ref[rows, :] + acc, g_ref[...], b_ref[...])
            o_ref[rows, :] = out
            if emit_packed:
                tile_rows = slice(h * half * TOKEN_TILE_ROWS, (h + 1) * half * TOKEN_TILE_ROWS)
                tiles_ref[tile_rows, :] = _to_token_tiles(out)

    pl.when(i < n_tiles_first)(lambda: body(a_refs[0], res_refs[0]))
    pl.when(i >= n_tiles_first)(lambda: body(a_refs[-1], res_refs[-1]))


def _mm_ln(a_arrays, w, res_arrays, g, b, *, n_rows, n_tiles_first, name, emit_packed=False):
    n_tiles = n_rows // ROW_TILE
    kern = functools.partial(_mm_ln_kernel, n_a=len(a_arrays), n_res=len(res_arrays),
                             n_tiles_first=n_tiles_first, emit_packed=emit_packed)
    out_specs = [pl.BlockSpec((ROW_TILE, D_MODEL), lambda i: (i, 0))]
    out_shape = [jax.ShapeDtypeStruct((n_rows, D_MODEL), F32)]
    if emit_packed:
        out_specs.append(pl.BlockSpec((ROW_TILE * TOKEN_TILE_ROWS, LANES), lambda i: (i, 0)))
        out_shape.append(jax.ShapeDtypeStruct((n_rows * TOKEN_TILE_ROWS, LANES), BF16))
    return pl.pallas_call(
        kern,
        grid=(n_tiles,),
        in_specs=(_row_sources(a_arrays, n_tiles_first) + [_const_spec(w.shape)]
                  + _row_sources(res_arrays, n_tiles_first)
                  + [_const_spec((1, D_MODEL)), _const_spec((1, D_MODEL))]),
        out_specs=out_specs,
        out_shape=out_shape,
        scratch_shapes=[pltpu.VMEM(w.shape[1:], BF16)],
        compiler_params=pltpu.CompilerParams(dimension_semantics=("arbitrary",),
                                             vmem_limit_bytes=VMEM_LIMIT),
        name=name,
    )(*a_arrays, w, *res_arrays, g, b)


def _matmul_kernel(a_ref, w_ref, o_ref, *rest):
    (*split_refs, wb_ref) = rest

    @pl.when(pl.program_id(0) == 0)
    def _():
        wb_ref[...] = w_ref[0].astype(BF16)

    out = jnp.dot(a_ref[...].astype(BF16), wb_ref[...], preferred_element_type=F32).astype(o_ref.dtype)
    o_ref[...] = out
    for ref in split_refs:
        ref[...] = out.reshape(ref.shape)


def _matmul(a, w, out_dtype, name, heads=None):
    n_rows = a.shape[0]
    out_specs = [pl.BlockSpec((ROW_TILE, w.shape[2]), lambda i: (i, 0))]
    out_shape = [jax.ShapeDtypeStruct((n_rows, w.shape[2]), out_dtype)]
    if heads is not None:
        out_specs.append(pl.BlockSpec((ROW_TILE, heads, w.shape[2] // heads), lambda i: (i, 0, 0)))
        out_shape.append(jax.ShapeDtypeStruct((n_rows, heads, w.shape[2] // heads), out_dtype))
    return pl.pallas_call(
        _matmul_kernel,
        grid=(n_rows // ROW_TILE,),
        in_specs=[pl.BlockSpec((ROW_TILE, a.shape[1]), lambda i: (i, 0)), _const_spec(w.shape)],
        out_specs=out_specs,
        out_shape=out_shape,
        scratch_shapes=[pltpu.VMEM(w.shape[1:], BF16)],
        compiler_params=pltpu.CompilerParams(dimension_semantics=("arbitrary",),
                                             vmem_limit_bytes=VMEM_LIMIT),
        name=name,
    )(a, w)


def _attention_kernel(q_ref, k_ref, v_ref, o_ref):
    scale = MEM_HEAD_DIM ** -0.5
    heads_split = len(k_ref.shape) == 5

    def rows_by_model_dim(ref):
        if heads_split:
            return ref[0, 0].astype(BF16).reshape(N_MEM, D_MODEL)
        return ref[0].astype(BF16)

    k = rows_by_model_dim(k_ref)
    v = rows_by_model_dim(v_ref)
    for h in range(MEM_HEADS):
        cols = slice(h * MEM_HEAD_DIM, (h + 1) * MEM_HEAD_DIM)
        s = lax.dot_general(q_ref[:, cols], k[:, cols], (((1,), (1,)), ((), ())),
                            preferred_element_type=F32) * scale
        m = jnp.max(s, axis=-1, keepdims=True)
        p = jnp.exp(s - m)
        p = p * (1.0 / jnp.sum(p, axis=-1, keepdims=True))
        o_ref[:, cols] = jnp.dot(p.astype(BF16), v[:, cols], preferred_element_type=F32).astype(o_ref.dtype)


def _attention(q_all, mem_k, mem_v, *, n_streams, t_len, q_rows, row_offset, name):
    tiles_per_stream = t_len // q_rows
    base = row_offset // q_rows
    if mem_k.ndim == 5:
        kv_spec = pl.BlockSpec((1, 1, N_MEM, MEM_HEADS, MEM_HEAD_DIM), lambda b, t: (0, b, 0, 0, 0))
    else:
        kv_spec = pl.BlockSpec((1, N_MEM, D_MODEL), lambda b, t: (b, 0, 0))
    return pl.pallas_call(
        _attention_kernel,
        grid=(n_streams, tiles_per_stream),
        in_specs=[
            pl.BlockSpec((q_rows, D_MODEL), lambda b, t: (base + b * tiles_per_stream + t, 0)),
            kv_spec,
            kv_spec,
        ],
        out_specs=pl.BlockSpec((q_rows, D_MODEL), lambda b, t: (b * tiles_per_stream + t, 0)),
        out_shape=jax.ShapeDtypeStruct((n_streams * t_len, D_MODEL), BF16),
        compiler_params=pltpu.CompilerParams(dimension_semantics=("arbitrary", "arbitrary"),
                                             vmem_limit_bytes=VMEM_LIMIT),
        name=name,
    )(q_all, mem_k, mem_v)


_EXPERT_LANE0 = N_GROUPS
(_META_LANE1, _META_LANE2, _META_POS1, _META_POS2, _META_GATE1, _META_GATE2) = range(6)
(_OUT_SLOT1, _OUT_SLOT2, _OUT_GATE1, _OUT_GATE2) = range(4)
_MOE_TILE_LOG2 = MOE_TILE.bit_length() - 1
assert 1 << _MOE_TILE_LOG2 == MOE_TILE


_ROUTE_ROWS = 48


def _router_kernel(x_ref, whl_ref, wh_ref, out_ref, counts_ref, carry_ref, meta_ref, start_ref, later_ref):
    phase = pl.program_id(0)
    i = pl.program_id(1)
    n_tok = x_ref.shape[0]
    row = lax.broadcasted_iota(jnp.int32, (_ROUTE_ROWS, n_tok), 0)

    @pl.when((phase == 0) & (i == 0))
    def _():
        carry_ref[...] = jnp.zeros_like(carry_ref)
        later_ref[...] = (lax.broadcasted_iota(jnp.int32, (n_tok, n_tok), 0)
                          < lax.broadcasted_iota(jnp.int32, (n_tok, n_tok), 1)).astype(BF16)

    @pl.when(phase == 0)
    def _route():
        x = x_ref[...]
        xh = x.astype(BF16)
        xl = (x - xh.astype(F32)).astype(BF16)
        both = jnp.dot(xh, whl_ref[...], preferred_element_type=F32)
        logits = both[:, :LANES] + jnp.dot(xl, wh_ref[...], preferred_element_type=F32) + both[:, LANES:]
        lt = logits.T[0:_ROUTE_ROWS, :]

        def first_argmax(vals):
            m = jnp.max(vals, axis=0, keepdims=True)
            idx = jnp.min(jnp.where(vals == m, row, LANES), axis=0, keepdims=True)
            return m, idx

        gl = jnp.where(row < N_GROUPS, lt, _NEG_INF)
        gm, g_idx = first_argmax(gl)
        g_w = 1.0 / jnp.sum(jnp.exp(gl - gm), axis=0, keepdims=True)

        in_group = ((row >= _EXPERT_LANE0) & (row < _EXPERT_LANE0 + N_EXPERTS)
                    & (((row - _EXPERT_LANE0) >> 3) == g_idx))
        el = jnp.where(in_group, lt, _NEG_INF)
        m1, i1 = first_argmax(el)
        z = jnp.sum(jnp.exp(el - m1), axis=0, keepdims=True)
        m2, i2 = first_argmax(jnp.where(row == i1, _NEG_INF, el))
        p1 = 1.0 / z
        p2 = jnp.exp(m2 - m1) / z
        den = p1 + p2
        gate1 = p1 / den * g_w
        gate2 = p2 / den * g_w

        hit1 = row == i1
        hit2 = row == i2
        onehot = (hit1 | hit2).astype(BF16)
        carry = carry_ref[0:_ROUTE_ROWS, :]
        rank = jnp.dot(onehot, later_ref[...], preferred_element_type=F32) + carry
        pos1 = jnp.sum(jnp.where(hit1, rank, 0.0), axis=0, keepdims=True)
        pos2 = jnp.sum(jnp.where(hit2, rank, 0.0), axis=0, keepdims=True)
        carry_ref[0:_ROUTE_ROWS, :] = carry + jnp.sum(onehot.astype(F32), axis=1, keepdims=True)

        zero = jnp.zeros_like(pos1)
        meta_ref[i] = jnp.concatenate([i1.astype(F32), i2.astype(F32), pos1, pos2, gate1, gate2, zero, zero], axis=0)

    @pl.when((phase == 1) & (i == 0))
    def _segment_starts():
        counts_ref[...] = carry_ref[...]
        tiles = ((carry_ref[...].astype(jnp.int32) + (MOE_TILE - 1)) >> _MOE_TILE_LOG2).astype(F32).astype(BF16)
        before = (lax.broadcasted_iota(jnp.int32, (LANES, LANES), 1)
                  < lax.broadcasted_iota(jnp.int32, (LANES, LANES), 0)).astype(BF16)
        tiles_wide = jnp.broadcast_to(tiles, (LANES, LANES))
        start_ref[...] = jnp.dot(before, tiles_wide, preferred_element_type=F32)[:, 0:1] * float(MOE_TILE)

    @pl.when(phase == 1)
    def _slots():
        meta = meta_ref[i]
        starts = start_ref[0:_ROUTE_ROWS, :]

        def start_of(expert_row):
            return jnp.sum(jnp.where(row == expert_row.astype(jnp.int32), starts, 0.0), axis=0, keepdims=True)

        slot1 = start_of(meta[_META_LANE1:_META_LANE1 + 1]) + meta[_META_POS1:_META_POS1 + 1]
        slot2 = start_of(meta[_META_LANE2:_META_LANE2 + 1]) + meta[_META_POS2:_META_POS2 + 1]
        zero = jnp.zeros_like(slot1)
        out_ref[...] = jnp.concatenate([slot1, slot2, meta[_META_GATE1:_META_GATE1 + 1],
                                        meta[_META_GATE2:_META_GATE2 + 1], zero, zero, zero, zero], axis=0)


def _router(x_all, w_router):
    n_rows = x_all.shape[0]
    n_tiles = n_rows // ROW_TILE
    w_hi = w_router.astype(BF16)
    w_lo = (w_router - w_hi.astype(F32)).astype(BF16)
    w_hi_lo = jnp.concatenate([w_hi, w_lo], axis=1)
    return pl.pallas_call(
        _router_kernel,
        grid=(2, n_tiles),
        in_specs=[pl.BlockSpec((ROW_TILE, D_MODEL), lambda p, i: (i * (1 - p) + (n_tiles - 1) * p, 0)),
                  _const_spec((D_MODEL, 2 * LANES)), _const_spec((D_MODEL, LANES))],
        out_specs=[pl.BlockSpec((8, ROW_TILE), lambda p, i: (0, i * p)),
                   pl.BlockSpec((LANES, 1), lambda p, i: (0, 0))],
        out_shape=[jax.ShapeDtypeStruct((8, n_rows), F32), jax.ShapeDtypeStruct((LANES, 1), F32)],
        scratch_shapes=[pltpu.VMEM((LANES, 1), F32), pltpu.VMEM((n_tiles, 8, ROW_TILE), F32),
                        pltpu.VMEM((LANES, 1), F32), pltpu.VMEM((ROW_TILE, ROW_TILE), BF16)],
        compiler_params=pltpu.CompilerParams(dimension_semantics=("arbitrary", "arbitrary"),
                                             vmem_limit_bytes=VMEM_LIMIT),
        name="router",
    )(x_all, w_hi_lo, w_hi)


def _token_rows(first_token, n_tokens):
    return pl.ds(pl.multiple_of(first_token * TOKEN_TILE_ROWS, TOKEN_TILE_ROWS), n_tokens * TOKEN_TILE_ROWS)


_TAIL_BITS = MOE_TILE.bit_length() - 1


def _dispatch_kernel(slots_ref, tail_start_ref, tail_len_ref, x_ref, xs_hbm, zeros_ref, sem, zsem):
    i = pl.program_id(0)
    n_tokens = x_ref.shape[0] // TOKEN_TILE_ROWS
    group = 16

    def issue(j, carry):
        for u in range(group):
            r = j * group + u
            src = x_ref.at[pl.ds(pl.multiple_of(r * TOKEN_TILE_ROWS, TOKEN_TILE_ROWS), TOKEN_TILE_ROWS), :]
            for k in range(2):
                slot = slots_ref[2 * (i * n_tokens + r) + k]
                pltpu.make_async_copy(src, xs_hbm.at[_token_rows(slot, 1), :], sem).start(priority=k)
        return carry
    lax.fori_loop(0, n_tokens // group, issue, 0)

    def drain(j, carry):
        pltpu.make_async_copy(x_ref.at[pl.ds(0, 2 * group * TOKEN_TILE_ROWS), :],
                              xs_hbm.at[_token_rows(0, 2 * group), :], sem).wait()
        return carry
    lax.fori_loop(0, n_tokens // group, drain, 0)

    def tail_copies(fn):
        def per_expert(ex, carry):
            length = tail_len_ref[ex]
            pos = tail_start_ref[ex]
            for bit in reversed(range(_TAIL_BITS)):
                piece = 1 << bit
                take = (length & piece) != 0
                cp = pltpu.make_async_copy(zeros_ref.at[pl.ds(0, piece * TOKEN_TILE_ROWS), :],
                                           xs_hbm.at[_token_rows(pos, piece), :], zsem)
                pl.when(take)(lambda cp=cp: fn(cp))
                pos = pos + jnp.where(take, piece, 0)
            return carry
        lax.fori_loop(0, N_EXPERTS, per_expert, 0)

    def unused_copies(fn):
        half = zeros_ref.shape[0] // TOKEN_TILE_ROWS
        first_unused = tail_start_ref[N_EXPERTS - 1] + tail_len_ref[N_EXPERTS - 1]

        def per_half(j, carry):
            fn(pltpu.make_async_copy(zeros_ref, xs_hbm.at[_token_rows(first_unused + j * half, half), :], zsem))
            return carry
        lax.fori_loop(0, (xs_hbm.shape[0] // TOKEN_TILE_ROWS - first_unused) // half, per_half, 0)

    @pl.when(i == pl.num_programs(0) - 1)
    def _tails():
        zeros_ref[...] = jnp.zeros_like(zeros_ref)
        tail_copies(lambda cp: cp.start())
        unused_copies(lambda cp: cp.start())
        tail_copies(lambda cp: cp.wait())
        unused_copies(lambda cp: cp.wait())


def _dispatch(slots, tail_start, tail_len, xg, n_tiles_max):
    block_rows = xg.shape[0] // DISPATCH_STEPS
    assert xg.shape[0] % DISPATCH_STEPS == 0 and block_rows % (16 * TOKEN_TILE_ROWS) == 0
    return pl.pallas_call(
        _dispatch_kernel,
        grid_spec=pltpu.PrefetchScalarGridSpec(
            num_scalar_prefetch=3,
            grid=(xg.shape[0] // block_rows,),
            in_specs=[pl.BlockSpec((block_rows, LANES), lambda i, sl, ts, tl: (i, 0))],
            out_specs=pl.BlockSpec(memory_space=pl.ANY),
            scratch_shapes=[pltpu.VMEM((MOE_TILE // 2 * TOKEN_TILE_ROWS, LANES), BF16),
                            pltpu.SemaphoreType.DMA(()), pltpu.SemaphoreType.DMA(())],
        ),
        out_shape=jax.ShapeDtypeStruct((n_tiles_max * MOE_TILE * TOKEN_TILE_ROWS, LANES), BF16),
        compiler_params=pltpu.CompilerParams(dimension_semantics=("arbitrary",),
                                             vmem_limit_bytes=VMEM_LIMIT),
        name="dispatch",
    )(slots, tail_start, tail_len, xg)


def _experts_kernel(tile_start_ref, n_active_ref, xs_hbm, w1_ref, w3_ref, w2_ref, y_hbm,
                    xbuf, ybuf, w1b, w3b, w2b, gsem, ysem, *, n_tiles_max):
    e = pl.program_id(0)
    n_active = n_active_ref[0]

    def x_copy(tile, buf):
        return pltpu.make_async_copy(xs_hbm.at[_token_rows(tile * MOE_TILE, MOE_TILE), :], xbuf.at[buf], gsem.at[buf])

    def y_copy(tile, buf):
        return pltpu.make_async_copy(ybuf.at[buf], y_hbm.at[_token_rows(tile * MOE_TILE, MOE_TILE), :], ysem.at[buf])

    @pl.when(e == 0)
    def _():
        x_copy(0, 0).start()

    w1b[...] = w1_ref[0, 0, 0].astype(BF16)
    w3b[...] = w3_ref[0, 0, 0].astype(BF16)
    w2b[...] = w2_ref[0, 0, 0].astype(BF16)
    first_tile = tile_start_ref[e]

    def tile_body(t, carry):
        g = first_tile + t
        buf = g % 2
        x_copy(g, buf).wait()

        @pl.when(g >= 2)
        def _():
            y_copy(g, buf).wait()

        @pl.when(g + 1 < n_active)
        def _():
            x_copy(g + 1, 1 - buf).start()

        x = _from_token_tiles(xbuf[buf])
        a = jnp.dot(x, w1b[...], preferred_element_type=F32)
        b = jnp.dot(x, w3b[...], preferred_element_type=F32)
        hdn = (_silu(a) * b).astype(BF16)
        y = jnp.dot(hdn, w2b[...], preferred_element_type=F32)
        ybuf[buf] = _to_token_tiles(y)
        y_copy(g, buf).start()
        return carry

    lax.fori_loop(0, tile_start_ref[e + 1] - first_tile, tile_body, 0)

    @pl.when(e == N_EXPERTS - 1)
    def _drain():
        @pl.when(n_active >= 2)
        def _():
            y_copy(0, n_active % 2).wait()

        @pl.when(n_active >= 1)
        def _():
            y_copy(0, (n_active + 1) % 2).wait()

        ybuf[0] = jnp.zeros(ybuf.shape[1:], ybuf.dtype)

        def fill(g, carry):
            cp = y_copy(g, 0)
            cp.start()
            cp.wait()
            return carry
        lax.fori_loop(n_active, n_tiles_max, fill, 0)


def _experts(tile_start, n_active, x_sorted, w1, w3, w2, n_tiles_max):
    def expert_block(e, ts, na):
        return (0, e // EXPERTS_PER_GROUP, e % EXPERTS_PER_GROUP, 0, 0)

    tile_words = MOE_TILE * TOKEN_TILE_ROWS
    kern = functools.partial(_experts_kernel, n_tiles_max=n_tiles_max)
    return pl.pallas_call(
        kern,
        grid_spec=pltpu.PrefetchScalarGridSpec(
            num_scalar_prefetch=2,
            grid=(N_EXPERTS,),
            in_specs=[
                pl.BlockSpec(memory_space=pl.ANY),
                pl.BlockSpec((1, 1, 1, D_MODEL, EXPERT_HIDDEN), expert_block),
                pl.BlockSpec((1, 1, 1, D_MODEL, EXPERT_HIDDEN), expert_block),
                pl.BlockSpec((1, 1, 1, EXPERT_HIDDEN, D_MODEL), expert_block),
            ],
            out_specs=pl.BlockSpec(memory_space=pl.ANY),
            scratch_shapes=[
                pltpu.VMEM((2, tile_words, LANES), BF16),
                pltpu.VMEM((2, tile_words, LANES), BF16),
                pltpu.VMEM((D_MODEL, EXPERT_HIDDEN), BF16),
                pltpu.VMEM((D_MODEL, EXPERT_HIDDEN), BF16),
                pltpu.VMEM((EXPERT_HIDDEN, D_MODEL), BF16),
                pltpu.SemaphoreType.DMA((2,)),
                pltpu.SemaphoreType.DMA((2,)),
            ],
        ),
        out_shape=jax.ShapeDtypeStruct((n_tiles_max * tile_words, LANES), BF16),
        compiler_params=pltpu.CompilerParams(dimension_semantics=("arbitrary",),
                                             vmem_limit_bytes=VMEM_LIMIT),
        name="experts",
    )(tile_start, n_active, x_sorted, w1, w3, w2)


def _combine_kernel(slot_ref, y_hbm, gates_ref, x_ref, g_ref, b_ref, op_ref, os_ref, ybuf, sem,
                    *, n_tiles, n_tiles_first):
    i = pl.program_id(0)

    tile_words = COMBINE_TILE * TOKEN_TILE_ROWS

    def gather(tile, buf):
        for r in range(COMBINE_TILE):
            tok = tile * COMBINE_TILE + r
            for k in range(2):
                row0 = pl.multiple_of(slot_ref[2 * tok + k] * TOKEN_TILE_ROWS, TOKEN_TILE_ROWS)
                pltpu.make_async_copy(y_hbm.at[pl.ds(row0, TOKEN_TILE_ROWS), :],
                                      ybuf.at[buf, k, pl.ds(r * TOKEN_TILE_ROWS, TOKEN_TILE_ROWS), :],
                                      sem.at[buf]).start(priority=k)

    buf = i % 2

    def wait_all():
        for k in range(2):
            pltpu.make_async_copy(y_hbm.at[pl.ds(0, tile_words), :], ybuf.at[buf, k], sem.at[buf]).wait()

    def compute():
        y = (gates_ref[:, 0:1] * _from_token_tiles(ybuf[buf, 0]).astype(F32)
             + gates_ref[:, 1:2] * _from_token_tiles(ybuf[buf, 1]).astype(F32))
        out = _layer_norm(ALPHA * x_ref[...] + y, g_ref[...], b_ref[...])

        @pl.when(i < n_tiles_first)
        def _():
            op_ref[...] = out

        @pl.when(i >= n_tiles_first)
        def _():
            os_ref[...] = out

    @pl.when(i == 0)
    def _():
        gather(0, 0)

    @pl.when(i + 1 < n_tiles)
    def _steady():
        wait_all()
        gather(i + 1, 1 - buf)
        compute()

    @pl.when(i + 1 == n_tiles)
    def _last():
        wait_all()
        compute()


def _combine(slots, y_sorted, gates, x_all, g, b, n_rows_first):
    n_rows = x_all.shape[0]
    n_tiles = n_rows // COMBINE_TILE
    n_first = n_rows_first // COMBINE_TILE
    kern = functools.partial(_combine_kernel, n_tiles=n_tiles, n_tiles_first=n_first)
    return pl.pallas_call(
        kern,
        grid_spec=pltpu.PrefetchScalarGridSpec(
            num_scalar_prefetch=1,
            grid=(n_tiles,),
            in_specs=[
                pl.BlockSpec(memory_space=pl.ANY),
                pl.BlockSpec((COMBINE_TILE, 2), lambda i, sl: (i, 0)),
                pl.BlockSpec((COMBINE_TILE, D_MODEL), lambda i, sl: (i, 0)),
                pl.BlockSpec((1, D_MODEL), lambda i, sl: (0, 0)),
                pl.BlockSpec((1, D_MODEL), lambda i, sl: (0, 0)),
            ],
            out_specs=[
                pl.BlockSpec((COMBINE_TILE, D_MODEL), lambda i, sl: (jnp.minimum(i, n_first - 1), 0)),
                pl.BlockSpec((COMBINE_TILE, D_MODEL), lambda i, sl: (jnp.maximum(i - n_first, 0), 0)),
            ],
            scratch_shapes=[pltpu.VMEM((2, 2, COMBINE_TILE * TOKEN_TILE_ROWS, LANES), BF16),
                            pltpu.SemaphoreType.DMA((2,))],
        ),
        out_shape=[jax.ShapeDtypeStruct((n_rows_first, D_MODEL), F32),
                   jax.ShapeDtypeStruct((n_rows - n_rows_first, D_MODEL), F32)],
        compiler_params=pltpu.CompilerParams(dimension_semantics=("arbitrary",),
                                             vmem_limit_bytes=VMEM_LIMIT),
        name="combine",
    )(slots, y_sorted, gates, x_all, g, b)


def _dispatch_plan(routed, counts):
    slots = routed[_OUT_SLOT1:_OUT_SLOT2 + 1].T.astype(jnp.int32).reshape(-1)
    gates = routed[_OUT_GATE1:_OUT_GATE2 + 1].T
    cnt = counts[_EXPERT_LANE0:_EXPERT_LANE0 + N_EXPERTS, 0].astype(jnp.int32)
    tiles = (cnt + MOE_TILE - 1) // MOE_TILE
    tile_end = jnp.cumsum(tiles)
    tile_start = jnp.concatenate([jnp.zeros((1,), jnp.int32), tile_end]).astype(jnp.int32)
    tail_start = (tile_start[:-1] * MOE_TILE + cnt).astype(jnp.int32)
    tail_len = (tiles * MOE_TILE - cnt).astype(jnp.int32)
    return tile_start, tile_end[-1:].astype(jnp.int32), slots, gates, tail_start, tail_len


def kernel(x_prompt, x_sample, mem_prompt, state_pool, state_ret, cache_mem_k, cache_mem_v, w_in, w_pool, b_pool,
           pool_scale, ret_gn_g, ret_gn_b, w_out, ln1_g, ln1_b, w_mq, w_mk, w_mv, w_mo, ln2_g, ln2_b, w_rg, w_re,
           w1, w3, w2, ln3_g, ln3_b):
    assert w_in.shape[0] == DEPTH == 1
    bp_n, tp, _ = x_prompt.shape
    bs_n, ts, _ = x_sample.shape
    rows_p, rows_s = bp_n * tp, bs_n * ts
    assert rows_s == ROW_TILE and rows_p % ROW_TILE == 0
    n_rows = rows_p + rows_s
    tiles_p = rows_p // ROW_TILE

    wp_b = w_pool[0].astype(BF16)
    bp = b_pool[0].reshape(1, POOL_WIDTH)
    ps = pool_scale[0].reshape(1, POOL_WIDTH)
    gng = ret_gn_g[0].reshape(1, RET_WIDTH)
    gnb = ret_gn_b[0].reshape(1, RET_WIDTH)
    row = lambda p: p[0].reshape(1, D_MODEL)

    xp2d = x_prompt.reshape(rows_p, D_MODEL)
    xs2d = x_sample.reshape(rows_s, D_MODEL)

    mem2d = mem_prompt.reshape(bp_n * N_MEM, D_MODEL)
    mk_p, mk_heads = _matmul(mem2d, w_mk, F32, "mem_k", heads=MEM_HEADS)
    mv_p, mv_heads = _matmul(mem2d, w_mv, F32, "mem_v", heads=MEM_HEADS)
    mk_p = mk_p.reshape(bp_n, N_MEM, D_MODEL)
    mv_p = mv_p.reshape(bp_n, N_MEM, D_MODEL)

    zeros_s = jnp.zeros((bp_n, RET_HEADS, RET_HEAD_DIM, RET_HEAD_DIM), F32)
    zeros_h = jnp.zeros((bp_n, POOL_HIST, POOL_WIDTH), F32)
    cat_p, ret_p, pool_p, w_in_b = _mixer(xp2d, w_in, wp_b, bp, ps, gng, gnb, zeros_s, zeros_h,
                                          n_streams=bp_n, t_len=tp, tile_rows=MIXER_TILE, chunk=MIXER_TILE, pos0=0)
    cat_s, ret_s, pool_s = _mixer(xs2d, w_in_b, wp_b, bp, ps, gng, gnb, state_ret[0], state_pool[0],
                                  n_streams=bs_n, t_len=ts, tile_rows=rows_s, chunk=ts, pos0=PAST_LEN)
    (x1,) = _mm_ln([cat_p, cat_s], w_out, [xp2d, xs2d], row(ln1_g), row(ln1_b),
                   n_rows=n_rows, n_tiles_first=tiles_p, name="out_ln1")

    (q_all,) = _matmul(x1, w_mq, BF16, "mem_q")
    o_p = _attention(q_all, mk_p, mv_p, n_streams=bp_n, t_len=tp, q_rows=ROW_TILE, row_offset=0, name="attn_prompt")
    o_s = _attention(q_all, cache_mem_k, cache_mem_v,
                     n_streams=bs_n, t_len=ts, q_rows=ts, row_offset=rows_p, name="attn_sample")
    x2, x2_tiles = _mm_ln([o_p, o_s], w_mo, [x1], row(ln2_g), row(ln2_b),
                          n_rows=n_rows, n_tiles_first=tiles_p, name="mo_ln2", emit_packed=True)

    w_router = jnp.concatenate([w_rg[0], w_re[0].reshape(D_MODEL, N_EXPERTS),
                                jnp.zeros((D_MODEL, LANES - N_GROUPS - N_EXPERTS), F32)], axis=1)
    routed, counts = _router(x2, w_router)
    n_tiles = (2 * n_rows) // MOE_TILE + N_EXPERTS
    tile_start, n_active, slots, gates, tail_start, tail_len = _dispatch_plan(routed, counts)
    x_sorted = _dispatch(slots, tail_start, tail_len, x2_tiles, n_tiles)
    y_sorted = _experts(tile_start, n_active, x_sorted, w1, w3, w2, n_tiles)
    y_p, y_s = _combine(slots, y_sorted, gates, x2, row(ln3_g), row(ln3_b), rows_p)

    kv_shape = (DEPTH, bp_n, N_MEM, MEM_HEADS, MEM_HEAD_DIM)
    return (y_p.reshape(bp_n, tp, D_MODEL), y_s.reshape(bs_n, ts, D_MODEL), pool_p[None], ret_p[None],
            mk_heads.reshape(kv_shape), mv_heads.reshape(kv_shape), pool_s[None], ret_s[None])
```

```python
import functools
import math

import jax
import jax.numpy as jnp
import numpy as np
from jax import lax
from jax.experimental import pallas as pl
from jax.experimental.pallas import tpu as pltpu

F32 = jnp.float32
BF16 = jnp.bfloat16

D_MODEL = 2048
POOL_WIDTH = 1024
POOL_WINDOWS = (2, 4, 8, 16)
POOL_CH = 256
POOL_HIST = 15
RET_WIDTH = 1024
RET_HEADS = 8
RET_HEAD_DIM = 128
IN_WIDTH = POOL_WIDTH + 4 * RET_WIDTH
ROPE_BASE = 10000.0
N_MEM = 256
MEM_HEADS = 4
MEM_HEAD_DIM = 512
N_GROUPS = 4
EXPERTS_PER_GROUP = 8
N_EXPERTS = N_GROUPS * EXPERTS_PER_GROUP
EXPERT_HIDDEN = 512
LN_EPS = 1e-5
GN_EPS = 1e-6
DEPTH = 1
ALPHA = (2.0 * DEPTH) ** 0.25
PAST_LEN = 2048

LANES = 128
HIST_PAD = 16
ROW_TILE = 512
MIXER_TILE = 256
MOE_TILE = 256
COMBINE_TILE = 256
DISPATCH_STEPS = 4
VMEM_LIMIT = 58 * 1024 * 1024

_NEG_INF = float("-inf")


def _const_spec(shape):
    zeros = (0,) * len(shape)
    return pl.BlockSpec(shape, lambda *_: zeros, pipeline_mode=pl.Buffered(1))


def _when(cond, fn):
    if cond is True:
        fn()
    else:
        pl.when(cond)(fn)


def _layer_norm(z, g, b):
    mu = jnp.mean(z, axis=-1, keepdims=True)
    zc = z - mu
    var = jnp.mean(zc * zc, axis=-1, keepdims=True)
    return zc * lax.rsqrt(var + LN_EPS) * g + b


def _silu(a):
    return a * (1.0 / (1.0 + jnp.exp(-a)))


TOKEN_TILE_ROWS = D_MODEL // LANES


def _to_token_tiles(x):
    rows = x.shape[0]
    return x.astype(BF16).reshape(rows, TOKEN_TILE_ROWS, LANES).reshape(rows * TOKEN_TILE_ROWS, LANES)


def _from_token_tiles(tiles):
    rows = tiles.shape[0] // TOKEN_TILE_ROWS
    return tiles.reshape(rows, TOKEN_TILE_ROWS, LANES).reshape(rows, D_MODEL)


def _mix_segment(h_ref, r0, cos_ref, sin_ref, dec_ref, kd_ref, qd_ref, wp_ref, bp_ref, ps_ref, gng_ref, gnb_ref,
                 s0_ref, h0_ref, cat_ref, snew_ref, hnew_ref, s_ref, u_ref,
                 *, seg_len, chunk, pos_start, first, g_chunk):
    hist = jnp.concatenate([jnp.zeros((1, POOL_WIDTH), F32), h0_ref[0]], axis=0)
    if first is True:
        s_ref[...] = s0_ref[0]
        u_ref[0:HIST_PAD, :] = hist
    else:
        s_ref[...] = jnp.where(first, s0_ref[0], s_ref[...])
        u_ref[0:HIST_PAD, :] = jnp.where(first, hist, u_ref[0:HIST_PAD, :])

    u_new = h_ref[pl.ds(r0, seg_len), 0:POOL_WIDTH]
    u_ref[HIST_PAD:HIST_PAD + seg_len, :] = u_new
    pos = (pos_start + lax.broadcasted_iota(jnp.int32, (seg_len, 1), 0)).astype(F32)
    for gi, w in enumerate(POOL_WINDOWS):
        cols = slice(gi * POOL_CH, (gi + 1) * POOL_CH)
        win = u_ref[HIST_PAD:HIST_PAD + seg_len, cols]
        for back in range(1, w):
            win = win + u_ref[HIST_PAD - back:HIST_PAD - back + seg_len, cols]
        cnt = jnp.minimum(float(w), pos + 1.0)
        d = win * (1.0 / cnt) - u_ref[HIST_PAD:HIST_PAD + seg_len, cols]
        pooled = jnp.dot(d.astype(BF16), wp_ref[gi], preferred_element_type=F32) + bp_ref[:, cols]
        cat_ref[:, cols] = (pooled * ps_ref[:, cols]).astype(BF16)

    hnew_ref[0] = u_ref[seg_len + 1:seg_len + HIST_PAD, :]
    u_ref[0:HIST_PAD, :] = u_ref[seg_len:seg_len + HIST_PAD, :]

    scale = RET_HEAD_DIM ** -0.5
    for c in range(seg_len // chunk):
        rows = pl.ds(r0 + c * chunk, chunk)
        trows = slice(c * chunk, (c + 1) * chunk)
        cos_t = cos_ref[trows, :]
        sin_t = sin_ref[trows, :]
        for hd in range(RET_HEADS):
            lo = hd * RET_HEAD_DIM
            hcols = slice(lo, lo + RET_HEAD_DIM)
            q = h_ref[rows, POOL_WIDTH + lo:POOL_WIDTH + lo + RET_HEAD_DIM]
            k = h_ref[rows, POOL_WIDTH + RET_WIDTH + lo:POOL_WIDTH + RET_WIDTH + lo + RET_HEAD_DIM]
            v = h_ref[rows, POOL_WIDTH + 2 * RET_WIDTH + lo:POOL_WIDTH + 2 * RET_WIDTH + lo + RET_HEAD_DIM]
            gate = h_ref[rows, POOL_WIDTH + 3 * RET_WIDTH + lo:POOL_WIDTH + 3 * RET_WIDTH + lo + RET_HEAD_DIM]
            qr = q * cos_t + pltpu.roll(q, RET_HEAD_DIM // 2, 1) * sin_t
            kr = (k * cos_t + pltpu.roll(k, RET_HEAD_DIM // 2, 1) * sin_t) * scale
            vb = v.astype(BF16)
            scores = lax.dot_general(qr.astype(BF16), kr.astype(BF16), (((1,), (1,)), ((), ())),
                                     preferred_element_type=F32) * dec_ref[hd]
            o = jnp.dot(scores.astype(BF16), vb, preferred_element_type=F32)
            s_prev = s_ref[hd]
            o = o + jnp.dot((qr * qd_ref[hd]).astype(BF16), s_prev.astype(BF16), preferred_element_type=F32)
            upd = lax.dot_general((kr * kd_ref[hd]).astype(BF16), vb, (((0,), (0,)), ((), ())),
                                  preferred_element_type=F32)
            s_ref[hd] = g_chunk[hd] * s_prev + upd
            mu = jnp.mean(o, axis=-1, keepdims=True)
            oc = o - mu
            var = jnp.mean(oc * oc, axis=-1, keepdims=True)
            on = oc * lax.rsqrt(var + GN_EPS) * gng_ref[:, hcols] + gnb_ref[:, hcols]
            cat_ref[trows, POOL_WIDTH + lo:POOL_WIDTH + lo + RET_HEAD_DIM] = (on * _silu(gate)).astype(BF16)

    snew_ref[0] = s_ref[...]


def _mixer_segments_kernel(x_ref, w_in_ref, *refs, seg_len, chunk, pos0, g_chunk):
    (*mix_refs, h_ref, s_ref, u_ref) = refs
    j = pl.program_id(0)

    @pl.when(j == 0)
    def _project():
        h_ref[...] = jnp.dot(x_ref[...].astype(BF16), w_in_ref[...], preferred_element_type=F32)

    _mix_segment(h_ref, pl.multiple_of(j * seg_len, seg_len), *mix_refs, s_ref, u_ref,
                 seg_len=seg_len, chunk=chunk, pos_start=pos0, first=True, g_chunk=g_chunk)


def _mixer_pipelined_kernel(x_ref, w_in_hbm, *refs, tiles_per_stream, seg_len, chunk, pos0, g_chunk):
    (*mix_refs, w_bf16_hbm, ha_ref, hb_ref, s_ref, u_ref, wb_ref, wsem, osem) = refs
    g = pl.program_id(0)
    m = jnp.maximum(g - 1, 0)
    first = ((m % tiles_per_stream) == 0) | (g == 0)
    pos_start = pos0 + (m % tiles_per_stream) * seg_len
    w_out_copy = pltpu.make_async_copy(wb_ref, w_bf16_hbm, osem)

    @pl.when(g == 0)
    def _stage_weight():
        stage = (ha_ref, hb_ref)
        chunk_rows = ha_ref.shape[0]
        n_chunks = wb_ref.shape[0] // chunk_rows
        copies = [pltpu.make_async_copy(w_in_hbm.at[0, pl.ds(c * chunk_rows, chunk_rows), :], stage[c % 2],
                                        wsem.at[c % 2]) for c in range(n_chunks)]
        for c in range(min(2, n_chunks)):
            copies[c].start()
        for c in range(n_chunks):
            copies[c].wait()
            wb_ref[pl.ds(c * chunk_rows, chunk_rows), :] = stage[c % 2][...].astype(BF16)
            if c + 2 < n_chunks:
                copies[c + 2].start()
        w_out_copy.start()
        hb_ref[...] = jnp.zeros_like(hb_ref)

    def step(h_write, h_read):
        _mix_segment(h_read, 0, *mix_refs, s_ref, u_ref, seg_len=seg_len, chunk=chunk, pos_start=pos_start,
                     first=first, g_chunk=g_chunk)
        h_write[...] = jnp.dot(x_ref[...].astype(BF16), wb_ref[...], preferred_element_type=F32)

    pl.when(g % 2 == 0)(lambda: step(ha_ref, hb_ref))
    pl.when(g % 2 == 1)(lambda: step(hb_ref, ha_ref))

    @pl.when(g == pl.num_programs(0) - 1)
    def _():
        w_out_copy.wait()


def _retention_tables(chunk, t_len, pos0):
    log_gamma = np.log(1.0 - 2.0 ** (-5.0 - np.arange(RET_HEADS, dtype=np.float64)))
    idx = np.arange(chunk, dtype=np.float64)
    diff = idx[:, None] - idx[None, :]
    dec = np.where(diff >= 0, np.exp(log_gamma[:, None, None] * np.maximum(diff, 0.0)), 0.0)
    kd = np.exp(log_gamma[:, None] * (chunk - 1.0 - idx)[None, :])
    qd = np.exp(log_gamma[:, None] * (idx + 1.0)[None, :])
    kd = np.broadcast_to(kd[:, :, None], (RET_HEADS, chunk, RET_HEAD_DIM))
    qd = np.broadcast_to(qd[:, :, None], (RET_HEADS, chunk, RET_HEAD_DIM))
    half = RET_HEAD_DIM // 2
    freqs = ROPE_BASE ** (-np.arange(half, dtype=np.float64) / half)
    pos = pos0 + np.arange(t_len, dtype=np.float64)
    ang = pos[:, None] * freqs[None, :]
    cos = np.cos(ang)
    sin = np.sin(ang)
    cos_t = np.concatenate([cos, cos], axis=-1)
    sin_t = np.concatenate([-sin, sin], axis=-1)
    g_chunk = tuple(math.exp(math.log(1.0 - 2.0 ** (-5.0 - h)) * chunk) for h in range(RET_HEADS))
    dec, kd, qd, cos_t, sin_t = (jnp.asarray(t, F32) for t in (dec, kd, qd, cos_t, sin_t))
    return dec, kd, qd, cos_t, sin_t, g_chunk


def _mixer(x2d, w_in, wp_b, bp, ps, gng, gnb, s0, h0, *, n_streams, t_len, tile_rows, chunk, pos0):
    rows = n_streams * t_len
    pipelined = t_len > tile_rows
    seg_len = tile_rows if pipelined else t_len
    assert (t_len % tile_rows == 0) if pipelined else (rows == tile_rows)
    assert seg_len % chunk == 0 and t_len >= POOL_HIST
    dec, kd, qd, cos_t, sin_t, g_chunk = _retention_tables(chunk, t_len, pos0)
    state_block = (1, RET_HEADS, RET_HEAD_DIM, RET_HEAD_DIM)
    hist_block = (1, POOL_HIST, POOL_WIDTH)

    if pipelined:
        n_tiles = rows // tile_rows
        tiles_per_stream = t_len // tile_rows
        grid = (n_tiles + 1,)
        mixed = lambda g: jnp.maximum(g - 1, 0)
        x_map = lambda g: (jnp.minimum(g, n_tiles - 1), 0)
        time_map = lambda g: (mixed(g) % tiles_per_stream, 0)
        cat_map = lambda g: (mixed(g), 0)
        state_map = lambda g: (mixed(g) // tiles_per_stream, 0, 0, 0)
        hist_map = lambda g: (mixed(g) // tiles_per_stream, 0, 0)
        kern = functools.partial(_mixer_pipelined_kernel, tiles_per_stream=tiles_per_stream, seg_len=seg_len,
                                 chunk=chunk, pos0=pos0, g_chunk=g_chunk)
        h_scratch = [pltpu.VMEM((tile_rows, IN_WIDTH), F32), pltpu.VMEM((tile_rows, IN_WIDTH), F32)]
        w_spec = pl.BlockSpec(memory_space=pl.ANY)
        extra_out_specs = [pl.BlockSpec(memory_space=pl.ANY)]
        extra_out_shape = [jax.ShapeDtypeStruct((D_MODEL, IN_WIDTH), BF16)]
        extra_scratch = [pltpu.VMEM((D_MODEL, IN_WIDTH), BF16), pltpu.SemaphoreType.DMA((2,)),
                         pltpu.SemaphoreType.DMA(())]
        assert D_MODEL % tile_rows == 0
    else:
        grid = (n_streams,)
        x_map = lambda j: (0, 0)
        time_map = lambda j: (0, 0)
        cat_map = lambda j: (j, 0)
        state_map = lambda j: (j, 0, 0, 0)
        hist_map = lambda j: (j, 0, 0)
        kern = functools.partial(_mixer_segments_kernel, seg_len=seg_len, chunk=chunk, pos0=pos0, g_chunk=g_chunk)
        h_scratch = [pltpu.VMEM((tile_rows, IN_WIDTH), F32)]
        w_spec = _const_spec((D_MODEL, IN_WIDTH))
        extra_out_specs, extra_out_shape, extra_scratch = [], [], []

    return pl.pallas_call(
        kern,
        grid=grid,
        in_specs=[
            pl.BlockSpec((tile_rows, D_MODEL), x_map),
            w_spec,
            pl.BlockSpec((seg_len, RET_HEAD_DIM), time_map),
            pl.BlockSpec((seg_len, RET_HEAD_DIM), time_map),
            _const_spec((RET_HEADS, chunk, chunk)),
            _const_spec((RET_HEADS, chunk, RET_HEAD_DIM)),
            _const_spec((RET_HEADS, chunk, RET_HEAD_DIM)),
            _const_spec((len(POOL_WINDOWS), POOL_CH, POOL_CH)),
            _const_spec((1, POOL_WIDTH)),
            _const_spec((1, POOL_WIDTH)),
            _const_spec((1, RET_WIDTH)),
            _const_spec((1, RET_WIDTH)),
            pl.BlockSpec(state_block, state_map),
            pl.BlockSpec(hist_block, hist_map),
        ],
        out_specs=[
            pl.BlockSpec((seg_len, D_MODEL), cat_map),
            pl.BlockSpec(state_block, state_map),
            pl.BlockSpec(hist_block, hist_map),
        ] + extra_out_specs,
        out_shape=[
            jax.ShapeDtypeStruct((rows, D_MODEL), BF16),
            jax.ShapeDtypeStruct((n_streams, RET_HEADS, RET_HEAD_DIM, RET_HEAD_DIM), F32),
            jax.ShapeDtypeStruct((n_streams, POOL_HIST, POOL_WIDTH), F32),
        ] + extra_out_shape,
        scratch_shapes=h_scratch + [
            pltpu.VMEM((RET_HEADS, RET_HEAD_DIM, RET_HEAD_DIM), F32),
            pltpu.VMEM((HIST_PAD + seg_len, POOL_WIDTH), F32),
        ] + extra_scratch,
        compiler_params=pltpu.CompilerParams(dimension_semantics=("arbitrary",),
                                             vmem_limit_bytes=VMEM_LIMIT),
        name="mixer",
    )(x2d, w_in, cos_t, sin_t, dec, kd, qd, wp_b, bp, ps, gng, gnb, s0, h0)


def _row_sources(arrays, n_tiles_first):
    if len(arrays) == 1:
        return [pl.BlockSpec((ROW_TILE, arrays[0].shape[1]), lambda i: (i, 0))]
    first, second = arrays
    return [
        pl.BlockSpec((ROW_TILE, first.shape[1]), lambda i: (jnp.minimum(i, n_tiles_first - 1), 0)),
        pl.BlockSpec((ROW_TILE, second.shape[1]), lambda i: (jnp.maximum(i - n_tiles_first, 0), 0),
                     pipeline_mode=pl.Buffered(1)),
    ]


def _mm_ln_kernel(*refs, n_a, n_res, n_tiles_first, emit_packed):
    a_refs = refs[:n_a]
    w_ref = refs[n_a]
    res_refs = refs[n_a + 1:n_a + 1 + n_res]
    g_ref, b_ref, o_ref = refs[n_a + 1 + n_res:n_a + 4 + n_res]
    tiles_ref = refs[n_a + 4 + n_res] if emit_packed else None
    wb_ref = refs[-1]
    i = pl.program_id(0)

    @pl.when(i == 0)
    def _():
        wb_ref[...] = w_ref[0].astype(BF16)

    def body(a_ref, res_ref):
        half = ROW_TILE // 2
        for h in range(2):
            rows = slice(h * half, (h + 1) * half)
            acc = jnp.dot(a_ref[rows, :].astype(BF16), wb_ref[...], preferred_element_type=F32)
            out = _layer_norm(ALPHA * res_ref[rows, :] + acc, g_ref[...], b_ref[...])
            o_ref[rows, :] = out
            if emit_packed:
                tile_rows = slice(h * half * TOKEN_TILE_ROWS, (h + 1) * half * TOKEN_TILE_ROWS)
                tiles_ref[tile_rows, :] = _to_token_tiles(out)

    pl.when(i < n_tiles_first)(lambda: body(a_refs[0], res_refs[0]))
    pl.when(i >= n_tiles_first)(lambda: body(a_refs[-1], res_refs[-1]))


def _mm_ln(a_arrays, w, res_arrays, g, b, *, n_rows, n_tiles_first, name, emit_packed=False):
    n_tiles = n_rows // ROW_TILE
    kern = functools.partial(_mm_ln_kernel, n_a=len(a_arrays), n_res=len(res_arrays),
                             n_tiles_first=n_tiles_first, emit_packed=emit_packed)
    out_specs = [pl.BlockSpec((ROW_TILE, D_MODEL), lambda i: (i, 0))]
    out_shape = [jax.ShapeDtypeStruct((n_rows, D_MODEL), F32)]
    if emit_packed:
        out_specs.append(pl.BlockSpec((ROW_TILE * TOKEN_TILE_ROWS, LANES), lambda i: (i, 0)))
        out_shape.append(jax.ShapeDtypeStruct((n_rows * TOKEN_TILE_ROWS, LANES), BF16))
    return pl.pallas_call(
        kern,
        grid=(n_tiles,),
        in_specs=(_row_sources(a_arrays, n_tiles_first) + [_const_spec(w.shape)]
                  + _row_sources(res_arrays, n_tiles_first)
                  + [_const_spec((1, D_MODEL)), _const_spec((1, D_MODEL))]),
        out_specs=out_specs,
        out_shape=out_shape,
        scratch_shapes=[pltpu.VMEM(w.shape[1:], BF16)],
        compiler_params=pltpu.CompilerParams(dimension_semantics=("arbitrary",),
                                             vmem_limit_bytes=VMEM_LIMIT),
        name=name,
    )(*a_arrays, w, *res_arrays, g, b)


def _matmul_kernel(a_ref, w_ref, o_ref, *rest):
    (*split_refs, wb_ref) = rest

    @pl.when(pl.program_id(0) == 0)
    def _():
        wb_ref[...] = w_ref[0].astype(BF16)

    out = jnp.dot(a_ref[...].astype(BF16), wb_ref[...], preferred_element_type=F32).astype(o_ref.dtype)
    o_ref[...] = out
    for ref in split_refs:
        ref[...] = out.reshape(ref.shape)


def _matmul(a, w, out_dtype, name, heads=None):
    n_rows = a.shape[0]
    out_specs = [pl.BlockSpec((ROW_TILE, w.shape[2]), lambda i: (i, 0))]
    out_shape = [jax.ShapeDtypeStruct((n_rows, w.shape[2]), out_dtype)]
    if heads is not None:
        out_specs.append(pl.BlockSpec((ROW_TILE, heads, w.shape[2] // heads), lambda i: (i, 0, 0)))
        out_shape.append(jax.ShapeDtypeStruct((n_rows, heads, w.shape[2] // heads), out_dtype))
    return pl.pallas_call(
        _matmul_kernel,
        grid=(n_rows // ROW_TILE,),
        in_specs=[pl.BlockSpec((ROW_TILE, a.shape[1]), lambda i: (i, 0)), _const_spec(w.shape)],
        out_specs=out_specs,
        out_shape=out_shape,
        scratch_shapes=[pltpu.VMEM(w.shape[1:], BF16)],
        compiler_params=pltpu.CompilerParams(dimension_semantics=("arbitrary",),
                                             vmem_limit_bytes=VMEM_LIMIT),
        name=name,
    )(a, w)


def _attention_kernel(q_ref, k_ref, v_ref, o_ref):
    scale = MEM_HEAD_DIM ** -0.5
    heads_split = len(k_ref.shape) == 5

    def rows_by_model_dim(ref):
        if heads_split:
            return ref[0, 0].astype(BF16).reshape(N_MEM, D_MODEL)
        return ref[0].astype(BF16)

    k = rows_by_model_dim(k_ref)
    v = rows_by_model_dim(v_ref)
    for h in range(MEM_HEADS):
        cols = slice(h * MEM_HEAD_DIM, (h + 1) * MEM_HEAD_DIM)
        s = lax.dot_general(q_ref[:, cols], k[:, cols], (((1,), (1,)), ((), ())),
                            preferred_element_type=F32) * scale
        m = jnp.max(s, axis=-1, keepdims=True)
        p = jnp.exp(s - m)
        p = p * (1.0 / jnp.sum(p, axis=-1, keepdims=True))
        o_ref[:, cols] = jnp.dot(p.astype(BF16), v[:, cols], preferred_element_type=F32).astype(o_ref.dtype)


def _attention(q_all, mem_k, mem_v, *, n_streams, t_len, q_rows, row_offset, name):
    tiles_per_stream = t_len // q_rows
    base = row_offset // q_rows
    if mem_k.ndim == 5:
        kv_spec = pl.BlockSpec((1, 1, N_MEM, MEM_HEADS, MEM_HEAD_DIM), lambda b, t: (0, b, 0, 0, 0))
    else:
        kv_spec = pl.BlockSpec((1, N_MEM, D_MODEL), lambda b, t: (b, 0, 0))
    return pl.pallas_call(
        _attention_kernel,
        grid=(n_streams, tiles_per_stream),
        in_specs=[
            pl.BlockSpec((q_rows, D_MODEL), lambda b, t: (base + b * tiles_per_stream + t, 0)),
            kv_spec,
            kv_spec,
        ],
        out_specs=pl.BlockSpec((q_rows, D_MODEL), lambda b, t: (b * tiles_per_stream + t, 0)),
        out_shape=jax.ShapeDtypeStruct((n_streams * t_len, D_MODEL), BF16),
        compiler_params=pltpu.CompilerParams(dimension_semantics=("arbitrary", "arbitrary"),
                                             vmem_limit_bytes=VMEM_LIMIT),
        name=name,
    )(q_all, mem_k, mem_v)


_EXPERT_LANE0 = N_GROUPS
(_META_LANE1, _META_LANE2, _META_POS1, _META_POS2, _META_GATE1, _META_GATE2) = range(6)
(_OUT_SLOT1, _OUT_SLOT2, _OUT_GATE1, _OUT_GATE2) = range(4)
_MOE_TILE_LOG2 = MOE_TILE.bit_length() - 1
assert 1 << _MOE_TILE_LOG2 == MOE_TILE


_ROUTE_ROWS = 48


def _router_kernel(x_ref, whl_ref, wh_ref, out_ref, counts_ref, carry_ref, meta_ref, start_ref, later_ref):
    phase = pl.program_id(0)
    i = pl.program_id(1)
    n_tok = x_ref.shape[0]
    row = lax.broadcasted_iota(jnp.int32, (_ROUTE_ROWS, n_tok), 0)

    @pl.when((phase == 0) & (i == 0))
    def _():
        carry_ref[...] = jnp.zeros_like(carry_ref)
        later_ref[...] = (lax.broadcasted_iota(jnp.int32, (n_tok, n_tok), 0)
                          < lax.broadcasted_iota(jnp.int32, (n_tok, n_tok), 1)).astype(BF16)

    @pl.when(phase == 0)
    def _route():
        x = x_ref[...]
        xh = x.astype(BF16)
        xl = (x - xh.astype(F32)).astype(BF16)
        both = jnp.dot(xh, whl_ref[...], preferred_element_type=F32)
        logits = both[:, :LANES] + jnp.dot(xl, wh_ref[...], preferred_element_type=F32) + both[:, LANES:]
        lt = logits.T[0:_ROUTE_ROWS, :]

        def first_argmax(vals):
            m = jnp.max(vals, axis=0, keepdims=True)
            idx = jnp.min(jnp.where(vals == m, row, LANES), axis=0, keepdims=True)
            return m, idx

        gl = jnp.where(row < N_GROUPS, lt, _NEG_INF)
        gm, g_idx = first_argmax(gl)
        g_w = 1.0 / jnp.sum(jnp.exp(gl - gm), axis=0, keepdims=True)

        in_group = ((row >= _EXPERT_LANE0) & (row < _EXPERT_LANE0 + N_EXPERTS)
                    & (((row - _EXPERT_LANE0) >> 3) == g_idx))
        el = jnp.where(in_group, lt, _NEG_INF)
        m1, i1 = first_argmax(el)
        z = jnp.sum(jnp.exp(el - m1), axis=0, keepdims=True)
        m2, i2 = first_argmax(jnp.where(row == i1, _NEG_INF, el))
        p1 = 1.0 / z
        p2 = jnp.exp(m2 - m1) / z
        den = p1 + p2
        gate1 = p1 / den * g_w
        gate2 = p2 / den * g_w

        hit1 = row == i1
        hit2 = row == i2
        onehot = (hit1 | hit2).astype(BF16)
        carry = carry_ref[0:_ROUTE_ROWS, :]
        rank = jnp.dot(onehot, later_ref[...], preferred_element_type=F32) + carry
        pos1 = jnp.sum(jnp.where(hit1, rank, 0.0), axis=0, keepdims=True)
        pos2 = jnp.sum(jnp.where(hit2, rank, 0.0), axis=0, keepdims=True)
        carry_ref[0:_ROUTE_ROWS, :] = carry + jnp.sum(onehot.astype(F32), axis=1, keepdims=True)

        zero = jnp.zeros_like(pos1)
        meta_ref[i] = jnp.concatenate([i1.astype(F32), i2.astype(F32), pos1, pos2, gate1, gate2, zero, zero], axis=0)

    @pl.when((phase == 1) & (i == 0))
    def _segment_starts():
        counts_ref[...] = carry_ref[...]
        tiles = ((carry_ref[...].astype(jnp.int32) + (MOE_TILE - 1)) >> _MOE_TILE_LOG2).astype(F32).astype(BF16)
        before = (lax.broadcasted_iota(jnp.int32, (LANES, LANES), 1)
                  < lax.broadcasted_iota(jnp.int32, (LANES, LANES), 0)).astype(BF16)
        tiles_wide = jnp.broadcast_to(tiles, (LANES, LANES))
        start_ref[...] = jnp.dot(before, tiles_wide, preferred_element_type=F32)[:, 0:1] * float(MOE_TILE)

    @pl.when(phase == 1)
    def _slots():
        meta = meta_ref[i]
        starts = start_ref[0:_ROUTE_ROWS, :]

        def start_of(expert_row):
            return jnp.sum(jnp.where(row == expert_row.astype(jnp.int32), starts, 0.0), axis=0, keepdims=True)

        slot1 = start_of(meta[_META_LANE1:_META_LANE1 + 1]) + meta[_META_POS1:_META_POS1 + 1]
        slot2 = start_of(meta[_META_LANE2:_META_LANE2 + 1]) + meta[_META_POS2:_META_POS2 + 1]
        zero = jnp.zeros_like(slot1)
        out_ref[...] = jnp.concatenate([slot1, slot2, meta[_META_GATE1:_META_GATE1 + 1],
                                        meta[_META_GATE2:_META_GATE2 + 1], zero, zero, zero, zero], axis=0)


def _router(x_all, w_router):
    n_rows = x_all.shape[0]
    n_tiles = n_rows // ROW_TILE
    w_hi = w_router.astype(BF16)
    w_lo = (w_router - w_hi.astype(F32)).astype(BF16)
    w_hi_lo = jnp.concatenate([w_hi, w_lo], axis=1)
    return pl.pallas_call(
        _router_kernel,
        grid=(2, n_tiles),
        in_specs=[pl.BlockSpec((ROW_TILE, D_MODEL), lambda p, i: (i * (1 - p) + (n_tiles - 1) * p, 0)),
                  _const_spec((D_MODEL, 2 * LANES)), _const_spec((D_MODEL, LANES))],
        out_specs=[pl.BlockSpec((8, ROW_TILE), lambda p, i: (0, i * p)),
                   pl.BlockSpec((LANES, 1), lambda p, i: (0, 0))],
        out_shape=[jax.ShapeDtypeStruct((8, n_rows), F32), jax.ShapeDtypeStruct((LANES, 1), F32)],
        scratch_shapes=[pltpu.VMEM((LANES, 1), F32), pltpu.VMEM((n_tiles, 8, ROW_TILE), F32),
                        pltpu.VMEM((LANES, 1), F32), pltpu.VMEM((ROW_TILE, ROW_TILE), BF16)],
        compiler_params=pltpu.CompilerParams(dimension_semantics=("arbitrary", "arbitrary"),
                                             vmem_limit_bytes=VMEM_LIMIT),
        name="router",
    )(x_all, w_hi_lo, w_hi)


def _token_rows(first_token, n_tokens):
    return pl.ds(pl.multiple_of(first_token * TOKEN_TILE_ROWS, TOKEN_TILE_ROWS), n_tokens * TOKEN_TILE_ROWS)


_TAIL_BITS = MOE_TILE.bit_length() - 1


def _dispatch_kernel(slots_ref, tail_start_ref, tail_len_ref, x_ref, xs_hbm, zeros_ref, sem, zsem):
    i = pl.program_id(0)
    n_tokens = x_ref.shape[0] // TOKEN_TILE_ROWS
    group = 16

    def issue(j, carry):
        for u in range(group):
            r = j * group + u
            src = x_ref.at[pl.ds(pl.multiple_of(r * TOKEN_TILE_ROWS, TOKEN_TILE_ROWS), TOKEN_TILE_ROWS), :]
            for k in range(2):
                slot = slots_ref[2 * (i * n_tokens + r) + k]
                pltpu.make_async_copy(src, xs_hbm.at[_token_rows(slot, 1), :], sem).start(priority=k)
        return carry
    lax.fori_loop(0, n_tokens // group, issue, 0)

    def drain(j, carry):
        pltpu.make_async_copy(x_ref.at[pl.ds(0, 2 * group * TOKEN_TILE_ROWS), :],
                              xs_hbm.at[_token_rows(0, 2 * group), :], sem).wait()
        return carry
    lax.fori_loop(0, n_tokens // group, drain, 0)

    def tail_copies(fn):
        def per_expert(ex, carry):
            length = tail_len_ref[ex]
            pos = tail_start_ref[ex]
            for bit in reversed(range(_TAIL_BITS)):
                piece = 1 << bit
                take = (length & piece) != 0
                cp = pltpu.make_async_copy(zeros_ref.at[pl.ds(0, piece * TOKEN_TILE_ROWS), :],
                                           xs_hbm.at[_token_rows(pos, piece), :], zsem)
                pl.when(take)(lambda cp=cp: fn(cp))
                pos = pos + jnp.where(take, piece, 0)
            return carry
        lax.fori_loop(0, N_EXPERTS, per_expert, 0)

    def unused_copies(fn):
        half = zeros_ref.shape[0] // TOKEN_TILE_ROWS
        first_unused = tail_start_ref[N_EXPERTS - 1] + tail_len_ref[N_EXPERTS - 1]

        def per_half(j, carry):
            fn(pltpu.make_async_copy(zeros_ref, xs_hbm.at[_token_rows(first_unused + j * half, half), :], zsem))
            return carry
        lax.fori_loop(0, (xs_hbm.shape[0] // TOKEN_TILE_ROWS - first_unused) // half, per_half, 0)

    @pl.when(i == pl.num_programs(0) - 1)
    def _tails():
        zeros_ref[...] = jnp.zeros_like(zeros_ref)
        tail_copies(lambda cp: cp.start())
        unused_copies(lambda cp: cp.start())
        tail_copies(lambda cp: cp.wait())
        unused_copies(lambda cp: cp.wait())


def _dispatch(slots, tail_start, tail_len, xg, n_tiles_max):
    block_rows = xg.shape[0] // DISPATCH_STEPS
    assert xg.shape[0] % DISPATCH_STEPS == 0 and block_rows % (16 * TOKEN_TILE_ROWS) == 0
    return pl.pallas_call(
        _dispatch_kernel,
        grid_spec=pltpu.PrefetchScalarGridSpec(
            num_scalar_prefetch=3,
            grid=(xg.shape[0] // block_rows,),
            in_specs=[pl.BlockSpec((block_rows, LANES), lambda i, sl, ts, tl: (i, 0))],
            out_specs=pl.BlockSpec(memory_space=pl.ANY),
            scratch_shapes=[pltpu.VMEM((MOE_TILE // 2 * TOKEN_TILE_ROWS, LANES), BF16),
                            pltpu.SemaphoreType.DMA(()), pltpu.SemaphoreType.DMA(())],
        ),
        out_shape=jax.ShapeDtypeStruct((n_tiles_max * MOE_TILE * TOKEN_TILE_ROWS, LANES), BF16),
        compiler_params=pltpu.CompilerParams(dimension_semantics=("arbitrary",),
                                             vmem_limit_bytes=VMEM_LIMIT),
        name="dispatch",
    )(slots, tail_start, tail_len, xg)


_WEIGHT_PIECES = 8


def _experts_kernel(tile_start_ref, n_active_ref, xs_hbm, w1_hbm, w3_hbm, w2_hbm, y_hbm,
                    xbuf, ybuf, w1f, w3f, w2f, w1b, w3b, w2b, gsem, ysem, wsem, *, n_tiles_max):
    e = pl.program_id(0)
    n_active = n_active_ref[0]

    def weight_copies(expert, piece_index, fn):
        slot = expert % 2
        grp, idx = expert // EXPERTS_PER_GROUP, expert % EXPERTS_PER_GROUP
        for hbm, stage in ((w1_hbm, w1f), (w3_hbm, w3f), (w2_hbm, w2f)):
            piece = stage.shape[1] // _WEIGHT_PIECES
            rows = pl.ds(pl.multiple_of(piece_index * piece, piece), piece)
            fn(pltpu.make_async_copy(hbm.at[0, grp, idx, rows, :], stage.at[slot, rows, :], wsem.at[slot]))

    first_tile = tile_start_ref[e]
    n_tiles = tile_start_ref[e + 1] - first_tile

    def start_next_weights(share):
        def start_piece(c, carry):
            weight_copies(e + 1, c, lambda cp: cp.start())
            return carry

        @pl.when(e + 1 < N_EXPERTS)
        def _():
            lax.fori_loop((share * _WEIGHT_PIECES) // (n_tiles + 1), ((share + 1) * _WEIGHT_PIECES) // (n_tiles + 1),
                          start_piece, 0)

    @pl.when(e == 0)
    def _():
        for c in range(_WEIGHT_PIECES):
            weight_copies(e, c, lambda cp: cp.start())

    start_next_weights(0)
    for c in range(_WEIGHT_PIECES):
        weight_copies(e, c, lambda cp: cp.wait())

    def x_copy(tile, buf):
        return pltpu.make_async_copy(xs_hbm.at[_token_rows(tile * MOE_TILE, MOE_TILE), :], xbuf.at[buf], gsem.at[buf])

    def y_copy(tile, buf):
        return pltpu.make_async_copy(ybuf.at[buf], y_hbm.at[_token_rows(tile * MOE_TILE, MOE_TILE), :], ysem.at[buf])

    @pl.when(e == 0)
    def _():
        x_copy(0, 0).start()

    w1b[...] = w1f[e % 2].astype(BF16)
    w3b[...] = w3f[e % 2].astype(BF16)
    w2b[...] = w2f[e % 2].astype(BF16)

    def tile_body(t, carry):
        g = first_tile + t
        buf = g % 2
        x_copy(g, buf).wait()

        @pl.when(g >= 2)
        def _():
            y_copy(g, buf).wait()

        @pl.when(g + 1 < n_active)
        def _():
            x_copy(g + 1, 1 - buf).start()

        start_next_weights(t + 1)

        x = _from_token_tiles(xbuf[buf])
        a = jnp.dot(x, w1b[...], preferred_element_type=F32)
        b = jnp.dot(x, w3b[...], preferred_element_type=F32)
        hdn = (_silu(a) * b).astype(BF16)
        y = jnp.dot(hdn, w2b[...], preferred_element_type=F32)
        ybuf[buf] = _to_token_tiles(y)
        y_copy(g, buf).start()
        return carry

    lax.fori_loop(0, n_tiles, tile_body, 0)

    @pl.when(e == N_EXPERTS - 1)
    def _drain():
        @pl.when(n_active >= 2)
        def _():
            y_copy(0, n_active % 2).wait()

        @pl.when(n_active >= 1)
        def _():
            y_copy(0, (n_active + 1) % 2).wait()

        ybuf[0] = jnp.zeros(ybuf.shape[1:], ybuf.dtype)

        def fill(g, carry):
            cp = y_copy(g, 0)
            cp.start()
            cp.wait()
            return carry
        lax.fori_loop(n_active, n_tiles_max, fill, 0)


def _experts(tile_start, n_active, x_sorted, w1, w3, w2, n_tiles_max):
    tile_words = MOE_TILE * TOKEN_TILE_ROWS
    kern = functools.partial(_experts_kernel, n_tiles_max=n_tiles_max)
    return pl.pallas_call(
        kern,
        grid_spec=pltpu.PrefetchScalarGridSpec(
            num_scalar_prefetch=2,
            grid=(N_EXPERTS,),
            in_specs=[pl.BlockSpec(memory_space=pl.ANY)] * 4,
            out_specs=pl.BlockSpec(memory_space=pl.ANY),
            scratch_shapes=[
                pltpu.VMEM((2, tile_words, LANES), BF16),
                pltpu.VMEM((2, tile_words, LANES), BF16),
                pltpu.VMEM((2, D_MODEL, EXPERT_HIDDEN), F32),
                pltpu.VMEM((2, D_MODEL, EXPERT_HIDDEN), F32),
                pltpu.VMEM((2, EXPERT_HIDDEN, D_MODEL), F32),
                pltpu.VMEM((D_MODEL, EXPERT_HIDDEN), BF16),
                pltpu.VMEM((D_MODEL, EXPERT_HIDDEN), BF16),
                pltpu.VMEM((EXPERT_HIDDEN, D_MODEL), BF16),
                pltpu.SemaphoreType.DMA((2,)),
                pltpu.SemaphoreType.DMA((2,)),
                pltpu.SemaphoreType.DMA((2,)),
            ],
        ),
        out_shape=jax.ShapeDtypeStruct((n_tiles_max * tile_words, LANES), BF16),
        compiler_params=pltpu.CompilerParams(dimension_semantics=("arbitrary",),
                                             vmem_limit_bytes=VMEM_LIMIT),
        name="experts",
    )(tile_start, n_active, x_sorted, w1, w3, w2)


def _combine_kernel(slot_ref, y_hbm, gates_ref, x_ref, g_ref, b_ref, op_ref, os_ref, ybuf, sem,
                    *, n_tiles, n_tiles_first):
    i = pl.program_id(0)

    tile_words = COMBINE_TILE * TOKEN_TILE_ROWS

    def gather(tile, buf):
        for r in range(COMBINE_TILE):
            tok = tile * COMBINE_TILE + r
            for k in range(2):
                row0 = pl.multiple_of(slot_ref[2 * tok + k] * TOKEN_TILE_ROWS, TOKEN_TILE_ROWS)
                pltpu.make_async_copy(y_hbm.at[pl.ds(row0, TOKEN_TILE_ROWS), :],
                                      ybuf.at[buf, k, pl.ds(r * TOKEN_TILE_ROWS, TOKEN_TILE_ROWS), :],
                                      sem.at[buf]).start(priority=k)

    buf = i % 2

    def wait_all():
        for k in range(2):
            pltpu.make_async_copy(y_hbm.at[pl.ds(0, tile_words), :], ybuf.at[buf, k], sem.at[buf]).wait()

    def compute():
        y = (gates_ref[:, 0:1] * _from_token_tiles(ybuf[buf, 0]).astype(F32)
             + gates_ref[:, 1:2] * _from_token_tiles(ybuf[buf, 1]).astype(F32))
        out = _layer_norm(ALPHA * x_ref[...] + y, g_ref[...], b_ref[...])

        @pl.when(i < n_tiles_first)
        def _():
            op_ref[...] = out

        @pl.when(i >= n_tiles_first)
        def _():
            os_ref[...] = out

    @pl.when(i == 0)
    def _():
        gather(0, 0)

    @pl.when(i + 1 < n_tiles)
    def _steady():
        wait_all()
        gather(i + 1, 1 - buf)
        compute()

    @pl.when(i + 1 == n_tiles)
    def _last():
        wait_all()
        compute()


def _combine(slots, y_sorted, gates, x_all, g, b, n_rows_first):
    n_rows = x_all.shape[0]
    n_tiles = n_rows // COMBINE_TILE
    n_first = n_rows_first // COMBINE_TILE
    kern = functools.partial(_combine_kernel, n_tiles=n_tiles, n_tiles_first=n_first)
    return pl.pallas_call(
        kern,
        grid_spec=pltpu.PrefetchScalarGridSpec(
            num_scalar_prefetch=1,
            grid=(n_tiles,),
            in_specs=[
                pl.BlockSpec(memory_space=pl.ANY),
                pl.BlockSpec((COMBINE_TILE, 2), lambda i, sl: (i, 0)),
                pl.BlockSpec((COMBINE_TILE, D_MODEL), lambda i, sl: (i, 0)),
                pl.BlockSpec((1, D_MODEL), lambda i, sl: (0, 0)),
                pl.BlockSpec((1, D_MODEL), lambda i, sl: (0, 0)),
            ],
            out_specs=[
                pl.BlockSpec((COMBINE_TILE, D_MODEL), lambda i, sl: (jnp.minimum(i, n_first - 1), 0)),
                pl.BlockSpec((COMBINE_TILE, D_MODEL), lambda i, sl: (jnp.maximum(i - n_first, 0), 0)),
            ],
            scratch_shapes=[pltpu.VMEM((2, 2, COMBINE_TILE * TOKEN_TILE_ROWS, LANES), BF16),
                            pltpu.SemaphoreType.DMA((2,))],
        ),
        out_shape=[jax.ShapeDtypeStruct((n_rows_first, D_MODEL), F32),
                   jax.ShapeDtypeStruct((n_rows - n_rows_first, D_MODEL), F32)],
        compiler_params=pltpu.CompilerParams(dimension_semantics=("arbitrary",),
                                             vmem_limit_bytes=VMEM_LIMIT),
        name="combine",
    )(slots, y_sorted, gates, x_all, g, b)


def _dispatch_plan(routed, counts):
    slots = routed[_OUT_SLOT1:_OUT_SLOT2 + 1].T.astype(jnp.int32).reshape(-1)
    gates = routed[_OUT_GATE1:_OUT_GATE2 + 1].T
    cnt = counts[_EXPERT_LANE0:_EXPERT_LANE0 + N_EXPERTS, 0].astype(jnp.int32)
    tiles = (cnt + MOE_TILE - 1) // MOE_TILE
    tile_end = jnp.cumsum(tiles)
    tile_start = jnp.concatenate([jnp.zeros((1,), jnp.int32), tile_end]).astype(jnp.int32)
    tail_start = (tile_start[:-1] * MOE_TILE + cnt).astype(jnp.int32)
    tail_len = (tiles * MOE_TILE - cnt).astype(jnp.int32)
    return tile_start, tile_end[-1:].astype(jnp.int32), slots, gates, tail_start, tail_len


def kernel(x_prompt, x_sample, mem_prompt, state_pool, state_ret, cache_mem_k, cache_mem_v, w_in, w_pool, b_pool,
           pool_scale, ret_gn_g, ret_gn_b, w_out, ln1_g, ln1_b, w_mq, w_mk, w_mv, w_mo, ln2_g, ln2_b, w_rg, w_re,
           w1, w3, w2, ln3_g, ln3_b):
    assert w_in.shape[0] == DEPTH == 1
    bp_n, tp, _ = x_prompt.shape
    bs_n, ts, _ = x_sample.shape
    rows_p, rows_s = bp_n * tp, bs_n * ts
    assert rows_s == ROW_TILE and rows_p % ROW_TILE == 0
    n_rows = rows_p + rows_s
    tiles_p = rows_p // ROW_TILE

    wp_b = w_pool[0].astype(BF16)
    bp = b_pool[0].reshape(1, POOL_WIDTH)
    ps = pool_scale[0].reshape(1, POOL_WIDTH)
    gng = ret_gn_g[0].reshape(1, RET_WIDTH)
    gnb = ret_gn_b[0].reshape(1, RET_WIDTH)
    row = lambda p: p[0].reshape(1, D_MODEL)

    xp2d = x_prompt.reshape(rows_p, D_MODEL)
    xs2d = x_sample.reshape(rows_s, D_MODEL)

    mem2d = mem_prompt.reshape(bp_n * N_MEM, D_MODEL)
    mk_p, mk_heads = _matmul(mem2d, w_mk, F32, "mem_k", heads=MEM_HEADS)
    mv_p, mv_heads = _matmul(mem2d, w_mv, F32, "mem_v", heads=MEM_HEADS)
    mk_p = mk_p.reshape(bp_n, N_MEM, D_MODEL)
    mv_p = mv_p.reshape(bp_n, N_MEM, D_MODEL)

    zeros_s = jnp.zeros((bp_n, RET_HEADS, RET_HEAD_DIM, RET_HEAD_DIM), F32)
    zeros_h = jnp.zeros((bp_n, POOL_HIST, POOL_WIDTH), F32)
    cat_p, ret_p, pool_p, w_in_b = _mixer(xp2d, w_in, wp_b, bp, ps, gng, gnb, zeros_s, zeros_h,
                                          n_streams=bp_n, t_len=tp, tile_rows=MIXER_TILE, chunk=MIXER_TILE, pos0=0)
    cat_s, ret_s, pool_s = _mixer(xs2d, w_in_b, wp_b, bp, ps, gng, gnb, state_ret[0], state_pool[0],
                                  n_streams=bs_n, t_len=ts, tile_rows=rows_s, chunk=ts, pos0=PAST_LEN)
    (x1,) = _mm_ln([cat_p, cat_s], w_out, [xp2d, xs2d], row(ln1_g), row(ln1_b),
                   n_rows=n_rows, n_tiles_first=tiles_p, name="out_ln1")

    (q_all,) = _matmul(x1, w_mq, BF16, "mem_q")
    o_p = _attention(q_all, mk_p, mv_p, n_streams=bp_n, t_len=tp, q_rows=ROW_TILE, row_offset=0, name="attn_prompt")
    o_s = _attention(q_all, cache_mem_k, cache_mem_v,
                     n_streams=bs_n, t_len=ts, q_rows=ts, row_offset=rows_p, name="attn_sample")
    x2, x2_tiles = _mm_ln([o_p, o_s], w_mo, [x1], row(ln2_g), row(ln2_b),
                          n_rows=n_rows, n_tiles_first=tiles_p, name="mo_ln2", emit_packed=True)

    w_router = jnp.concatenate([w_rg[0], w_re[0].reshape(D_MODEL, N_EXPERTS),
                                jnp.zeros((D_MODEL, LANES - N_GROUPS - N_EXPERTS), F32)], axis=1)
    routed, counts = _router(x2, w_router)
    n_tiles = (2 * n_rows) // MOE_TILE + N_EXPERTS
    tile_start, n_active, slots, gates, tail_start, tail_len = _dispatch_plan(routed, counts)
    x_sorted = _dispatch(slots, tail_start, tail_len, x2_tiles, n_tiles)
    y_sorted = _experts(tile_start, n_active, x_sorted, w1, w3, w2, n_tiles)
    y_p, y_s = _combine(slots, y_sorted, gates, x2, row(ln3_g), row(ln3_b), rows_p)

    kv_shape = (DEPTH, bp_n, N_MEM, MEM_HEADS, MEM_HEAD_DIM)
    return (y_p.reshape(bp_n, tp, D_MODEL), y_s.reshape(bs_n, ts, D_MODEL), pool_p[None], ret_p[None],
            mk_heads.reshape(kv_shape), mv_heads.reshape(kv_shape), pool_s[None], ret_s[None])
```

```python
import functools
import math

import jax
import jax.numpy as jnp
import numpy as np
from jax import lax
from jax.experimental import pallas as pl
from jax.experimental.pallas import tpu as pltpu

F32 = jnp.float32
BF16 = jnp.bfloat16

D_MODEL = 2048
POOL_WIDTH = 1024
POOL_WINDOWS = (2, 4, 8, 16)
POOL_CH = 256
POOL_HIST = 15
RET_WIDTH = 1024
RET_HEADS = 8
RET_HEAD_DIM = 128
IN_WIDTH = POOL_WIDTH + 4 * RET_WIDTH
ROPE_BASE = 10000.0
N_MEM = 256
MEM_HEADS = 4
MEM_HEAD_DIM = 512
N_GROUPS = 4
EXPERTS_PER_GROUP = 8
N_EXPERTS = N_GROUPS * EXPERTS_PER_GROUP
EXPERT_HIDDEN = 512
LN_EPS = 1e-5
GN_EPS = 1e-6
DEPTH = 1
ALPHA = (2.0 * DEPTH) ** 0.25
PAST_LEN = 2048

LANES = 128
HIST_PAD = 16
ROW_TILE = 512
MIXER_TILE = 256
MOE_TILE = 256
COMBINE_TILE = 256
DISPATCH_STEPS = 4
VMEM_LIMIT = 58 * 1024 * 1024

_NEG_INF = float("-inf")


def _const_spec(shape):
    zeros = (0,) * len(shape)
    return pl.BlockSpec(shape, lambda *_: zeros, pipeline_mode=pl.Buffered(1))


def _when(cond, fn):
    if cond is True:
        fn()
    else:
        pl.when(cond)(fn)


def _layer_norm(z, g, b):
    mu = jnp.mean(z, axis=-1, keepdims=True)
    zc = z - mu
    var = jnp.mean(zc * zc, axis=-1, keepdims=True)
    return zc * lax.rsqrt(var + LN_EPS) * g + b


def _silu(a):
    return a * (1.0 / (1.0 + jnp.exp(-a)))


TOKEN_TILE_ROWS = D_MODEL // LANES


def _to_token_tiles(x):
    rows = x.shape[0]
    return x.astype(BF16).reshape(rows, TOKEN_TILE_ROWS, LANES).reshape(rows * TOKEN_TILE_ROWS, LANES)


def _from_token_tiles(tiles):
    rows = tiles.shape[0] // TOKEN_TILE_ROWS
    return tiles.reshape(rows, TOKEN_TILE_ROWS, LANES).reshape(rows, D_MODEL)


def _mix_segment(h_ref, r0, cos_ref, sin_ref, dec_ref, kd_ref, qd_ref, wp_ref, bp_ref, ps_ref, gng_ref, gnb_ref,
                 s0_ref, h0_ref, cat_ref, snew_ref, hnew_ref, s_ref, u_ref,
                 *, seg_len, chunk, pos_start, first, g_chunk):
    hist = jnp.concatenate([jnp.zeros((1, POOL_WIDTH), F32), h0_ref[0]], axis=0)
    if first is True:
        s_ref[...] = s0_ref[0]
        u_ref[0:HIST_PAD, :] = hist
    else:
        s_ref[...] = jnp.where(first, s0_ref[0], s_ref[...])
        u_ref[0:HIST_PAD, :] = jnp.where(first, hist, u_ref[0:HIST_PAD, :])

    u_new = h_ref[pl.ds(r0, seg_len), 0:POOL_WIDTH]
    u_ref[HIST_PAD:HIST_PAD + seg_len, :] = u_new
    pos = (pos_start + lax.broadcasted_iota(jnp.int32, (seg_len, 1), 0)).astype(F32)
    for gi, w in enumerate(POOL_WINDOWS):
        cols = slice(gi * POOL_CH, (gi + 1) * POOL_CH)
        win = u_ref[HIST_PAD:HIST_PAD + seg_len, cols]
        for back in range(1, w):
            win = win + u_ref[HIST_PAD - back:HIST_PAD - back + seg_len, cols]
        cnt = jnp.minimum(float(w), pos + 1.0)
        d = win * (1.0 / cnt) - u_ref[HIST_PAD:HIST_PAD + seg_len, cols]
        pooled = jnp.dot(d.astype(BF16), wp_ref[gi], preferred_element_type=F32) + bp_ref[:, cols]
        cat_ref[:, cols] = (pooled * ps_ref[:, cols]).astype(BF16)

    hnew_ref[0] = u_ref[seg_len + 1:seg_len + HIST_PAD, :]
    u_ref[0:HIST_PAD, :] = u_ref[seg_len:seg_len + HIST_PAD, :]

    scale = RET_HEAD_DIM ** -0.5
    for c in range(seg_len // chunk):
        rows = pl.ds(r0 + c * chunk, chunk)
        trows = slice(c * chunk, (c + 1) * chunk)
        cos_t = cos_ref[trows, :]
        sin_t = sin_ref[trows, :]
        for hd in range(RET_HEADS):
            lo = hd * RET_HEAD_DIM
            hcols = slice(lo, lo + RET_HEAD_DIM)
            q = h_ref[rows, POOL_WIDTH + lo:POOL_WIDTH + lo + RET_HEAD_DIM]
            k = h_ref[rows, POOL_WIDTH + RET_WIDTH + lo:POOL_WIDTH + RET_WIDTH + lo + RET_HEAD_DIM]
            v = h_ref[rows, POOL_WIDTH + 2 * RET_WIDTH + lo:POOL_WIDTH + 2 * RET_WIDTH + lo + RET_HEAD_DIM]
            gate = h_ref[rows, POOL_WIDTH + 3 * RET_WIDTH + lo:POOL_WIDTH + 3 * RET_WIDTH + lo + RET_HEAD_DIM]
            qr = q * cos_t + pltpu.roll(q, RET_HEAD_DIM // 2, 1) * sin_t
            kr = (k * cos_t + pltpu.roll(k, RET_HEAD_DIM // 2, 1) * sin_t) * scale
            vb = v.astype(BF16)
            scores = lax.dot_general(qr.astype(BF16), kr.astype(BF16), (((1,), (1,)), ((), ())),
                                     preferred_element_type=F32) * dec_ref[hd]
            o = jnp.dot(scores.astype(BF16), vb, preferred_element_type=F32)
            s_prev = s_ref[hd]
            o = o + jnp.dot((qr * qd_ref[hd]).astype(BF16), s_prev.astype(BF16), preferred_element_type=F32)
            upd = lax.dot_general((kr * kd_ref[hd]).astype(BF16), vb, (((0,), (0,)), ((), ())),
                                  preferred_element_type=F32)
            s_ref[hd] = g_chunk[hd] * s_prev + upd
            mu = jnp.mean(o, axis=-1, keepdims=True)
            oc = o - mu
            var = jnp.mean(oc * oc, axis=-1, keepdims=True)
            on = oc * lax.rsqrt(var + GN_EPS) * gng_ref[:, hcols] + gnb_ref[:, hcols]
            cat_ref[trows, POOL_WIDTH + lo:POOL_WIDTH + lo + RET_HEAD_DIM] = (on * _silu(gate)).astype(BF16)

    snew_ref[0] = s_ref[...]


def _mixer_segments_kernel(x_ref, w_in_ref, *refs, seg_len, chunk, pos0, g_chunk):
    (*mix_refs, h_ref, s_ref, u_ref) = refs
    j = pl.program_id(0)

    @pl.when(j == 0)
    def _project():
        h_ref[...] = jnp.dot(x_ref[...].astype(BF16), w_in_ref[...], preferred_element_type=F32)

    _mix_segment(h_ref, pl.multiple_of(j * seg_len, seg_len), *mix_refs, s_ref, u_ref,
                 seg_len=seg_len, chunk=chunk, pos_start=pos0, first=True, g_chunk=g_chunk)


def _mixer_pipelined_kernel(x_ref, w_in_hbm, *refs, tiles_per_stream, seg_len, chunk, pos0, g_chunk):
    (*mix_refs, w_bf16_hbm, ha_ref, hb_ref, s_ref, u_ref, wb_ref, wsem, osem) = refs
    g = pl.program_id(0)
    m = jnp.maximum(g - 1, 0)
    first = ((m % tiles_per_stream) == 0) | (g == 0)
    pos_start = pos0 + (m % tiles_per_stream) * seg_len
    w_out_copy = pltpu.make_async_copy(wb_ref, w_bf16_hbm, osem)

    @pl.when(g == 0)
    def _stage_weight():
        stage = (ha_ref, hb_ref)
        chunk_rows = ha_ref.shape[0]
        n_chunks = wb_ref.shape[0] // chunk_rows
        copies = [pltpu.make_async_copy(w_in_hbm.at[0, pl.ds(c * chunk_rows, chunk_rows), :], stage[c % 2],
                                        wsem.at[c % 2]) for c in range(n_chunks)]
        for c in range(min(2, n_chunks)):
            copies[c].start()
        for c in range(n_chunks):
            copies[c].wait()
            wb_ref[pl.ds(c * chunk_rows, chunk_rows), :] = stage[c % 2][...].astype(BF16)
            if c + 2 < n_chunks:
                copies[c + 2].start()
        w_out_copy.start()
        hb_ref[...] = jnp.zeros_like(hb_ref)

    def step(h_write, h_read):
        _mix_segment(h_read, 0, *mix_refs, s_ref, u_ref, seg_len=seg_len, chunk=chunk, pos_start=pos_start,
                     first=first, g_chunk=g_chunk)
        h_write[...] = jnp.dot(x_ref[...].astype(BF16), wb_ref[...], preferred_element_type=F32)

    pl.when(g % 2 == 0)(lambda: step(ha_ref, hb_ref))
    pl.when(g % 2 == 1)(lambda: step(hb_ref, ha_ref))

    @pl.when(g == pl.num_programs(0) - 1)
    def _():
        w_out_copy.wait()


def _retention_tables(chunk, t_len, pos0):
    log_gamma = np.log(1.0 - 2.0 ** (-5.0 - np.arange(RET_HEADS, dtype=np.float64)))
    idx = np.arange(chunk, dtype=np.float64)
    diff = idx[:, None] - idx[None, :]
    dec = np.where(diff >= 0, np.exp(log_gamma[:, None, None] * np.maximum(diff, 0.0)), 0.0)
    kd = np.exp(log_gamma[:, None] * (chunk - 1.0 - idx)[None, :])
    qd = np.exp(log_gamma[:, None] * (idx + 1.0)[None, :])
    kd = np.broadcast_to(kd[:, :, None], (RET_HEADS, chunk, RET_HEAD_DIM))
    qd = np.broadcast_to(qd[:, :, None], (RET_HEADS, chunk, RET_HEAD_DIM))
    half = RET_HEAD_DIM // 2
    freqs = ROPE_BASE ** (-np.arange(half, dtype=np.float64) / half)
    pos = pos0 + np.arange(t_len, dtype=np.float64)
    ang = pos[:, None] * freqs[None, :]
    cos = np.cos(ang)
    sin = np.sin(ang)
    cos_t = np.concatenate([cos, cos], axis=-1)
    sin_t = np.concatenate([-sin, sin], axis=-1)
    g_chunk = tuple(math.exp(math.log(1.0 - 2.0 ** (-5.0 - h)) * chunk) for h in range(RET_HEADS))
    dec, kd, qd, cos_t, sin_t = (jnp.asarray(t, F32) for t in (dec, kd, qd, cos_t, sin_t))
    return dec, kd, qd, cos_t, sin_t, g_chunk


def _mixer(x2d, w_in, wp_b, bp, ps, gng, gnb, s0, h0, *, n_streams, t_len, tile_rows, chunk, pos0):
    rows = n_streams * t_len
    pipelined = t_len > tile_rows
    seg_len = tile_rows if pipelined else t_len
    assert (t_len % tile_rows == 0) if pipelined else (rows == tile_rows)
    assert seg_len % chunk == 0 and t_len >= POOL_HIST
    dec, kd, qd, cos_t, sin_t, g_chunk = _retention_tables(chunk, t_len, pos0)
    state_block = (1, RET_HEADS, RET_HEAD_DIM, RET_HEAD_DIM)
    hist_block = (1, POOL_HIST, POOL_WIDTH)

    if pipelined:
        n_tiles = rows // tile_rows
        tiles_per_stream = t_len // tile_rows
        grid = (n_tiles + 1,)
        mixed = lambda g: jnp.maximum(g - 1, 0)
        x_map = lambda g: (jnp.minimum(g, n_tiles - 1), 0)
        time_map = lambda g: (mixed(g) % tiles_per_stream, 0)
        cat_map = lambda g: (mixed(g), 0)
        state_map = lambda g: (mixed(g) // tiles_per_stream, 0, 0, 0)
        hist_map = lambda g: (mixed(g) // tiles_per_stream, 0, 0)
        kern = functools.partial(_mixer_pipelined_kernel, tiles_per_stream=tiles_per_stream, seg_len=seg_len,
                                 chunk=chunk, pos0=pos0, g_chunk=g_chunk)
        h_scratch = [pltpu.VMEM((tile_rows, IN_WIDTH), F32), pltpu.VMEM((tile_rows, IN_WIDTH), F32)]
        w_spec = pl.BlockSpec(memory_space=pl.ANY)
        extra_out_specs = [pl.BlockSpec(memory_space=pl.ANY)]
        extra_out_shape = [jax.ShapeDtypeStruct((D_MODEL, IN_WIDTH), BF16)]
        extra_scratch = [pltpu.VMEM((D_MODEL, IN_WIDTH), BF16), pltpu.SemaphoreType.DMA((2,)),
                         pltpu.SemaphoreType.DMA(())]
        assert D_MODEL % tile_rows == 0
    else:
        grid = (n_streams,)
        x_map = lambda j: (0, 0)
        time_map = lambda j: (0, 0)
        cat_map = lambda j: (j, 0)
        state_map = lambda j: (j, 0, 0, 0)
        hist_map = lambda j: (j, 0, 0)
        kern = functools.partial(_mixer_segments_kernel, seg_len=seg_len, chunk=chunk, pos0=pos0, g_chunk=g_chunk)
        h_scratch = [pltpu.VMEM((tile_rows, IN_WIDTH), F32)]
        w_spec = _const_spec((D_MODEL, IN_WIDTH))
        extra_out_specs, extra_out_shape, extra_scratch = [], [], []

    return pl.pallas_call(
        kern,
        grid=grid,
        in_specs=[
            pl.BlockSpec((tile_rows, D_MODEL), x_map),
            w_spec,
            pl.BlockSpec((seg_len, RET_HEAD_DIM), time_map),
            pl.BlockSpec((seg_len, RET_HEAD_DIM), time_map),
            _const_spec((RET_HEADS, chunk, chunk)),
            _const_spec((RET_HEADS, chunk, RET_HEAD_DIM)),
            _const_spec((RET_HEADS, chunk, RET_HEAD_DIM)),
            _const_spec((len(POOL_WINDOWS), POOL_CH, POOL_CH)),
            _const_spec((1, POOL_WIDTH)),
            _const_spec((1, POOL_WIDTH)),
            _const_spec((1, RET_WIDTH)),
            _const_spec((1, RET_WIDTH)),
            pl.BlockSpec(state_block, state_map),
            pl.BlockSpec(hist_block, hist_map),
        ],
        out_specs=[
            pl.BlockSpec((seg_len, D_MODEL), cat_map),
            pl.BlockSpec(state_block, state_map),
            pl.BlockSpec(hist_block, hist_map),
        ] + extra_out_specs,
        out_shape=[
            jax.ShapeDtypeStruct((rows, D_MODEL), BF16),
            jax.ShapeDtypeStruct((n_streams, RET_HEADS, RET_HEAD_DIM, RET_HEAD_DIM), F32),
            jax.ShapeDtypeStruct((n_streams, POOL_HIST, POOL_WIDTH), F32),
        ] + extra_out_shape,
        scratch_shapes=h_scratch + [
            pltpu.VMEM((RET_HEADS, RET_HEAD_DIM, RET_HEAD_DIM), F32),
            pltpu.VMEM((HIST_PAD + seg_len, POOL_WIDTH), F32),
        ] + extra_scratch,
        compiler_params=pltpu.CompilerParams(dimension_semantics=("arbitrary",),
                                             vmem_limit_bytes=VMEM_LIMIT),
        name="mixer",
    )(x2d, w_in, cos_t, sin_t, dec, kd, qd, wp_b, bp, ps, gng, gnb, s0, h0)


def _row_sources(arrays, n_tiles_first):
    if len(arrays) == 1:
        return [pl.BlockSpec((ROW_TILE, arrays[0].shape[1]), lambda i: (i, 0))]
    first, second = arrays
    return [
        pl.BlockSpec((ROW_TILE, first.shape[1]), lambda i: (jnp.minimum(i, n_tiles_first - 1), 0)),
        pl.BlockSpec((ROW_TILE, second.shape[1]), lambda i: (jnp.maximum(i - n_tiles_first, 0), 0),
                     pipeline_mode=pl.Buffered(1)),
    ]


def _mm_ln_kernel(*refs, n_a, n_res, n_tiles_first, emit_packed):
    a_refs = refs[:n_a]
    w_ref = refs[n_a]
    res_refs = refs[n_a + 1:n_a + 1 + n_res]
    g_ref, b_ref, o_ref = refs[n_a + 1 + n_res:n_a + 4 + n_res]
    tiles_ref = refs[n_a + 4 + n_res] if emit_packed else None
    wb_ref = refs[-1]
    i = pl.program_id(0)

    @pl.when(i == 0)
    def _():
        wb_ref[...] = w_ref[0].astype(BF16)

    def body(a_ref, res_ref):
        half = ROW_TILE // 2
        for h in range(2):
            rows = slice(h * half, (h + 1) * half)
            acc = jnp.dot(a_ref[rows, :].astype(BF16), wb_ref[...], preferred_element_type=F32)
            out = _layer_norm(ALPHA * res_ref[rows, :] + acc, g_ref[...], b_ref[...])
            o_ref[rows, :] = out
            if emit_packed:
                tile_rows = slice(h * half * TOKEN_TILE_ROWS, (h + 1) * half * TOKEN_TILE_ROWS)
                tiles_ref[tile_rows, :] = _to_token_tiles(out)

    pl.when(i < n_tiles_first)(lambda: body(a_refs[0], res_refs[0]))
    pl.when(i >= n_tiles_first)(lambda: body(a_refs[-1], res_refs[-1]))


def _mm_ln(a_arrays, w, res_arrays, g, b, *, n_rows, n_tiles_first, name, emit_packed=False):
    n_tiles = n_rows // ROW_TILE
    kern = functools.partial(_mm_ln_kernel, n_a=len(a_arrays), n_res=len(res_arrays),
                             n_tiles_first=n_tiles_first, emit_packed=emit_packed)
    out_specs = [pl.BlockSpec((ROW_TILE, D_MODEL), lambda i: (i, 0))]
    out_shape = [jax.ShapeDtypeStruct((n_rows, D_MODEL), F32)]
    if emit_packed:
        out_specs.append(pl.BlockSpec((ROW_TILE * TOKEN_TILE_ROWS, LANES), lambda i: (i, 0)))
        out_shape.append(jax.ShapeDtypeStruct((n_rows * TOKEN_TILE_ROWS, LANES), BF16))
    return pl.pallas_call(
        kern,
        grid=(n_tiles,),
        in_specs=(_row_sources(a_arrays, n_tiles_first) + [_const_spec(w.shape)]
                  + _row_sources(res_arrays, n_tiles_first)
                  + [_const_spec((1, D_MODEL)), _const_spec((1, D_MODEL))]),
        out_specs=out_specs,
        out_shape=out_shape,
        scratch_shapes=[pltpu.VMEM(w.shape[1:], BF16)],
        compiler_params=pltpu.CompilerParams(dimension_semantics=("arbitrary",),
                                             vmem_limit_bytes=VMEM_LIMIT),
        name=name,
    )(*a_arrays, w, *res_arrays, g, b)


def _matmul_kernel(a_ref, w_ref, o_ref, *rest):
    (*split_refs, wb_ref) = rest

    @pl.when(pl.program_id(0) == 0)
    def _():
        wb_ref[...] = w_ref[0].astype(BF16)

    out = jnp.dot(a_ref[...].astype(BF16), wb_ref[...], preferred_element_type=F32).astype(o_ref.dtype)
    o_ref[...] = out
    for ref in split_refs:
        ref[...] = out.reshape(ref.shape)


def _matmul(a, w, out_dtype, name, heads=None):
    n_rows = a.shape[0]
    out_specs = [pl.BlockSpec((ROW_TILE, w.shape[2]), lambda i: (i, 0))]
    out_shape = [jax.ShapeDtypeStruct((n_rows, w.shape[2]), out_dtype)]
    if heads is not None:
        out_specs.append(pl.BlockSpec((ROW_TILE, heads, w.shape[2] // heads), lambda i: (i, 0, 0)))
        out_shape.append(jax.ShapeDtypeStruct((n_rows, heads, w.shape[2] // heads), out_dtype))
    return pl.pallas_call(
        _matmul_kernel,
        grid=(n_rows // ROW_TILE,),
        in_specs=[pl.BlockSpec((ROW_TILE, a.shape[1]), lambda i: (i, 0)), _const_spec(w.shape)],
        out_specs=out_specs,
        out_shape=out_shape,
        scratch_shapes=[pltpu.VMEM(w.shape[1:], BF16)],
        compiler_params=pltpu.CompilerParams(dimension_semantics=("arbitrary",),
                                             vmem_limit_bytes=VMEM_LIMIT),
        name=name,
    )(a, w)


def _attention_kernel(q_ref, k_ref, v_ref, o_ref):
    scale = MEM_HEAD_DIM ** -0.5
    heads_split = len(k_ref.shape) == 5

    def rows_by_model_dim(ref):
        if heads_split:
            return ref[0, 0].astype(BF16).reshape(N_MEM, D_MODEL)
        return ref[0].astype(BF16)

    k = rows_by_model_dim(k_ref)
    v = rows_by_model_dim(v_ref)
    for h in range(MEM_HEADS):
        cols = slice(h * MEM_HEAD_DIM, (h + 1) * MEM_HEAD_DIM)
        s = lax.dot_general(q_ref[:, cols], k[:, cols], (((1,), (1,)), ((), ())),
                            preferred_element_type=F32) * scale
        m = jnp.max(s, axis=-1, keepdims=True)
        p = jnp.exp(s - m)
        p = p * (1.0 / jnp.sum(p, axis=-1, keepdims=True))
        o_ref[:, cols] = jnp.dot(p.astype(BF16), v[:, cols], preferred_element_type=F32).astype(o_ref.dtype)


def _attention(q_all, mem_k, mem_v, *, n_streams, t_len, q_rows, row_offset, name):
    tiles_per_stream = t_len // q_rows
    base = row_offset // q_rows
    if mem_k.ndim == 5:
        kv_spec = pl.BlockSpec((1, 1, N_MEM, MEM_HEADS, MEM_HEAD_DIM), lambda b, t: (0, b, 0, 0, 0))
    else:
        kv_spec = pl.BlockSpec((1, N_MEM, D_MODEL), lambda b, t: (b, 0, 0))
    return pl.pallas_call(
        _attention_kernel,
        grid=(n_streams, tiles_per_stream),
        in_specs=[
            pl.BlockSpec((q_rows, D_MODEL), lambda b, t: (base + b * tiles_per_stream + t, 0)),
            kv_spec,
            kv_spec,
        ],
        out_specs=pl.BlockSpec((q_rows, D_MODEL), lambda b, t: (b * tiles_per_stream + t, 0)),
        out_shape=jax.ShapeDtypeStruct((n_streams * t_len, D_MODEL), BF16),
        compiler_params=pltpu.CompilerParams(dimension_semantics=("arbitrary", "arbitrary"),
                                             vmem_limit_bytes=VMEM_LIMIT),
        name=name,
    )(q_all, mem_k, mem_v)


_EXPERT_LANE0 = N_GROUPS
(_META_LANE1, _META_LANE2, _META_POS1, _META_POS2, _META_GATE1, _META_GATE2) = range(6)
(_OUT_SLOT1, _OUT_SLOT2, _OUT_GATE1, _OUT_GATE2) = range(4)
_MOE_TILE_LOG2 = MOE_TILE.bit_length() - 1
assert 1 << _MOE_TILE_LOG2 == MOE_TILE


_ROUTE_ROWS = 48


def _router_kernel(x_ref, whl_ref, wh_ref, out_ref, counts_ref, carry_ref, meta_ref, start_ref, later_ref):
    phase = pl.program_id(0)
    i = pl.program_id(1)
    n_tok = x_ref.shape[0]
    row = lax.broadcasted_iota(jnp.int32, (_ROUTE_ROWS, n_tok), 0)

    @pl.when((phase == 0) & (i == 0))
    def _():
        carry_ref[...] = jnp.zeros_like(carry_ref)
        later_ref[...] = (lax.broadcasted_iota(jnp.int32, (n_tok, n_tok), 0)
                          < lax.broadcasted_iota(jnp.int32, (n_tok, n_tok), 1)).astype(BF16)

    @pl.when(phase == 0)
    def _route():
        x = x_ref[...]
        xh = x.astype(BF16)
        xl = (x - xh.astype(F32)).astype(BF16)
        both = jnp.dot(xh, whl_ref[...], preferred_element_type=F32)
        logits = both[:, :LANES] + jnp.dot(xl, wh_ref[...], preferred_element_type=F32) + both[:, LANES:]
        lt = logits.T[0:_ROUTE_ROWS, :]

        def first_argmax(vals):
            m = jnp.max(vals, axis=0, keepdims=True)
            idx = jnp.min(jnp.where(vals == m, row, LANES), axis=0, keepdims=True)
            return m, idx

        gl = jnp.where(row < N_GROUPS, lt, _NEG_INF)
        gm, g_idx = first_argmax(gl)
        g_w = 1.0 / jnp.sum(jnp.exp(gl - gm), axis=0, keepdims=True)

        in_group = ((row >= _EXPERT_LANE0) & (row < _EXPERT_LANE0 + N_EXPERTS)
                    & (((row - _EXPERT_LANE0) >> 3) == g_idx))
        el = jnp.where(in_group, lt, _NEG_INF)
        m1, i1 = first_argmax(el)
        z = jnp.sum(jnp.exp(el - m1), axis=0, keepdims=True)
        m2, i2 = first_argmax(jnp.where(row == i1, _NEG_INF, el))
        p1 = 1.0 / z
        p2 = jnp.exp(m2 - m1) / z
        den = p1 + p2
        gate1 = p1 / den * g_w
        gate2 = p2 / den * g_w

        hit1 = row == i1
        hit2 = row == i2
        onehot = (hit1 | hit2).astype(BF16)
        carry = carry_ref[0:_ROUTE_ROWS, :]
        rank = jnp.dot(onehot, later_ref[...], preferred_element_type=F32) + carry
        pos1 = jnp.sum(jnp.where(hit1, rank, 0.0), axis=0, keepdims=True)
        pos2 = jnp.sum(jnp.where(hit2, rank, 0.0), axis=0, keepdims=True)
        carry_ref[0:_ROUTE_ROWS, :] = carry + jnp.sum(onehot.astype(F32), axis=1, keepdims=True)

        zero = jnp.zeros_like(pos1)
        meta_ref[i] = jnp.concatenate([i1.astype(F32), i2.astype(F32), pos1, pos2, gate1, gate2, zero, zero], axis=0)

    @pl.when((phase == 1) & (i == 0))
    def _segment_starts():
        counts_ref[...] = carry_ref[...]
        tiles = ((carry_ref[...].astype(jnp.int32) + (MOE_TILE - 1)) >> _MOE_TILE_LOG2).astype(F32).astype(BF16)
        before = (lax.broadcasted_iota(jnp.int32, (LANES, LANES), 1)
                  < lax.broadcasted_iota(jnp.int32, (LANES, LANES), 0)).astype(BF16)
        tiles_wide = jnp.broadcast_to(tiles, (LANES, LANES))
        start_ref[...] = jnp.dot(before, tiles_wide, preferred_element_type=F32)[:, 0:1] * float(MOE_TILE)

    @pl.when(phase == 1)
    def _slots():
        meta = meta_ref[i]
        starts = start_ref[0:_ROUTE_ROWS, :]

        def start_of(expert_row):
            return jnp.sum(jnp.where(row == expert_row.astype(jnp.int32), starts, 0.0), axis=0, keepdims=True)

        slot1 = start_of(meta[_META_LANE1:_META_LANE1 + 1]) + meta[_META_POS1:_META_POS1 + 1]
        slot2 = start_of(meta[_META_LANE2:_META_LANE2 + 1]) + meta[_META_POS2:_META_POS2 + 1]
        zero = jnp.zeros_like(slot1)
        out_ref[...] = jnp.concatenate([slot1, slot2, meta[_META_GATE1:_META_GATE1 + 1],
                                        meta[_META_GATE2:_META_GATE2 + 1], zero, zero, zero, zero], axis=0)


def _router(x_all, w_router):
    n_rows = x_all.shape[0]
    n_tiles = n_rows // ROW_TILE
    w_hi = w_router.astype(BF16)
    w_lo = (w_router - w_hi.astype(F32)).astype(BF16)
    w_hi_lo = jnp.concatenate([w_hi, w_lo], axis=1)
    return pl.pallas_call(
        _router_kernel,
        grid=(2, n_tiles),
        in_specs=[pl.BlockSpec((ROW_TILE, D_MODEL), lambda p, i: (i * (1 - p) + (n_tiles - 1) * p, 0)),
                  _const_spec((D_MODEL, 2 * LANES)), _const_spec((D_MODEL, LANES))],
        out_specs=[pl.BlockSpec((8, ROW_TILE), lambda p, i: (0, i * p)),
                   pl.BlockSpec((LANES, 1), lambda p, i: (0, 0))],
        out_shape=[jax.ShapeDtypeStruct((8, n_rows), F32), jax.ShapeDtypeStruct((LANES, 1), F32)],
        scratch_shapes=[pltpu.VMEM((LANES, 1), F32), pltpu.VMEM((n_tiles, 8, ROW_TILE), F32),
                        pltpu.VMEM((LANES, 1), F32), pltpu.VMEM((ROW_TILE, ROW_TILE), BF16)],
        compiler_params=pltpu.CompilerParams(dimension_semantics=("arbitrary", "arbitrary"),
                                             vmem_limit_bytes=VMEM_LIMIT),
        name="router",
    )(x_all, w_hi_lo, w_hi)


def _token_rows(first_token, n_tokens):
    return pl.ds(pl.multiple_of(first_token * TOKEN_TILE_ROWS, TOKEN_TILE_ROWS), n_tokens * TOKEN_TILE_ROWS)


_TAIL_BITS = MOE_TILE.bit_length() - 1


def _dispatch_kernel(slots_ref, tail_start_ref, tail_len_ref, x_ref, xs_hbm, zeros_ref, sem, zsem):
    i = pl.program_id(0)
    n_tokens = x_ref.shape[0] // TOKEN_TILE_ROWS
    group = 16

    def issue(j, carry):
        for u in range(group):
            r = j * group + u
            src = x_ref.at[pl.ds(pl.multiple_of(r * TOKEN_TILE_ROWS, TOKEN_TILE_ROWS), TOKEN_TILE_ROWS), :]
            for k in range(2):
                row0 = pl.multiple_of(slots_ref[2 * (i * n_tokens + r) + k], TOKEN_TILE_ROWS)
                pltpu.make_async_copy(src, xs_hbm.at[pl.ds(row0, TOKEN_TILE_ROWS), :], sem).start(priority=k)
        return carry
    lax.fori_loop(0, n_tokens // group, issue, 0)

    def drain(j, carry):
        pltpu.make_async_copy(x_ref.at[pl.ds(0, 2 * group * TOKEN_TILE_ROWS), :],
                              xs_hbm.at[_token_rows(0, 2 * group), :], sem).wait()
        return carry
    lax.fori_loop(0, n_tokens // group, drain, 0)

    def tail_copies(fn):
        def per_expert(ex, carry):
            length = tail_len_ref[ex]
            pos = tail_start_ref[ex]
            for bit in reversed(range(_TAIL_BITS)):
                piece = 1 << bit
                take = (length & piece) != 0
                cp = pltpu.make_async_copy(zeros_ref.at[pl.ds(0, piece * TOKEN_TILE_ROWS), :],
                                           xs_hbm.at[_token_rows(pos, piece), :], zsem)
                pl.when(take)(lambda cp=cp: fn(cp))
                pos = pos + jnp.where(take, piece, 0)
            return carry
        lax.fori_loop(0, N_EXPERTS, per_expert, 0)

    def unused_copies(fn):
        half = zeros_ref.shape[0] // TOKEN_TILE_ROWS
        first_unused = tail_start_ref[N_EXPERTS - 1] + tail_len_ref[N_EXPERTS - 1]

        def per_half(j, carry):
            fn(pltpu.make_async_copy(zeros_ref, xs_hbm.at[_token_rows(first_unused + j * half, half), :], zsem))
            return carry
        lax.fori_loop(0, (xs_hbm.shape[0] // TOKEN_TILE_ROWS - first_unused) // half, per_half, 0)

    @pl.when(i == pl.num_programs(0) - 1)
    def _tails():
        zeros_ref[...] = jnp.zeros_like(zeros_ref)
        tail_copies(lambda cp: cp.start())
        unused_copies(lambda cp: cp.start())
        tail_copies(lambda cp: cp.wait())
        unused_copies(lambda cp: cp.wait())


def _dispatch(slots, tail_start, tail_len, xg, n_tiles_max):
    block_rows = xg.shape[0] // DISPATCH_STEPS
    assert xg.shape[0] % DISPATCH_STEPS == 0 and block_rows % (16 * TOKEN_TILE_ROWS) == 0
    return pl.pallas_call(
        _dispatch_kernel,
        grid_spec=pltpu.PrefetchScalarGridSpec(
            num_scalar_prefetch=3,
            grid=(xg.shape[0] // block_rows,),
            in_specs=[pl.BlockSpec((block_rows, LANES), lambda i, sl, ts, tl: (i, 0))],
            out_specs=pl.BlockSpec(memory_space=pl.ANY),
            scratch_shapes=[pltpu.VMEM((MOE_TILE // 2 * TOKEN_TILE_ROWS, LANES), BF16),
                            pltpu.SemaphoreType.DMA(()), pltpu.SemaphoreType.DMA(())],
        ),
        out_shape=jax.ShapeDtypeStruct((n_tiles_max * MOE_TILE * TOKEN_TILE_ROWS, LANES), BF16),
        compiler_params=pltpu.CompilerParams(dimension_semantics=("arbitrary",),
                                             vmem_limit_bytes=VMEM_LIMIT),
        name="dispatch",
    )(slots, tail_start, tail_len, xg)


_WEIGHT_PIECES = 8


def _experts_kernel(tile_start_ref, n_active_ref, xs_hbm, w1_hbm, w3_hbm, w2_hbm, y_hbm,
                    xbuf, ybuf, w1f, w3f, w2f, w1b, w3b, w2b, gsem, ysem, wsem, *, n_tiles_max):
    e = pl.program_id(0)
    n_active = n_active_ref[0]

    def weight_copies(expert, piece_index, fn):
        slot = expert % 2
        grp, idx = expert // EXPERTS_PER_GROUP, expert % EXPERTS_PER_GROUP
        for hbm, stage in ((w1_hbm, w1f), (w3_hbm, w3f), (w2_hbm, w2f)):
            piece = stage.shape[1] // _WEIGHT_PIECES
            rows = pl.ds(pl.multiple_of(piece_index * piece, piece), piece)
            fn(pltpu.make_async_copy(hbm.at[0, grp, idx, rows, :], stage.at[slot, rows, :], wsem.at[slot]))

    first_tile = tile_start_ref[e]
    n_tiles = tile_start_ref[e + 1] - first_tile

    def start_next_weights(share):
        def start_piece(c, carry):
            weight_copies(e + 1, c, lambda cp: cp.start())
            return carry

        @pl.when(e + 1 < N_EXPERTS)
        def _():
            lax.fori_loop((share * _WEIGHT_PIECES) // (n_tiles + 1), ((share + 1) * _WEIGHT_PIECES) // (n_tiles + 1),
                          start_piece, 0)

    @pl.when(e == 0)
    def _():
        for c in range(_WEIGHT_PIECES):
            weight_copies(e, c, lambda cp: cp.start())

    start_next_weights(0)
    for c in range(_WEIGHT_PIECES):
        weight_copies(e, c, lambda cp: cp.wait())

    def x_copy(tile, buf):
        return pltpu.make_async_copy(xs_hbm.at[_token_rows(tile * MOE_TILE, MOE_TILE), :], xbuf.at[buf], gsem.at[buf])

    def y_copy(tile, buf):
        return pltpu.make_async_copy(ybuf.at[buf], y_hbm.at[_token_rows(tile * MOE_TILE, MOE_TILE), :], ysem.at[buf])

    @pl.when(e == 0)
    def _():
        x_copy(0, 0).start()

    w1b[...] = w1f[e % 2].astype(BF16)
    w3b[...] = w3f[e % 2].astype(BF16)
    w2b[...] = w2f[e % 2].astype(BF16)

    def tile_body(t, carry):
        g = first_tile + t
        buf = g % 2
        x_copy(g, buf).wait()

        @pl.when(g >= 2)
        def _():
            y_copy(g, buf).wait()

        @pl.when(g + 1 < n_active)
        def _():
            x_copy(g + 1, 1 - buf).start()

        start_next_weights(t + 1)

        x = _from_token_tiles(xbuf[buf])
        a = jnp.dot(x, w1b[...], preferred_element_type=F32)
        b = jnp.dot(x, w3b[...], preferred_element_type=F32)
        hdn = (_silu(a) * b).astype(BF16)
        y = jnp.dot(hdn, w2b[...], preferred_element_type=F32)
        ybuf[buf] = _to_token_tiles(y)
        y_copy(g, buf).start()
        return carry

    lax.fori_loop(0, n_tiles, tile_body, 0)

    @pl.when(e == N_EXPERTS - 1)
    def _drain():
        @pl.when(n_active >= 2)
        def _():
            y_copy(0, n_active % 2).wait()

        @pl.when(n_active >= 1)
        def _():
            y_copy(0, (n_active + 1) % 2).wait()

        ybuf[0] = jnp.zeros(ybuf.shape[1:], ybuf.dtype)

        def fill(g, carry):
            cp = y_copy(g, 0)
            cp.start()
            cp.wait()
            return carry
        lax.fori_loop(n_active, n_tiles_max, fill, 0)


def _experts(tile_start, n_active, x_sorted, w1, w3, w2, n_tiles_max):
    tile_words = MOE_TILE * TOKEN_TILE_ROWS
    kern = functools.partial(_experts_kernel, n_tiles_max=n_tiles_max)
    return pl.pallas_call(
        kern,
        grid_spec=pltpu.PrefetchScalarGridSpec(
            num_scalar_prefetch=2,
            grid=(N_EXPERTS,),
            in_specs=[pl.BlockSpec(memory_space=pl.ANY)] * 4,
            out_specs=pl.BlockSpec(memory_space=pl.ANY),
            scratch_shapes=[
                pltpu.VMEM((2, tile_words, LANES), BF16),
                pltpu.VMEM((2, tile_words, LANES), BF16),
                pltpu.VMEM((2, D_MODEL, EXPERT_HIDDEN), F32),
                pltpu.VMEM((2, D_MODEL, EXPERT_HIDDEN), F32),
                pltpu.VMEM((2, EXPERT_HIDDEN, D_MODEL), F32),
                pltpu.VMEM((D_MODEL, EXPERT_HIDDEN), BF16),
                pltpu.VMEM((D_MODEL, EXPERT_HIDDEN), BF16),
                pltpu.VMEM((EXPERT_HIDDEN, D_MODEL), BF16),
                pltpu.SemaphoreType.DMA((2,)),
                pltpu.SemaphoreType.DMA((2,)),
                pltpu.SemaphoreType.DMA((2,)),
            ],
        ),
        out_shape=jax.ShapeDtypeStruct((n_tiles_max * tile_words, LANES), BF16),
        compiler_params=pltpu.CompilerParams(dimension_semantics=("arbitrary",),
                                             vmem_limit_bytes=VMEM_LIMIT),
        name="experts",
    )(tile_start, n_active, x_sorted, w1, w3, w2)


def _combine_kernel(slot_ref, y_hbm, gates_ref, x_ref, g_ref, b_ref, op_ref, os_ref, ybuf, sem,
                    *, n_tiles, n_tiles_first):
    i = pl.program_id(0)

    tile_words = COMBINE_TILE * TOKEN_TILE_ROWS

    def gather(tile, buf):
        for r in range(COMBINE_TILE):
            tok = tile * COMBINE_TILE + r
            for k in range(2):
                row0 = pl.multiple_of(slot_ref[2 * tok + k], TOKEN_TILE_ROWS)
                pltpu.make_async_copy(y_hbm.at[pl.ds(row0, TOKEN_TILE_ROWS), :],
                                      ybuf.at[buf, k, pl.ds(r * TOKEN_TILE_ROWS, TOKEN_TILE_ROWS), :],
                                      sem.at[buf]).start(priority=k)

    buf = i % 2

    def wait_all():
        for k in range(2):
            pltpu.make_async_copy(y_hbm.at[pl.ds(0, tile_words), :], ybuf.at[buf, k], sem.at[buf]).wait()

    def compute():
        y = (gates_ref[:, 0:1] * _from_token_tiles(ybuf[buf, 0]).astype(F32)
             + gates_ref[:, 1:2] * _from_token_tiles(ybuf[buf, 1]).astype(F32))
        out = _layer_norm(ALPHA * x_ref[...] + y, g_ref[...], b_ref[...])

        @pl.when(i < n_tiles_first)
        def _():
            op_ref[...] = out

        @pl.when(i >= n_tiles_first)
        def _():
            os_ref[...] = out

    @pl.when(i == 0)
    def _():
        gather(0, 0)

    @pl.when(i + 1 < n_tiles)
    def _steady():
        wait_all()
        gather(i + 1, 1 - buf)
        compute()

    @pl.when(i + 1 == n_tiles)
    def _last():
        wait_all()
        compute()


def _combine(slots, y_sorted, gates, x_all, g, b, n_rows_first):
    n_rows = x_all.shape[0]
    n_tiles = n_rows // COMBINE_TILE
    n_first = n_rows_first // COMBINE_TILE
    kern = functools.partial(_combine_kernel, n_tiles=n_tiles, n_tiles_first=n_first)
    return pl.pallas_call(
        kern,
        grid_spec=pltpu.PrefetchScalarGridSpec(
            num_scalar_prefetch=1,
            grid=(n_tiles,),
            in_specs=[
                pl.BlockSpec(memory_space=pl.ANY),
                pl.BlockSpec((COMBINE_TILE, 2), lambda i, sl: (i, 0)),
                pl.BlockSpec((COMBINE_TILE, D_MODEL), lambda i, sl: (i, 0)),
                pl.BlockSpec((1, D_MODEL), lambda i, sl: (0, 0)),
                pl.BlockSpec((1, D_MODEL), lambda i, sl: (0, 0)),
            ],
            out_specs=[
                pl.BlockSpec((COMBINE_TILE, D_MODEL), lambda i, sl: (jnp.minimum(i, n_first - 1), 0)),
                pl.BlockSpec((COMBINE_TILE, D_MODEL), lambda i, sl: (jnp.maximum(i - n_first, 0), 0)),
            ],
            scratch_shapes=[pltpu.VMEM((2, 2, COMBINE_TILE * TOKEN_TILE_ROWS, LANES), BF16),
                            pltpu.SemaphoreType.DMA((2,))],
        ),
        out_shape=[jax.ShapeDtypeStruct((n_rows_first, D_MODEL), F32),
                   jax.ShapeDtypeStruct((n_rows - n_rows_first, D_MODEL), F32)],
        compiler_params=pltpu.CompilerParams(dimension_semantics=("arbitrary",),
                                             vmem_limit_bytes=VMEM_LIMIT),
        name="combine",
    )(slots, y_sorted, gates, x_all, g, b)


def _dispatch_plan(routed, counts):
    slots = routed[_OUT_SLOT1:_OUT_SLOT2 + 1].T.astype(jnp.int32).reshape(-1) * TOKEN_TILE_ROWS
    gates = routed[_OUT_GATE1:_OUT_GATE2 + 1].T
    cnt = counts[_EXPERT_LANE0:_EXPERT_LANE0 + N_EXPERTS, 0].astype(jnp.int32)
    tiles = (cnt + MOE_TILE - 1) // MOE_TILE
    tile_end = jnp.cumsum(tiles)
    tile_start = jnp.concatenate([jnp.zeros((1,), jnp.int32), tile_end]).astype(jnp.int32)
    tail_start = (tile_start[:-1] * MOE_TILE + cnt).astype(jnp.int32)
    tail_len = (tiles * MOE_TILE - cnt).astype(jnp.int32)
    return tile_start, tile_end[-1:].astype(jnp.int32), slots, gates, tail_start, tail_len


def kernel(x_prompt, x_sample, mem_prompt, state_pool, state_ret, cache_mem_k, cache_mem_v, w_in, w_pool, b_pool,
           pool_scale, ret_gn_g, ret_gn_b, w_out, ln1_g, ln1_b, w_mq, w_mk, w_mv, w_mo, ln2_g, ln2_b, w_rg, w_re,
           w1, w3, w2, ln3_g, ln3_b):
    assert w_in.shape[0] == DEPTH == 1
    bp_n, tp, _ = x_prompt.shape
    bs_n, ts, _ = x_sample.shape
    rows_p, rows_s = bp_n * tp, bs_n * ts
    assert rows_s == ROW_TILE and rows_p % ROW_TILE == 0
    n_rows = rows_p + rows_s
    tiles_p = rows_p // ROW_TILE

    wp_b = w_pool[0].astype(BF16)
    bp = b_pool[0].reshape(1, POOL_WIDTH)
    ps = pool_scale[0].reshape(1, POOL_WIDTH)
    gng = ret_gn_g[0].reshape(1, RET_WIDTH)
    gnb = ret_gn_b[0].reshape(1, RET_WIDTH)
    row = lambda p: p[0].reshape(1, D_MODEL)

    xp2d = x_prompt.reshape(rows_p, D_MODEL)
    xs2d = x_sample.reshape(rows_s, D_MODEL)

    mem2d = mem_prompt.reshape(bp_n * N_MEM, D_MODEL)
    mk_p, mk_heads = _matmul(mem2d, w_mk, F32, "mem_k", heads=MEM_HEADS)
    mv_p, mv_heads = _matmul(mem2d, w_mv, F32, "mem_v", heads=MEM_HEADS)
    mk_p = mk_p.reshape(bp_n, N_MEM, D_MODEL)
    mv_p = mv_p.reshape(bp_n, N_MEM, D_MODEL)

    zeros_s = jnp.zeros((bp_n, RET_HEADS, RET_HEAD_DIM, RET_HEAD_DIM), F32)
    zeros_h = jnp.zeros((bp_n, POOL_HIST, POOL_WIDTH), F32)
    cat_p, ret_p, pool_p, w_in_b = _mixer(xp2d, w_in, wp_b, bp, ps, gng, gnb, zeros_s, zeros_h,
                                          n_streams=bp_n, t_len=tp, tile_rows=MIXER_TILE, chunk=MIXER_TILE, pos0=0)
    cat_s, ret_s, pool_s = _mixer(xs2d, w_in_b, wp_b, bp, ps, gng, gnb, state_ret[0], state_pool[0],
                                  n_streams=bs_n, t_len=ts, tile_rows=rows_s, chunk=ts, pos0=PAST_LEN)
    (x1,) = _mm_ln([cat_p, cat_s], w_out, [xp2d, xs2d], row(ln1_g), row(ln1_b),
                   n_rows=n_rows, n_tiles_first=tiles_p, name="out_ln1")

    (q_all,) = _matmul(x1, w_mq, BF16, "mem_q")
    o_p = _attention(q_all, mk_p, mv_p, n_streams=bp_n, t_len=tp, q_rows=ROW_TILE, row_offset=0, name="attn_prompt")
    o_s = _attention(q_all, cache_mem_k, cache_mem_v,
                     n_streams=bs_n, t_len=ts, q_rows=ts, row_offset=rows_p, name="attn_sample")
    x2, x2_tiles = _mm_ln([o_p, o_s], w_mo, [x1], row(ln2_g), row(ln2_b),
                          n_rows=n_rows, n_tiles_first=tiles_p, name="mo_ln2", emit_packed=True)

    w_router = jnp.concatenate([w_rg[0], w_re[0].reshape(D_MODEL, N_EXPERTS),
                                jnp.zeros((D_MODEL, LANES - N_GROUPS - N_EXPERTS), F32)], axis=1)
    routed, counts = _router(x2, w_router)
    n_tiles = (2 * n_rows) // MOE_TILE + N_EXPERTS
    tile_start, n_active, slots, gates, tail_start, tail_len = _dispatch_plan(routed, counts)
    x_sorted = _dispatch(slots, tail_start, tail_len, x2_tiles, n_tiles)
    y_sorted = _experts(tile_start, n_active, x_sorted, w1, w3, w2, n_tiles)
    y_p, y_s = _combine(slots, y_sorted, gates, x2, row(ln3_g), row(ln3_b), rows_p)

    kv_shape = (DEPTH, bp_n, N_MEM, MEM_HEADS, MEM_HEAD_DIM)
    return (y_p.reshape(bp_n, tp, D_MODEL), y_s.reshape(bs_n, ts, D_MODEL), pool_p[None], ret_p[None],
            mk_heads.reshape(kv_shape), mv_heads.reshape(kv_shape), pool_s[None], ret_s[None])
```

```python
import functools
import math

import jax
import jax.numpy as jnp
import numpy as np
from jax import lax
from jax.experimental import pallas as pl
from jax.experimental.pallas import tpu as pltpu

F32 = jnp.float32
BF16 = jnp.bfloat16

D_MODEL = 2048
POOL_WIDTH = 1024
POOL_WINDOWS = (2, 4, 8, 16)
POOL_CH = 256
POOL_HIST = 15
RET_WIDTH = 1024
RET_HEADS = 8
RET_HEAD_DIM = 128
IN_WIDTH = POOL_WIDTH + 4 * RET_WIDTH
ROPE_BASE = 10000.0
N_MEM = 256
MEM_HEADS = 4
MEM_HEAD_DIM = 512
N_GROUPS = 4
EXPERTS_PER_GROUP = 8
N_EXPERTS = N_GROUPS * EXPERTS_PER_GROUP
EXPERT_HIDDEN = 512
LN_EPS = 1e-5
GN_EPS = 1e-6
DEPTH = 1
ALPHA = (2.0 * DEPTH) ** 0.25
PAST_LEN = 2048

LANES = 128
HIST_PAD = 16
ROW_TILE = 512
MIXER_TILE = 256
MOE_TILE = 256
COMBINE_TILE = 256
DISPATCH_STEPS = 4
VMEM_LIMIT = 58 * 1024 * 1024

_NEG_INF = float("-inf")


def _const_spec(shape):
    zeros = (0,) * len(shape)
    return pl.BlockSpec(shape, lambda *_: zeros, pipeline_mode=pl.Buffered(1))


def _when(cond, fn):
    if cond is True:
        fn()
    else:
        pl.when(cond)(fn)


def _layer_norm(z, g, b):
    mu = jnp.mean(z, axis=-1, keepdims=True)
    zc = z - mu
    var = jnp.mean(zc * zc, axis=-1, keepdims=True)
    return zc * lax.rsqrt(var + LN_EPS) * g + b


def _silu(a):
    return a * (1.0 / (1.0 + jnp.exp(-a)))


TOKEN_TILE_ROWS = D_MODEL // LANES


def _to_token_tiles(x):
    rows = x.shape[0]
    return x.astype(BF16).reshape(rows, TOKEN_TILE_ROWS, LANES).reshape(rows * TOKEN_TILE_ROWS, LANES)


def _from_token_tiles(tiles):
    rows = tiles.shape[0] // TOKEN_TILE_ROWS
    return tiles.reshape(rows, TOKEN_TILE_ROWS, LANES).reshape(rows, D_MODEL)


def _mix_segment(h_ref, r0, cos_ref, sin_ref, dec_ref, kd_ref, qd_ref, wp_ref, bp_ref, ps_ref, gng_ref, gnb_ref,
                 s0_ref, h0_ref, cat_ref, snew_ref, hnew_ref, s_ref, u_ref,
                 *, seg_len, chunk, pos_start, first, g_chunk):
    hist = jnp.concatenate([jnp.zeros((1, POOL_WIDTH), F32), h0_ref[0]], axis=0)
    if first is True:
        s_ref[...] = s0_ref[0]
        u_ref[0:HIST_PAD, :] = hist
    else:
        s_ref[...] = jnp.where(first, s0_ref[0], s_ref[...])
        u_ref[0:HIST_PAD, :] = jnp.where(first, hist, u_ref[0:HIST_PAD, :])

    u_new = h_ref[pl.ds(r0, seg_len), 0:POOL_WIDTH]
    u_ref[HIST_PAD:HIST_PAD + seg_len, :] = u_new
    pos = (pos_start + lax.broadcasted_iota(jnp.int32, (seg_len, 1), 0)).astype(F32)
    for gi, w in enumerate(POOL_WINDOWS):
        cols = slice(gi * POOL_CH, (gi + 1) * POOL_CH)
        win = u_ref[HIST_PAD:HIST_PAD + seg_len, cols]
        for back in range(1, w):
            win = win + u_ref[HIST_PAD - back:HIST_PAD - back + seg_len, cols]
        cnt = jnp.minimum(float(w), pos + 1.0)
        d = win * (1.0 / cnt) - u_ref[HIST_PAD:HIST_PAD + seg_len, cols]
        pooled = jnp.dot(d.astype(BF16), wp_ref[gi], preferred_element_type=F32) + bp_ref[:, cols]
        cat_ref[:, cols] = (pooled * ps_ref[:, cols]).astype(BF16)

    hnew_ref[0] = u_ref[seg_len + 1:seg_len + HIST_PAD, :]
    u_ref[0:HIST_PAD, :] = u_ref[seg_len:seg_len + HIST_PAD, :]

    scale = RET_HEAD_DIM ** -0.5
    for c in range(seg_len // chunk):
        rows = pl.ds(r0 + c * chunk, chunk)
        trows = slice(c * chunk, (c + 1) * chunk)
        cos_t = cos_ref[trows, :]
        sin_t = sin_ref[trows, :]
        for hd in range(RET_HEADS):
            lo = hd * RET_HEAD_DIM
            hcols = slice(lo, lo + RET_HEAD_DIM)
            q = h_ref[rows, POOL_WIDTH + lo:POOL_WIDTH + lo + RET_HEAD_DIM]
            k = h_ref[rows, POOL_WIDTH + RET_WIDTH + lo:POOL_WIDTH + RET_WIDTH + lo + RET_HEAD_DIM]
            v = h_ref[rows, POOL_WIDTH + 2 * RET_WIDTH + lo:POOL_WIDTH + 2 * RET_WIDTH + lo + RET_HEAD_DIM]
            gate = h_ref[rows, POOL_WIDTH + 3 * RET_WIDTH + lo:POOL_WIDTH + 3 * RET_WIDTH + lo + RET_HEAD_DIM]
            qr = q * cos_t + pltpu.roll(q, RET_HEAD_DIM // 2, 1) * sin_t
            kr = (k * cos_t + pltpu.roll(k, RET_HEAD_DIM // 2, 1) * sin_t) * scale
            vb = v.astype(BF16)
            scores = lax.dot_general(qr.astype(BF16), kr.astype(BF16), (((1,), (1,)), ((), ())),
                                     preferred_element_type=F32) * dec_ref[hd]
            o = jnp.dot(scores.astype(BF16), vb, preferred_element_type=F32)
            s_prev = s_ref[hd]
            o = o + jnp.dot((qr * qd_ref[hd]).astype(BF16), s_prev.astype(BF16), preferred_element_type=F32)
            upd = lax.dot_general((kr * kd_ref[hd]).astype(BF16), vb, (((0,), (0,)), ((), ())),
                                  preferred_element_type=F32)
            s_ref[hd] = g_chunk[hd] * s_prev + upd
            mu = jnp.mean(o, axis=-1, keepdims=True)
            oc = o - mu
            var = jnp.mean(oc * oc, axis=-1, keepdims=True)
            on = oc * lax.rsqrt(var + GN_EPS) * gng_ref[:, hcols] + gnb_ref[:, hcols]
            cat_ref[trows, POOL_WIDTH + lo:POOL_WIDTH + lo + RET_HEAD_DIM] = (on * _silu(gate)).astype(BF16)

    snew_ref[0] = s_ref[...]


def _mixer_segments_kernel(x_ref, w_in_ref, *refs, seg_len, chunk, pos0, g_chunk):
    (*mix_refs, h_ref, s_ref, u_ref) = refs
    j = pl.program_id(0)

    @pl.when(j == 0)
    def _project():
        h_ref[...] = jnp.dot(x_ref[...].astype(BF16), w_in_ref[...], preferred_element_type=F32)

    _mix_segment(h_ref, pl.multiple_of(j * seg_len, seg_len), *mix_refs, s_ref, u_ref,
                 seg_len=seg_len, chunk=chunk, pos_start=pos0, first=True, g_chunk=g_chunk)


def _mixer_pipelined_kernel(x_ref, w_in_hbm, *refs, tiles_per_stream, seg_len, chunk, pos0, g_chunk):
    (*mix_refs, w_bf16_hbm, ha_ref, hb_ref, s_ref, u_ref, wb_ref, wsem, osem) = refs
    g = pl.program_id(0)
    m = jnp.maximum(g - 1, 0)
    first = ((m % tiles_per_stream) == 0) | (g == 0)
    pos_start = pos0 + (m % tiles_per_stream) * seg_len
    w_out_copy = pltpu.make_async_copy(wb_ref, w_bf16_hbm, osem)

    @pl.when(g == 0)
    def _stage_weight():
        stage = (ha_ref, hb_ref)
        chunk_rows = ha_ref.shape[0]
        n_chunks = wb_ref.shape[0] // chunk_rows
        copies = [pltpu.make_async_copy(w_in_hbm.at[0, pl.ds(c * chunk_rows, chunk_rows), :], stage[c % 2],
                                        wsem.at[c % 2]) for c in range(n_chunks)]
        for c in range(min(2, n_chunks)):
            copies[c].start()
        for c in range(n_chunks):
            copies[c].wait()
            wb_ref[pl.ds(c * chunk_rows, chunk_rows), :] = stage[c % 2][...].astype(BF16)
            if c + 2 < n_chunks:
                copies[c + 2].start()
        w_out_copy.start()
        hb_ref[...] = jnp.zeros_like(hb_ref)

    def step(h_write, h_read):
        _mix_segment(h_read, 0, *mix_refs, s_ref, u_ref, seg_len=seg_len, chunk=chunk, pos_start=pos_start,
                     first=first, g_chunk=g_chunk)
        h_write[...] = jnp.dot(x_ref[...].astype(BF16), wb_ref[...], preferred_element_type=F32)

    pl.when(g % 2 == 0)(lambda: step(ha_ref, hb_ref))
    pl.when(g % 2 == 1)(lambda: step(hb_ref, ha_ref))

    @pl.when(g == pl.num_programs(0) - 1)
    def _():
        w_out_copy.wait()


def _retention_tables(chunk, t_len, pos0):
    log_gamma = np.log(1.0 - 2.0 ** (-5.0 - np.arange(RET_HEADS, dtype=np.float64)))
    idx = np.arange(chunk, dtype=np.float64)
    diff = idx[:, None] - idx[None, :]
    dec = np.where(diff >= 0, np.exp(log_gamma[:, None, None] * np.maximum(diff, 0.0)), 0.0)
    kd = np.exp(log_gamma[:, None] * (chunk - 1.0 - idx)[None, :])
    qd = np.exp(log_gamma[:, None] * (idx + 1.0)[None, :])
    kd = np.broadcast_to(kd[:, :, None], (RET_HEADS, chunk, RET_HEAD_DIM))
    qd = np.broadcast_to(qd[:, :, None], (RET_HEADS, chunk, RET_HEAD_DIM))
    half = RET_HEAD_DIM // 2
    freqs = ROPE_BASE ** (-np.arange(half, dtype=np.float64) / half)
    pos = pos0 + np.arange(t_len, dtype=np.float64)
    ang = pos[:, None] * freqs[None, :]
    cos = np.cos(ang)
    sin = np.sin(ang)
    cos_t = np.concatenate([cos, cos], axis=-1)
    sin_t = np.concatenate([-sin, sin], axis=-1)
    g_chunk = tuple(math.exp(math.log(1.0 - 2.0 ** (-5.0 - h)) * chunk) for h in range(RET_HEADS))
    dec, kd, qd, cos_t, sin_t = (jnp.asarray(t, F32) for t in (dec, kd, qd, cos_t, sin_t))
    return dec, kd, qd, cos_t, sin_t, g_chunk


def _mixer(x2d, w_in, wp_b, bp, ps, gng, gnb, s0, h0, *, n_streams, t_len, tile_rows, chunk, pos0):
    rows = n_streams * t_len
    pipelined = t_len > tile_rows
    seg_len = tile_rows if pipelined else t_len
    assert (t_len % tile_rows == 0) if pipelined else (rows == tile_rows)
    assert seg_len % chunk == 0 and t_len >= POOL_HIST
    dec, kd, qd, cos_t, sin_t, g_chunk = _retention_tables(chunk, t_len, pos0)
    state_block = (1, RET_HEADS, RET_HEAD_DIM, RET_HEAD_DIM)
    hist_block = (1, POOL_HIST, POOL_WIDTH)

    if pipelined:
        n_tiles = rows // tile_rows
        tiles_per_stream = t_len // tile_rows
        grid = (n_tiles + 1,)
        mixed = lambda g: jnp.maximum(g - 1, 0)
        x_map = lambda g: (jnp.minimum(g, n_tiles - 1), 0)
        time_map = lambda g: (mixed(g) % tiles_per_stream, 0)
        cat_map = lambda g: (mixed(g), 0)
        state_map = lambda g: (mixed(g) // tiles_per_stream, 0, 0, 0)
        hist_map = lambda g: (mixed(g) // tiles_per_stream, 0, 0)
        kern = functools.partial(_mixer_pipelined_kernel, tiles_per_stream=tiles_per_stream, seg_len=seg_len,
                                 chunk=chunk, pos0=pos0, g_chunk=g_chunk)
        h_scratch = [pltpu.VMEM((tile_rows, IN_WIDTH), F32), pltpu.VMEM((tile_rows, IN_WIDTH), F32)]
        w_spec = pl.BlockSpec(memory_space=pl.ANY)
        extra_out_specs = [pl.BlockSpec(memory_space=pl.ANY)]
        extra_out_shape = [jax.ShapeDtypeStruct((D_MODEL, IN_WIDTH), BF16)]
        extra_scratch = [pltpu.VMEM((D_MODEL, IN_WIDTH), BF16), pltpu.SemaphoreType.DMA((2,)),
                         pltpu.SemaphoreType.DMA(())]
        assert D_MODEL % tile_rows == 0
    else:
        grid = (n_streams,)
        x_map = lambda j: (0, 0)
        time_map = lambda j: (0, 0)
        cat_map = lambda j: (j, 0)
        state_map = lambda j: (j, 0, 0, 0)
        hist_map = lambda j: (j, 0, 0)
        kern = functools.partial(_mixer_segments_kernel, seg_len=seg_len, chunk=chunk, pos0=pos0, g_chunk=g_chunk)
        h_scratch = [pltpu.VMEM((tile_rows, IN_WIDTH), F32)]
        w_spec = _const_spec((D_MODEL, IN_WIDTH))
        extra_out_specs, extra_out_shape, extra_scratch = [], [], []

    return pl.pallas_call(
        kern,
        grid=grid,
        in_specs=[
            pl.BlockSpec((tile_rows, D_MODEL), x_map),
            w_spec,
            pl.BlockSpec((seg_len, RET_HEAD_DIM), time_map),
            pl.BlockSpec((seg_len, RET_HEAD_DIM), time_map),
            _const_spec((RET_HEADS, chunk, chunk)),
            _const_spec((RET_HEADS, chunk, RET_HEAD_DIM)),
            _const_spec((RET_HEADS, chunk, RET_HEAD_DIM)),
            _const_spec((len(POOL_WINDOWS), POOL_CH, POOL_CH)),
            _const_spec((1, POOL_WIDTH)),
            _const_spec((1, POOL_WIDTH)),
            _const_spec((1, RET_WIDTH)),
            _const_spec((1, RET_WIDTH)),
            pl.BlockSpec(state_block, state_map),
            pl.BlockSpec(hist_block, hist_map),
        ],
        out_specs=[
            pl.BlockSpec((seg_len, D_MODEL), cat_map),
            pl.BlockSpec(state_block, state_map),
            pl.BlockSpec(hist_block, hist_map),
        ] + extra_out_specs,
        out_shape=[
            jax.ShapeDtypeStruct((rows, D_MODEL), BF16),
            jax.ShapeDtypeStruct((n_streams, RET_HEADS, RET_HEAD_DIM, RET_HEAD_DIM), F32),
            jax.ShapeDtypeStruct((n_streams, POOL_HIST, POOL_WIDTH), F32),
        ] + extra_out_shape,
        scratch_shapes=h_scratch + [
            pltpu.VMEM((RET_HEADS, RET_HEAD_DIM, RET_HEAD_DIM), F32),
            pltpu.VMEM((HIST_PAD + seg_len, POOL_WIDTH), F32),
        ] + extra_scratch,
        compiler_params=pltpu.CompilerParams(dimension_semantics=("arbitrary",),
                                             vmem_limit_bytes=VMEM_LIMIT),
        name="mixer",
    )(x2d, w_in, cos_t, sin_t, dec, kd, qd, wp_b, bp, ps, gng, gnb, s0, h0)


def _row_sources(arrays, n_tiles_first):
    if len(arrays) == 1:
        return [pl.BlockSpec((ROW_TILE, arrays[0].shape[1]), lambda i: (i, 0))]
    first, second = arrays
    return [
        pl.BlockSpec((ROW_TILE, first.shape[1]), lambda i: (jnp.minimum(i, n_tiles_first - 1), 0)),
        pl.BlockSpec((ROW_TILE, second.shape[1]), lambda i: (jnp.maximum(i - n_tiles_first, 0), 0),
                     pipeline_mode=pl.Buffered(1)),
    ]


def _mm_ln_kernel(*refs, n_a, n_res, n_tiles_first, emit_packed):
    a_refs = refs[:n_a]
    w_ref = refs[n_a]
    res_refs = refs[n_a + 1:n_a + 1 + n_res]
    g_ref, b_ref, o_ref = refs[n_a + 1 + n_res:n_a + 4 + n_res]
    tiles_ref = refs[n_a + 4 + n_res] if emit_packed else None
    wb_ref = refs[-1]
    i = pl.program_id(0)

    @pl.when(i == 0)
    def _():
        wb_ref[...] = w_ref[0].astype(BF16)

    def body(a_ref, res_ref):
        half = ROW_TILE // 2
        for h in range(2):
            rows = slice(h * half, (h + 1) * half)
            acc = jnp.dot(a_ref[rows, :].astype(BF16), wb_ref[...], preferred_element_type=F32)
            out = _layer_norm(ALPHA * res_ref[rows, :] + acc, g_ref[...], b_ref[...])
            o_ref[rows, :] = out
            if emit_packed:
                tile_rows = slice(h * half * TOKEN_TILE_ROWS, (h + 1) * half * TOKEN_TILE_ROWS)
                tiles_ref[tile_rows, :] = _to_token_tiles(out)

    pl.when(i < n_tiles_first)(lambda: body(a_refs[0], res_refs[0]))
    pl.when(i >= n_tiles_first)(lambda: body(a_refs[-1], res_refs[-1]))


def _mm_ln(a_arrays, w, res_arrays, g, b, *, n_rows, n_tiles_first, name, emit_packed=False):
    n_tiles = n_rows // ROW_TILE
    kern = functools.partial(_mm_ln_kernel, n_a=len(a_arrays), n_res=len(res_arrays),
                             n_tiles_first=n_tiles_first, emit_packed=emit_packed)
    out_specs = [pl.BlockSpec((ROW_TILE, D_MODEL), lambda i: (i, 0))]
    out_shape = [jax.ShapeDtypeStruct((n_rows, D_MODEL), F32)]
    if emit_packed:
        out_specs.append(pl.BlockSpec((ROW_TILE * TOKEN_TILE_ROWS, LANES), lambda i: (i, 0)))
        out_shape.append(jax.ShapeDtypeStruct((n_rows * TOKEN_TILE_ROWS, LANES), BF16))
    return pl.pallas_call(
        kern,
        grid=(n_tiles,),
        in_specs=(_row_sources(a_arrays, n_tiles_first) + [_const_spec(w.shape)]
                  + _row_sources(res_arrays, n_tiles_first)
                  + [_const_spec((1, D_MODEL)), _const_spec((1, D_MODEL))]),
        out_specs=out_specs,
        out_shape=out_shape,
        scratch_shapes=[pltpu.VMEM(w.shape[1:], BF16)],
        compiler_params=pltpu.CompilerParams(dimension_semantics=("arbitrary",),
                                             vmem_limit_bytes=VMEM_LIMIT),
        name=name,
    )(*a_arrays, w, *res_arrays, g, b)


def _matmul_kernel(a_ref, w_ref, o_ref, *rest):
    (*split_refs, wb_ref) = rest

    @pl.when(pl.program_id(0) == 0)
    def _():
        wb_ref[...] = w_ref[0].astype(BF16)

    out = jnp.dot(a_ref[...].astype(BF16), wb_ref[...], preferred_element_type=F32).astype(o_ref.dtype)
    o_ref[...] = out
    for ref in split_refs:
        ref[...] = out.reshape(ref.shape)


def _matmul(a, w, out_dtype, name, heads=None):
    n_rows = a.shape[0]
    out_specs = [pl.BlockSpec((ROW_TILE, w.shape[2]), lambda i: (i, 0))]
    out_shape = [jax.ShapeDtypeStruct((n_rows, w.shape[2]), out_dtype)]
    if heads is not None:
        out_specs.append(pl.BlockSpec((ROW_TILE, heads, w.shape[2] // heads), lambda i: (i, 0, 0)))
        out_shape.append(jax.ShapeDtypeStruct((n_rows, heads, w.shape[2] // heads), out_dtype))
    return pl.pallas_call(
        _matmul_kernel,
        grid=(n_rows // ROW_TILE,),
        in_specs=[pl.BlockSpec((ROW_TILE, a.shape[1]), lambda i: (i, 0)), _const_spec(w.shape)],
        out_specs=out_specs,
        out_shape=out_shape,
        scratch_shapes=[pltpu.VMEM(w.shape[1:], BF16)],
        compiler_params=pltpu.CompilerParams(dimension_semantics=("arbitrary",),
                                             vmem_limit_bytes=VMEM_LIMIT),
        name=name,
    )(a, w)


def _attention_kernel(q_ref, k_ref, v_ref, o_ref):
    scale = MEM_HEAD_DIM ** -0.5
    heads_split = len(k_ref.shape) == 5

    def rows_by_model_dim(ref):
        if heads_split:
            return ref[0, 0].astype(BF16).reshape(N_MEM, D_MODEL)
        return ref[0].astype(BF16)

    k = rows_by_model_dim(k_ref)
    v = rows_by_model_dim(v_ref)
    for h in range(MEM_HEADS):
        cols = slice(h * MEM_HEAD_DIM, (h + 1) * MEM_HEAD_DIM)
        s = lax.dot_general(q_ref[:, cols], k[:, cols], (((1,), (1,)), ((), ())),
                            preferred_element_type=F32) * scale
        m = jnp.max(s, axis=-1, keepdims=True)
        p = jnp.exp(s - m)
        p = p * (1.0 / jnp.sum(p, axis=-1, keepdims=True))
        o_ref[:, cols] = jnp.dot(p.astype(BF16), v[:, cols], preferred_element_type=F32).astype(o_ref.dtype)


def _attention(q_all, mem_k, mem_v, *, n_streams, t_len, q_rows, row_offset, name):
    tiles_per_stream = t_len // q_rows
    base = row_offset // q_rows
    if mem_k.ndim == 5:
        kv_spec = pl.BlockSpec((1, 1, N_MEM, MEM_HEADS, MEM_HEAD_DIM), lambda b, t: (0, b, 0, 0, 0))
    else:
        kv_spec = pl.BlockSpec((1, N_MEM, D_MODEL), lambda b, t: (b, 0, 0))
    return pl.pallas_call(
        _attention_kernel,
        grid=(n_streams, tiles_per_stream),
        in_specs=[
            pl.BlockSpec((q_rows, D_MODEL), lambda b, t: (base + b * tiles_per_stream + t, 0)),
            kv_spec,
            kv_spec,
        ],
        out_specs=pl.BlockSpec((q_rows, D_MODEL), lambda b, t: (b * tiles_per_stream + t, 0)),
        out_shape=jax.ShapeDtypeStruct((n_streams * t_len, D_MODEL), BF16),
        compiler_params=pltpu.CompilerParams(dimension_semantics=("arbitrary", "arbitrary"),
                                             vmem_limit_bytes=VMEM_LIMIT),
        name=name,
    )(q_all, mem_k, mem_v)


_EXPERT_LANE0 = N_GROUPS
(_META_LANE1, _META_LANE2, _META_POS1, _META_POS2, _META_GATE1, _META_GATE2) = range(6)
(_OUT_SLOT1, _OUT_SLOT2, _OUT_GATE1, _OUT_GATE2) = range(4)
_MOE_TILE_LOG2 = MOE_TILE.bit_length() - 1
assert 1 << _MOE_TILE_LOG2 == MOE_TILE


_ROUTE_ROWS = 48


def _router_kernel(x_ref, whl_ref, wh_ref, out_ref, counts_ref, carry_ref, meta_ref, start_ref, later_ref):
    phase = pl.program_id(0)
    i = pl.program_id(1)
    n_tok = x_ref.shape[0]
    row = lax.broadcasted_iota(jnp.int32, (_ROUTE_ROWS, n_tok), 0)

    @pl.when((phase == 0) & (i == 0))
    def _():
        carry_ref[...] = jnp.zeros_like(carry_ref)
        later_ref[...] = (lax.broadcasted_iota(jnp.int32, (n_tok, n_tok), 0)
                          < lax.broadcasted_iota(jnp.int32, (n_tok, n_tok), 1)).astype(BF16)

    @pl.when(phase == 0)
    def _route():
        x = x_ref[...]
        xh = x.astype(BF16)
        xl = (x - xh.astype(F32)).astype(BF16)
        both = jnp.dot(xh, whl_ref[...], preferred_element_type=F32)
        logits = both[:, :LANES] + jnp.dot(xl, wh_ref[...], preferred_element_type=F32) + both[:, LANES:]
        lt = logits.T[0:_ROUTE_ROWS, :]

        def first_argmax(vals):
            m = jnp.max(vals, axis=0, keepdims=True)
            idx = jnp.min(jnp.where(vals == m, row, LANES), axis=0, keepdims=True)
            return m, idx

        gl = jnp.where(row < N_GROUPS, lt, _NEG_INF)
        gm, g_idx = first_argmax(gl)
        g_w = 1.0 / jnp.sum(jnp.exp(gl - gm), axis=0, keepdims=True)

        in_group = ((row >= _EXPERT_LANE0) & (row < _EXPERT_LANE0 + N_EXPERTS)
                    & (((row - _EXPERT_LANE0) >> 3) == g_idx))
        el = jnp.where(in_group, lt, _NEG_INF)
        m1, i1 = first_argmax(el)
        z = jnp.sum(jnp.exp(el - m1), axis=0, keepdims=True)
        m2, i2 = first_argmax(jnp.where(row == i1, _NEG_INF, el))
        p1 = 1.0 / z
        p2 = jnp.exp(m2 - m1) / z
        den = p1 + p2
        gate1 = p1 / den * g_w
        gate2 = p2 / den * g_w

        hit1 = row == i1
        hit2 = row == i2
        onehot = (hit1 | hit2).astype(BF16)
        carry = carry_ref[0:_ROUTE_ROWS, :]
        rank = jnp.dot(onehot, later_ref[...], preferred_element_type=F32) + carry
        pos1 = jnp.sum(jnp.where(hit1, rank, 0.0), axis=0, keepdims=True)
        pos2 = jnp.sum(jnp.where(hit2, rank, 0.0), axis=0, keepdims=True)
        carry_ref[0:_ROUTE_ROWS, :] = carry + jnp.sum(onehot.astype(F32), axis=1, keepdims=True)

        zero = jnp.zeros_like(pos1)
        meta_ref[i] = jnp.concatenate([i1.astype(F32), i2.astype(F32), pos1, pos2, gate1, gate2, zero, zero], axis=0)

    @pl.when((phase == 1) & (i == 0))
    def _segment_starts():
        counts_ref[...] = carry_ref[...]
        tiles = ((carry_ref[...].astype(jnp.int32) + (MOE_TILE - 1)) >> _MOE_TILE_LOG2).astype(F32).astype(BF16)
        before = (lax.broadcasted_iota(jnp.int32, (LANES, LANES), 1)
                  < lax.broadcasted_iota(jnp.int32, (LANES, LANES), 0)).astype(BF16)
        tiles_wide = jnp.broadcast_to(tiles, (LANES, LANES))
        start_ref[...] = jnp.dot(before, tiles_wide, preferred_element_type=F32)[:, 0:1] * float(MOE_TILE)

    @pl.when(phase == 1)
    def _slots():
        meta = meta_ref[i]
        starts = start_ref[0:_ROUTE_ROWS, :]

        def start_of(expert_row):
            return jnp.sum(jnp.where(row == expert_row.astype(jnp.int32), starts, 0.0), axis=0, keepdims=True)

        slot1 = start_of(meta[_META_LANE1:_META_LANE1 + 1]) + meta[_META_POS1:_META_POS1 + 1]
        slot2 = start_of(meta[_META_LANE2:_META_LANE2 + 1]) + meta[_META_POS2:_META_POS2 + 1]
        zero = jnp.zeros_like(slot1)
        out_ref[...] = jnp.concatenate([slot1, slot2, meta[_META_GATE1:_META_GATE1 + 1],
                                        meta[_META_GATE2:_META_GATE2 + 1], zero, zero, zero, zero], axis=0)


def _router(x_all, w_router):
    n_rows = x_all.shape[0]
    n_tiles = n_rows // ROW_TILE
    w_hi = w_router.astype(BF16)
    w_lo = (w_router - w_hi.astype(F32)).astype(BF16)
    w_hi_lo = jnp.concatenate([w_hi, w_lo], axis=1)
    return pl.pallas_call(
        _router_kernel,
        grid=(2, n_tiles),
        in_specs=[pl.BlockSpec((ROW_TILE, D_MODEL), lambda p, i: (i * (1 - p) + (n_tiles - 1) * p, 0)),
                  _const_spec((D_MODEL, 2 * LANES)), _const_spec((D_MODEL, LANES))],
        out_specs=[pl.BlockSpec((8, ROW_TILE), lambda p, i: (0, i * p)),
                   pl.BlockSpec((LANES, 1), lambda p, i: (0, 0))],
        out_shape=[jax.ShapeDtypeStruct((8, n_rows), F32), jax.ShapeDtypeStruct((LANES, 1), F32)],
        scratch_shapes=[pltpu.VMEM((LANES, 1), F32), pltpu.VMEM((n_tiles, 8, ROW_TILE), F32),
                        pltpu.VMEM((LANES, 1), F32), pltpu.VMEM((ROW_TILE, ROW_TILE), BF16)],
        compiler_params=pltpu.CompilerParams(dimension_semantics=("arbitrary", "arbitrary"),
                                             vmem_limit_bytes=VMEM_LIMIT),
        name="router",
    )(x_all, w_hi_lo, w_hi)


def _token_rows(first_token, n_tokens):
    return pl.ds(pl.multiple_of(first_token * TOKEN_TILE_ROWS, TOKEN_TILE_ROWS), n_tokens * TOKEN_TILE_ROWS)


_TAIL_BITS = MOE_TILE.bit_length() - 1


def _dispatch_kernel(slots_ref, tail_start_ref, tail_len_ref, x_ref, xs_hbm, zeros_ref, sem, zsem):
    i = pl.program_id(0)
    n_tokens = x_ref.shape[0] // TOKEN_TILE_ROWS
    group = 16

    def issue(j, carry):
        for u in range(group):
            r = j * group + u
            src = x_ref.at[pl.ds(pl.multiple_of(r * TOKEN_TILE_ROWS, TOKEN_TILE_ROWS), TOKEN_TILE_ROWS), :]
            for k in range(2):
                row0 = pl.multiple_of(slots_ref[2 * (i * n_tokens + r) + k], TOKEN_TILE_ROWS)
                pltpu.make_async_copy(src, xs_hbm.at[pl.ds(row0, TOKEN_TILE_ROWS), :], sem).start(priority=k)
        return carry
    lax.fori_loop(0, n_tokens // group, issue, 0)

    def drain(j, carry):
        pltpu.make_async_copy(x_ref.at[pl.ds(0, 2 * group * TOKEN_TILE_ROWS), :],
                              xs_hbm.at[_token_rows(0, 2 * group), :], sem).wait()
        return carry
    lax.fori_loop(0, n_tokens // group, drain, 0)

    def tail_copies(fn):
        def per_expert(ex, carry):
            length = tail_len_ref[ex]
            pos = tail_start_ref[ex]
            for bit in reversed(range(_TAIL_BITS)):
                piece = 1 << bit
                take = (length & piece) != 0
                cp = pltpu.make_async_copy(zeros_ref.at[pl.ds(0, piece * TOKEN_TILE_ROWS), :],
                                           xs_hbm.at[_token_rows(pos, piece), :], zsem)
                pl.when(take)(lambda cp=cp: fn(cp))
                pos = pos + jnp.where(take, piece, 0)
            return carry
        lax.fori_loop(0, N_EXPERTS, per_expert, 0)

    def unused_copies(fn):
        half = zeros_ref.shape[0] // TOKEN_TILE_ROWS
        first_unused = tail_start_ref[N_EXPERTS - 1] + tail_len_ref[N_EXPERTS - 1]

        def per_half(j, carry):
            fn(pltpu.make_async_copy(zeros_ref, xs_hbm.at[_token_rows(first_unused + j * half, half), :], zsem))
            return carry
        lax.fori_loop(0, (xs_hbm.shape[0] // TOKEN_TILE_ROWS - first_unused) // half, per_half, 0)

    @pl.when(i == pl.num_programs(0) - 1)
    def _tails():
        zeros_ref[...] = jnp.zeros_like(zeros_ref)
        tail_copies(lambda cp: cp.start())
        unused_copies(lambda cp: cp.start())
        tail_copies(lambda cp: cp.wait())
        unused_copies(lambda cp: cp.wait())


def _dispatch(slots, tail_start, tail_len, xg, n_tiles_max):
    block_rows = xg.shape[0] // DISPATCH_STEPS
    assert xg.shape[0] % DISPATCH_STEPS == 0 and block_rows % (16 * TOKEN_TILE_ROWS) == 0
    return pl.pallas_call(
        _dispatch_kernel,
        grid_spec=pltpu.PrefetchScalarGridSpec(
            num_scalar_prefetch=3,
            grid=(xg.shape[0] // block_rows,),
            in_specs=[pl.BlockSpec((block_rows, LANES), lambda i, sl, ts, tl: (i, 0))],
            out_specs=pl.BlockSpec(memory_space=pl.ANY),
            scratch_shapes=[pltpu.VMEM((MOE_TILE // 2 * TOKEN_TILE_ROWS, LANES), BF16),
                            pltpu.SemaphoreType.DMA(()), pltpu.SemaphoreType.DMA(())],
        ),
        out_shape=jax.ShapeDtypeStruct((n_tiles_max * MOE_TILE * TOKEN_TILE_ROWS, LANES), BF16),
        compiler_params=pltpu.CompilerParams(dimension_semantics=("arbitrary",),
                                             vmem_limit_bytes=VMEM_LIMIT),
        name="dispatch",
    )(slots, tail_start, tail_len, xg)


_WEIGHT_PIECES = 8
_X_RING = 3


def _experts_kernel(tile_start_ref, n_active_ref, xs_hbm, w1_hbm, w3_hbm, w2_hbm, y_hbm,
                    xbuf, ybuf, w1f, w3f, w2f, w1b, w3b, w2b, gsem, ysem, wsem, *, n_tiles_max):
    e = pl.program_id(0)
    n_active = n_active_ref[0]

    def weight_copies(expert, piece_index, fn):
        slot = expert % 2
        grp, idx = expert // EXPERTS_PER_GROUP, expert % EXPERTS_PER_GROUP
        for hbm, stage in ((w1_hbm, w1f), (w3_hbm, w3f), (w2_hbm, w2f)):
            piece = stage.shape[1] // _WEIGHT_PIECES
            rows = pl.ds(pl.multiple_of(piece_index * piece, piece), piece)
            fn(pltpu.make_async_copy(hbm.at[0, grp, idx, rows, :], stage.at[slot, rows, :], wsem.at[slot]))

    first_tile = tile_start_ref[e]
    n_tiles = tile_start_ref[e + 1] - first_tile

    def start_next_weights(share):
        def start_piece(c, carry):
            weight_copies(e + 1, c, lambda cp: cp.start())
            return carry

        @pl.when(e + 1 < N_EXPERTS)
        def _():
            lax.fori_loop((share * _WEIGHT_PIECES) // (n_tiles + 1), ((share + 1) * _WEIGHT_PIECES) // (n_tiles + 1),
                          start_piece, 0)

    @pl.when(e == 0)
    def _():
        for c in range(_WEIGHT_PIECES):
            weight_copies(e, c, lambda cp: cp.start())

    start_next_weights(0)
    for c in range(_WEIGHT_PIECES):
        weight_copies(e, c, lambda cp: cp.wait())

    def x_copy(tile):
        buf = tile % _X_RING
        return pltpu.make_async_copy(xs_hbm.at[_token_rows(tile * MOE_TILE, MOE_TILE), :], xbuf.at[buf], gsem.at[buf])

    def y_copy(tile, buf):
        return pltpu.make_async_copy(ybuf.at[buf], y_hbm.at[_token_rows(tile * MOE_TILE, MOE_TILE), :], ysem.at[buf])

    @pl.when(e == 0)
    def _():
        for ahead in range(_X_RING - 1):
            pl.when(ahead < n_active)(lambda ahead=ahead: x_copy(ahead).start())

    w1b[...] = w1f[e % 2].astype(BF16)
    w3b[...] = w3f[e % 2].astype(BF16)
    w2b[...] = w2f[e % 2].astype(BF16)

    def tile_body(t, carry):
        g = first_tile + t
        buf = g % 2
        x_copy(g).wait()

        @pl.when(g >= 2)
        def _():
            y_copy(g, buf).wait()

        @pl.when(g + _X_RING - 1 < n_active)
        def _():
            x_copy(g + _X_RING - 1).start()

        start_next_weights(t + 1)

        x = _from_token_tiles(xbuf[g % _X_RING])
        a = jnp.dot(x, w1b[...], preferred_element_type=F32)
        b = jnp.dot(x, w3b[...], preferred_element_type=F32)
        hdn = (_silu(a) * b).astype(BF16)
        y = jnp.dot(hdn, w2b[...], preferred_element_type=F32)
        ybuf[buf] = _to_token_tiles(y)
        y_copy(g, buf).start()
        return carry

    lax.fori_loop(0, n_tiles, tile_body, 0)

    @pl.when(e == N_EXPERTS - 1)
    def _drain():
        @pl.when(n_active >= 2)
        def _():
            y_copy(0, n_active % 2).wait()

        @pl.when(n_active >= 1)
        def _():
            y_copy(0, (n_active + 1) % 2).wait()

        ybuf[0] = jnp.zeros(ybuf.shape[1:], ybuf.dtype)

        def fill(g, carry):
            cp = y_copy(g, 0)
            cp.start()
            cp.wait()
            return carry
        lax.fori_loop(n_active, n_tiles_max, fill, 0)


def _experts(tile_start, n_active, x_sorted, w1, w3, w2, n_tiles_max):
    tile_words = MOE_TILE * TOKEN_TILE_ROWS
    kern = functools.partial(_experts_kernel, n_tiles_max=n_tiles_max)
    return pl.pallas_call(
        kern,
        grid_spec=pltpu.PrefetchScalarGridSpec(
            num_scalar_prefetch=2,
            grid=(N_EXPERTS,),
            in_specs=[pl.BlockSpec(memory_space=pl.ANY)] * 4,
            out_specs=pl.BlockSpec(memory_space=pl.ANY),
            scratch_shapes=[
                pltpu.VMEM((_X_RING, tile_words, LANES), BF16),
                pltpu.VMEM((2, tile_words, LANES), BF16),
                pltpu.VMEM((2, D_MODEL, EXPERT_HIDDEN), F32),
                pltpu.VMEM((2, D_MODEL, EXPERT_HIDDEN), F32),
                pltpu.VMEM((2, EXPERT_HIDDEN, D_MODEL), F32),
                pltpu.VMEM((D_MODEL, EXPERT_HIDDEN), BF16),
                pltpu.VMEM((D_MODEL, EXPERT_HIDDEN), BF16),
                pltpu.VMEM((EXPERT_HIDDEN, D_MODEL), BF16),
                pltpu.SemaphoreType.DMA((_X_RING,)),
                pltpu.SemaphoreType.DMA((2,)),
                pltpu.SemaphoreType.DMA((2,)),
            ],
        ),
        out_shape=jax.ShapeDtypeStruct((n_tiles_max * tile_words, LANES), BF16),
        compiler_params=pltpu.CompilerParams(dimension_semantics=("arbitrary",),
                                             vmem_limit_bytes=VMEM_LIMIT),
        name="experts",
    )(tile_start, n_active, x_sorted, w1, w3, w2)


def _combine_kernel(slot_ref, y_hbm, gates_ref, x_ref, g_ref, b_ref, op_ref, os_ref, ybuf, sem,
                    *, n_tiles, n_tiles_first):
    i = pl.program_id(0)

    tile_words = COMBINE_TILE * TOKEN_TILE_ROWS

    def gather(tile, buf):
        for r in range(COMBINE_TILE):
            tok = tile * COMBINE_TILE + r
            for k in range(2):
                row0 = pl.multiple_of(slot_ref[2 * tok + k], TOKEN_TILE_ROWS)
                pltpu.make_async_copy(y_hbm.at[pl.ds(row0, TOKEN_TILE_ROWS), :],
                                      ybuf.at[buf, k, pl.ds(r * TOKEN_TILE_ROWS, TOKEN_TILE_ROWS), :],
                                      sem.at[buf]).start(priority=k)

    buf = i % 2

    def wait_all():
        for k in range(2):
            pltpu.make_async_copy(y_hbm.at[pl.ds(0, tile_words), :], ybuf.at[buf, k], sem.at[buf]).wait()

    def compute():
        y = (gates_ref[:, 0:1] * _from_token_tiles(ybuf[buf, 0]).astype(F32)
             + gates_ref[:, 1:2] * _from_token_tiles(ybuf[buf, 1]).astype(F32))
        out = _layer_norm(ALPHA * x_ref[...] + y, g_ref[...], b_ref[...])

        @pl.when(i < n_tiles_first)
        def _():
            op_ref[...] = out

        @pl.when(i >= n_tiles_first)
        def _():
            os_ref[...] = out

    @pl.when(i == 0)
    def _():
        gather(0, 0)

    @pl.when(i + 1 < n_tiles)
    def _steady():
        wait_all()
        gather(i + 1, 1 - buf)
        compute()

    @pl.when(i + 1 == n_tiles)
    def _last():
        wait_all()
        compute()


def _combine(slots, y_sorted, gates, x_all, g, b, n_rows_first):
    n_rows = x_all.shape[0]
    n_tiles = n_rows // COMBINE_TILE
    n_first = n_rows_first // COMBINE_TILE
    kern = functools.partial(_combine_kernel, n_tiles=n_tiles, n_tiles_first=n_first)
    return pl.pallas_call(
        kern,
        grid_spec=pltpu.PrefetchScalarGridSpec(
            num_scalar_prefetch=1,
            grid=(n_tiles,),
            in_specs=[
                pl.BlockSpec(memory_space=pl.ANY),
                pl.BlockSpec((COMBINE_TILE, 2), lambda i, sl: (i, 0)),
                pl.BlockSpec((COMBINE_TILE, D_MODEL), lambda i, sl: (i, 0)),
                pl.BlockSpec((1, D_MODEL), lambda i, sl: (0, 0)),
                pl.BlockSpec((1, D_MODEL), lambda i, sl: (0, 0)),
            ],
            out_specs=[
                pl.BlockSpec((COMBINE_TILE, D_MODEL), lambda i, sl: (jnp.minimum(i, n_first - 1), 0)),
                pl.BlockSpec((COMBINE_TILE, D_MODEL), lambda i, sl: (jnp.maximum(i - n_first, 0), 0)),
            ],
            scratch_shapes=[pltpu.VMEM((2, 2, COMBINE_TILE * TOKEN_TILE_ROWS, LANES), BF16),
                            pltpu.SemaphoreType.DMA((2,))],
        ),
        out_shape=[jax.ShapeDtypeStruct((n_rows_first, D_MODEL), F32),
                   jax.ShapeDtypeStruct((n_rows - n_rows_first, D_MODEL), F32)],
        compiler_params=pltpu.CompilerParams(dimension_semantics=("arbitrary",),
                                             vmem_limit_bytes=VMEM_LIMIT),
        name="combine",
    )(slots, y_sorted, gates, x_all, g, b)


def _dispatch_plan(routed, counts):
    slots = routed[_OUT_SLOT1:_OUT_SLOT2 + 1].T.astype(jnp.int32).reshape(-1) * TOKEN_TILE_ROWS
    gates = routed[_OUT_GATE1:_OUT_GATE2 + 1].T
    cnt = counts[_EXPERT_LANE0:_EXPERT_LANE0 + N_EXPERTS, 0].astype(jnp.int32)
    tiles = (cnt + MOE_TILE - 1) // MOE_TILE
    tile_end = jnp.cumsum(tiles)
    tile_start = jnp.concatenate([jnp.zeros((1,), jnp.int32), tile_end]).astype(jnp.int32)
    tail_start = (tile_start[:-1] * MOE_TILE + cnt).astype(jnp.int32)
    tail_len = (tiles * MOE_TILE - cnt).astype(jnp.int32)
    return tile_start, tile_end[-1:].astype(jnp.int32), slots, gates, tail_start, tail_len


def kernel(x_prompt, x_sample, mem_prompt, state_pool, state_ret, cache_mem_k, cache_mem_v, w_in, w_pool, b_pool,
           pool_scale, ret_gn_g, ret_gn_b, w_out, ln1_g, ln1_b, w_mq, w_mk, w_mv, w_mo, ln2_g, ln2_b, w_rg, w_re,
           w1, w3, w2, ln3_g, ln3_b):
    assert w_in.shape[0] == DEPTH == 1
    bp_n, tp, _ = x_prompt.shape
    bs_n, ts, _ = x_sample.shape
    rows_p, rows_s = bp_n * tp, bs_n * ts
    assert rows_s == ROW_TILE and rows_p % ROW_TILE == 0
    n_rows = rows_p + rows_s
    tiles_p = rows_p // ROW_TILE

    wp_b = w_pool[0].astype(BF16)
    bp = b_pool[0].reshape(1, POOL_WIDTH)
    ps = pool_scale[0].reshape(1, POOL_WIDTH)
    gng = ret_gn_g[0].reshape(1, RET_WIDTH)
    gnb = ret_gn_b[0].reshape(1, RET_WIDTH)
    row = lambda p: p[0].reshape(1, D_MODEL)

    xp2d = x_prompt.reshape(rows_p, D_MODEL)
    xs2d = x_sample.reshape(rows_s, D_MODEL)

    mem2d = mem_prompt.reshape(bp_n * N_MEM, D_MODEL)
    mk_p, mk_heads = _matmul(mem2d, w_mk, F32, "mem_k", heads=MEM_HEADS)
    mv_p, mv_heads = _matmul(mem2d, w_mv, F32, "mem_v", heads=MEM_HEADS)
    mk_p = mk_p.reshape(bp_n, N_MEM, D_MODEL)
    mv_p = mv_p.reshape(bp_n, N_MEM, D_MODEL)

    zeros_s = jnp.zeros((bp_n, RET_HEADS, RET_HEAD_DIM, RET_HEAD_DIM), F32)
    zeros_h = jnp.zeros((bp_n, POOL_HIST, POOL_WIDTH), F32)
    cat_p, ret_p, pool_p, w_in_b = _mixer(xp2d, w_in, wp_b, bp, ps, gng, gnb, zeros_s, zeros_h,
                                          n_streams=bp_n, t_len=tp, tile_rows=MIXER_TILE, chunk=MIXER_TILE, pos0=0)
    cat_s, ret_s, pool_s = _mixer(xs2d, w_in_b, wp_b, bp, ps, gng, gnb, state_ret[0], state_pool[0],
                                  n_streams=bs_n, t_len=ts, tile_rows=rows_s, chunk=ts, pos0=PAST_LEN)
    (x1,) = _mm_ln([cat_p, cat_s], w_out, [xp2d, xs2d], row(ln1_g), row(ln1_b),
                   n_rows=n_rows, n_tiles_first=tiles_p, name="out_ln1")

    (q_all,) = _matmul(x1, w_mq, BF16, "mem_q")
    o_p = _attention(q_all, mk_p, mv_p, n_streams=bp_n, t_len=tp, q_rows=ROW_TILE, row_offset=0, name="attn_prompt")
    o_s = _attention(q_all, cache_mem_k, cache_mem_v,
                     n_streams=bs_n, t_len=ts, q_rows=ts, row_offset=rows_p, name="attn_sample")
    x2, x2_tiles = _mm_ln([o_p, o_s], w_mo, [x1], row(ln2_g), row(ln2_b),
                          n_rows=n_rows, n_tiles_first=tiles_p, name="mo_ln2", emit_packed=True)

    w_router = jnp.concatenate([w_rg[0], w_re[0].reshape(D_MODEL, N_EXPERTS),
                                jnp.zeros((D_MODEL, LANES - N_GROUPS - N_EXPERTS), F32)], axis=1)
    routed, counts = _router(x2, w_router)
    n_tiles = (2 * n_rows) // MOE_TILE + N_EXPERTS
    tile_start, n_active, slots, gates, tail_start, tail_len = _dispatch_plan(routed, counts)
    x_sorted = _dispatch(slots, tail_start, tail_len, x2_tiles, n_tiles)
    y_sorted = _experts(tile_start, n_active, x_sorted, w1, w3, w2, n_tiles)
    y_p, y_s = _combine(slots, y_sorted, gates, x2, row(ln3_g), row(ln3_b), rows_p)

    kv_shape = (DEPTH, bp_n, N_MEM, MEM_HEADS, MEM_HEAD_DIM)
    return (y_p.reshape(bp_n, tp, D_MODEL), y_s.reshape(bs_n, ts, D_MODEL), pool_p[None], ret_p[None],
            mk_heads.reshape(kv_shape), mv_heads.reshape(kv_shape), pool_s[None], ret_s[None])
```
